```python
import math
import jax, jax.numpy as jnp
from jax import lax
import numpy as np

D_MODEL = 1024
BATCH = 16
SEQ = 256
DEPTH = 2
DEC_BATCH = 2
DEC_SEQ = 2048
PAST_LEN = 512

GRID_W = 64
ATTN_DIM = D_MODEL // 2
SC_DIM = D_MODEL // 4
SSM_DIM = D_MODEL // 4
MIX_DIM = ATTN_DIM + SC_DIM + SSM_DIM
HEAD_DIM = 64
N_HEADS = ATTN_DIM // HEAD_DIM
KV_HEADS = 2
Q_PER_KV = N_HEADS // KV_HEADS
KV_DIM = KV_HEADS * HEAD_DIM
WINDOW = 128
Q_BLOCK = 128
BAND = Q_BLOCK + 2 * WINDOW
ATTN_SCALE = 1.0 / math.sqrt(HEAD_DIM)
ROPE_BASE = 10000.0
ROPE_FREQS = HEAD_DIM // 4
CONV_W = 3
SSM_CH = 16
SSM_GROUPS = SSM_DIM // SSM_CH
SSM_STATE = 64
N_DIR = 2
DT_MIN = 1e-3
DT_MAX = 1e-1
IN_DIM = ATTN_DIM + 2 * KV_DIM + 3 * SC_DIM + SSM_DIM
IN_SPLITS = (ATTN_DIM, ATTN_DIM + KV_DIM, ATTN_DIM + 2 * KV_DIM,
             ATTN_DIM + 2 * KV_DIM + SC_DIM, ATTN_DIM + 2 * KV_DIM + 2 * SC_DIM,
             ATTN_DIM + 2 * KV_DIM + 3 * SC_DIM)
D_FF = 11 * D_MODEL // 4
RMS_EPS = 1e-6

kernel_name = 'hybrid_prefix_diffusion_step'


def rms_norm(x, g):
    xf = x.astype(jnp.float32)
    y = xf * lax.rsqrt(jnp.mean(xf * xf, axis=-1, keepdims=True) + RMS_EPS)
    return (y * g.astype(jnp.float32)).astype(x.dtype)


def dwconv3(x, w):
    ch = x.shape[-1]
    rhs = jnp.transpose(w)[:, None, :].astype(x.dtype)
    return lax.conv_general_dilated(x, rhs, window_strides=(1,), padding=((1, 1),),
                                    dimension_numbers=('NWC', 'WIO', 'NWC'),
                                    feature_group_count=ch)


def axial_rope(seq_len):
    rows = seq_len // GRID_W
    row = jnp.repeat(jnp.arange(rows, dtype=jnp.float32), GRID_W)
    col = jnp.tile(jnp.arange(GRID_W, dtype=jnp.float32), rows)
    freqs = ROPE_BASE ** (-jnp.arange(ROPE_FREQS, dtype=jnp.float32) / ROPE_FREQS)
    ang = jnp.stack([row[:, None] * freqs, col[:, None] * freqs], axis=1)
    return jnp.cos(ang), jnp.sin(ang)


def apply_rope(x, cos, sin):
    b, l, h, _ = x.shape
    xr = x.reshape(b, l, h, 2, 2, ROPE_FREQS)
    x1, x2 = xr[..., 0, :], xr[..., 1, :]
    c, s = cos[None, :, None], sin[None, :, None]
    out = jnp.stack([x1 * c - x2 * s, x2 * c + x1 * s], axis=-2)
    return out.reshape(b, l, h, HEAD_DIM).astype(x.dtype)


def softmax_with_sink(s, sink):
    sk = sink.astype(jnp.float32).reshape(1, KV_HEADS, Q_PER_KV, 1, 1)
    m = jnp.maximum(jnp.max(s, axis=-1, keepdims=True), sk)
    e = jnp.exp(s - m)
    return e / (jnp.sum(e, axis=-1, keepdims=True) + jnp.exp(sk - m))


def context_attention(q, k, v, sink):
    b, lq = q.shape[0], q.shape[1]
    nb = lq // Q_BLOCK
    qb = q.reshape(b, nb, Q_BLOCK, KV_HEADS, Q_PER_KV, HEAD_DIM).transpose(1, 0, 2, 3, 4, 5)

    def one_block(qblk):
        s = jnp.einsum('bqkrd,bskd->bkrqs', qblk, k).astype(jnp.float32) * ATTN_SCALE
        p = softmax_with_sink(s, sink).astype(v.dtype)
        return jnp.einsum('bkrqs,bskd->bqkrd', p, v)

    o = lax.map(one_block, qb)
    return o.transpose(1, 0, 2, 3, 4, 5).reshape(b, lq, ATTN_DIM)


def latent_attention(q, k, v, ck, cv, sink):
    b, l = q.shape[0], q.shape[1]
    nb = l // Q_BLOCK
    qr = q.reshape(b, l, KV_HEADS, Q_PER_KV, HEAD_DIM)
    pad = ((0, 0), (WINDOW, WINDOW), (0, 0), (0, 0))
    kp, vp = jnp.pad(k, pad), jnp.pad(v, pad)
    r = jnp.arange(Q_BLOCK)[:, None]
    j = jnp.arange(BAND)[None, :]

    def one_block(i):
        start = i * Q_BLOCK
        qblk = lax.dynamic_slice_in_dim(qr, start, Q_BLOCK, axis=1)
        kb = lax.dynamic_slice_in_dim(kp, start, BAND, axis=1)
        vb = lax.dynamic_slice_in_dim(vp, start, BAND, axis=1)
        qpos = start + r
        kpos = start - WINDOW + j
        mask = (jnp.abs(qpos - kpos) <= WINDOW) & (kpos >= 0) & (kpos < l)
        s_band = jnp.einsum('bqkrd,bskd->bkrqs', qblk, kb).astype(jnp.float32) * ATTN_SCALE
        s_band = jnp.where(mask, s_band, -jnp.inf)
        s_ctx = jnp.einsum('bqkrd,bskd->bkrqs', qblk, ck).astype(jnp.float32) * ATTN_SCALE
        p = softmax_with_sink(jnp.concatenate([s_band, s_ctx], axis=-1), sink).astype(v.dtype)
        return (jnp.einsum('bkrqs,bskd->bqkrd', p[..., :BAND], vb)
                + jnp.einsum('bkrqs,bskd->bqkrd', p[..., BAND:], cv))

    o = lax.map(one_block, jnp.arange(nb))
    return o.transpose(1, 0, 2, 3, 4, 5).reshape(b, l, ATTN_DIM)


def _complex_affine_combine(e1, e2):
    a1r, a1i, b1r, b1i = e1
    a2r, a2i, b2r, b2i = e2
    return (a2r * a1r - a2i * a1i, a2r * a1i + a2i * a1r,
            a2r * b1r - a2i * b1i + b2r, a2r * b1i + a2i * b1r + b2i)


def zoh(lam_re, lam_im, log_dt, b_re, b_im):
    dt = jnp.exp(log_dt)[:, None]
    mag = jnp.exp(lam_re * dt)
    ar, ai = mag * jnp.cos(lam_im * dt), mag * jnp.sin(lam_im * dt)
    den = lam_re * lam_re + lam_im * lam_im
    fr = ((ar - 1.0) * lam_re + ai * lam_im) / den
    fi = (ai * lam_re - (ar - 1.0) * lam_im) / den
    bb_re = fr[..., None] * b_re - fi[..., None] * b_im
    bb_im = fr[..., None] * b_im + fi[..., None] * b_re
    return ar, ai, bb_re, bb_im


def ssm_direction(u, lam_re, lam_im, log_dt, b_re, b_im, c_re, c_im, h0, reverse):
    f32 = jnp.float32
    ar, ai, bb_re, bb_im = zoh(lam_re.astype(f32), lam_im.astype(f32), log_dt.astype(f32),
                               b_re.astype(f32), b_im.astype(f32))
    bu_re = jnp.einsum('blgc,gnc->blgn', u, bb_re)
    bu_im = jnp.einsum('blgc,gnc->blgn', u, bb_im)
    if reverse:
        bu_re, bu_im = jnp.flip(bu_re, 1), jnp.flip(bu_im, 1)
    a_re = jnp.broadcast_to(ar, bu_re.shape)
    a_im = jnp.broadcast_to(ai, bu_im.shape)
    acc_re, acc_im, h_re, h_im = lax.associative_scan(
        _complex_affine_combine, (a_re, a_im, bu_re, bu_im), axis=1)
    if h0 is not None:
        h0r, h0i = h0[0][:, None], h0[1][:, None]
        h_re, h_im = (h_re + acc_re * h0r - acc_im * h0i,
                      h_im + acc_re * h0i + acc_im * h0r)
    if reverse:
        h_re, h_im = jnp.flip(h_re, 1), jnp.flip(h_im, 1)
    y = (jnp.einsum('blgn,gcn->blgc', h_re, c_re.astype(f32))
         - jnp.einsum('blgn,gcn->blgc', h_im, c_im.astype(f32)))
    return y, h_re, h_im


def ssm_mixer(su, lp, h0_re=None, h0_im=None):
    f32 = jnp.float32
    b, l, _ = su.shape
    u = su.astype(f32).reshape(b, l, SSM_GROUPS, SSM_CH)
    y = lp['ssm_d'].astype(f32) * u
    fin_re, fin_im = [], []
    for d in range(N_DIR):
        h0 = None if h0_re is None else (h0_re[:, d].astype(f32), h0_im[:, d].astype(f32))
        yd, h_re, h_im = ssm_direction(u, lp['ssm_lam_re'][d], lp['ssm_lam_im'][d],
                                       lp['ssm_log_dt'][d], lp['ssm_b_re'][d], lp['ssm_b_im'][d],
                                       lp['ssm_c_re'][d], lp['ssm_c_im'][d], h0, d == 1)
        y = y + yd
        t = -1 if d == 0 else 0
        fin_re.append(h_re[:, t])
        fin_im.append(h_im[:, t])
    z = jax.nn.gelu(y.reshape(b, l, SSM_DIM))
    out = (z * jax.nn.sigmoid(z @ lp['ssm_w_glu'].astype(f32))).astype(su.dtype)
    if h0_re is None:
        return out, jnp.stack(fin_re, axis=1), jnp.stack(fin_im, axis=1)
    return out


def conv_ffn(h, lp):
    u = dwconv3(h @ lp['ffn_w_up'], lp['ffn_conv'])
    a, g = jnp.split(u, 2, axis=-1)
    return (a * jax.nn.silu(g)) @ lp['ffn_w_down']


def adaln(cvec, lp):
    return jax.nn.silu(cvec) @ lp['w_ada'] + lp['b_ada']


def block(x, mod, lp, ctx=None, rope=None):
    b, l, _ = x.shape
    sh1, sc1, g1, sh2, sc2, g2 = jnp.split(mod, 6, axis=-1)
    h = rms_norm(x, lp['norm_mix']) * (1.0 + sc1) + sh1
    q, k, v, gb, gc, gh, su = jnp.split(h @ lp['w_in'], IN_SPLITS, axis=-1)
    q = q.reshape(b, l, N_HEADS, HEAD_DIM)
    k = k.reshape(b, l, KV_HEADS, HEAD_DIM)
    v = v.reshape(b, l, KV_HEADS, HEAD_DIM)
    if ctx is None:
        attn = context_attention(q, k, v, lp['attn_sink'])
        ssm, st_re, st_im = ssm_mixer(su, lp)
    else:
        ck, cv, h0_re, h0_im = ctx
        cos, sin = rope
        attn = latent_attention(apply_rope(q, cos, sin), apply_rope(k, cos, sin), v,
                                ck, cv, lp['attn_sink'])
        ssm = ssm_mixer(su, lp, h0_re, h0_im)
    conv = gb * dwconv3(gc * gh, lp['sc_conv'])
    mix = jnp.concatenate([attn, conv.astype(attn.dtype), ssm.astype(attn.dtype)], axis=-1) @ lp['w_out']
    x = x + g1 * mix
    h2 = rms_norm(x, lp['norm_ffn']) * (1.0 + sc2) + sh2
    x = x + g2 * conv_ffn(h2, lp)
    if ctx is None:
        return x, k, v, st_re, st_im
    return x


def setup_inputs(seed: int = 0) -> dict:
    key = jax.random.key(seed)
    ks = jax.random.split(key, 32)
    f32 = jnp.float32

    def nrm(k, shape, scale=1.0):
        return jax.random.normal(k, shape, f32) * scale

    gsn = (DEPTH, N_DIR, SSM_GROUPS, SSM_STATE)
    return {
        'x_prompt': nrm(ks[0], (BATCH, SEQ, D_MODEL)),
        'x_sample': nrm(ks[1], (DEC_BATCH, DEC_SEQ, D_MODEL)),
        'cache_k': nrm(ks[2], (DEC_BATCH, DEPTH, PAST_LEN, KV_HEADS, HEAD_DIM)),
        'cache_v': nrm(ks[3], (DEC_BATCH, DEPTH, PAST_LEN, KV_HEADS, HEAD_DIM)),
        'state_ssm_re': nrm(ks[4], (DEC_BATCH, DEPTH, N_DIR, SSM_GROUPS, SSM_STATE), 0.3),
        'state_ssm_im': nrm(ks[5], (DEC_BATCH, DEPTH, N_DIR, SSM_GROUPS, SSM_STATE), 0.3),
        'c': nrm(ks[6], (DEC_BATCH, D_MODEL)),
        'c_ctx': nrm(ks[7], (D_MODEL,)),
        'norm_mix': 1.0 + nrm(ks[8], (DEPTH, D_MODEL), 0.02),
        'norm_ffn': 1.0 + nrm(ks[9], (DEPTH, D_MODEL), 0.02),
        'norm_final': 1.0 + nrm(ks[10], (D_MODEL,), 0.02),
        'w_ada': nrm(ks[11], (DEPTH, D_MODEL, 6 * D_MODEL), 0.5 * D_MODEL ** -0.5),
        'b_ada': nrm(ks[12], (DEPTH, 6 * D_MODEL), 0.02),
        'w_in': nrm(ks[13], (DEPTH, D_MODEL, IN_DIM), D_MODEL ** -0.5),
        'w_out': nrm(ks[14], (DEPTH, MIX_DIM, D_MODEL), MIX_DIM ** -0.5),
        'attn_sink': nrm(ks[15], (DEPTH, N_HEADS), 0.5),
        'sc_conv': nrm(ks[16], (DEPTH, SC_DIM, CONV_W), CONV_W ** -0.5),
        'ssm_lam_re': -0.5 + nrm(ks[17], gsn, 0.01),
        'ssm_lam_im': jnp.pi * jnp.arange(SSM_STATE, dtype=f32) + nrm(ks[18], gsn, 0.01),
        'ssm_log_dt': jax.random.uniform(ks[19], (DEPTH, N_DIR, SSM_GROUPS), f32,
                                         minval=math.log(DT_MIN), maxval=math.log(DT_MAX)),
        'ssm_b_re': nrm(ks[20], gsn + (SSM_CH,), (2 * SSM_CH) ** -0.5),
        'ssm_b_im': nrm(ks[21], gsn + (SSM_CH,), (2 * SSM_CH) ** -0.5),
        'ssm_c_re': nrm(ks[22], (DEPTH, N_DIR, SSM_GROUPS, SSM_CH, SSM_STATE), SSM_STATE ** -0.5),
        'ssm_c_im': nrm(ks[23], (DEPTH, N_DIR, SSM_GROUPS, SSM_CH, SSM_STATE), SSM_STATE ** -0.5),
        'ssm_d': nrm(ks[24], (DEPTH, SSM_GROUPS, SSM_CH)),
        'ssm_w_glu': nrm(ks[25], (DEPTH, SSM_DIM, SSM_DIM), SSM_DIM ** -0.5),
        'ffn_w_up': nrm(ks[26], (DEPTH, D_MODEL, 2 * D_FF), D_MODEL ** -0.5),
        'ffn_conv': nrm(ks[27], (DEPTH, 2 * D_FF, CONV_W), CONV_W ** -0.5),
        'ffn_w_down': nrm(ks[28], (DEPTH, D_FF, D_MODEL), D_FF ** -0.5),
    }


def reference(x_prompt, x_sample, cache_k, cache_v, state_ssm_re, state_ssm_im, c, c_ctx,
              norm_mix, norm_ffn, norm_final, w_ada, b_ada, w_in, w_out, attn_sink, sc_conv,
              ssm_lam_re, ssm_lam_im, ssm_log_dt, ssm_b_re, ssm_b_im, ssm_c_re, ssm_c_im,
              ssm_d, ssm_w_glu, ffn_w_up, ffn_conv, ffn_w_down):
    stacked = (('norm_mix', norm_mix), ('norm_ffn', norm_ffn), ('w_ada', w_ada),
               ('b_ada', b_ada), ('w_in', w_in), ('w_out', w_out), ('attn_sink', attn_sink),
               ('sc_conv', sc_conv), ('ssm_lam_re', ssm_lam_re), ('ssm_lam_im', ssm_lam_im),
               ('ssm_log_dt', ssm_log_dt), ('ssm_b_re', ssm_b_re), ('ssm_b_im', ssm_b_im),
               ('ssm_c_re', ssm_c_re), ('ssm_c_im', ssm_c_im), ('ssm_d', ssm_d),
               ('ssm_w_glu', ssm_w_glu), ('ffn_w_up', ffn_w_up), ('ffn_conv', ffn_conv),
               ('ffn_w_down', ffn_w_down))
    rope = axial_rope(x_sample.shape[1])
    xp, xs = x_prompt, x_sample
    ks_out, vs_out, sre_out, sim_out = [], [], [], []
    for l in range(DEPTH):
        lp = {name: arr[l] for name, arr in stacked}
        mod_ctx = adaln(c_ctx, lp)[None, None, :]
        xp, k_l, v_l, sre_l, sim_l = block(xp, mod_ctx, lp)
        ks_out.append(k_l)
        vs_out.append(v_l)
        sre_out.append(sre_l)
        sim_out.append(sim_l)
        mod_lat = adaln(c, lp)[:, None, :]
        ctx = (cache_k[:, l], cache_v[:, l], state_ssm_re[:, l], state_ssm_im[:, l])
        xs = block(xs, mod_lat, lp, ctx=ctx, rope=rope)
    y_prompt = rms_norm(xp, norm_final)
    y_sample = rms_norm(xs, norm_final)
    new_cache_k = jnp.stack(ks_out, axis=1)
    new_cache_v = jnp.stack(vs_out, axis=1)
    new_state_ssm_re = jnp.stack(sre_out, axis=1)
    new_state_ssm_im = jnp.stack(sim_out, axis=1)
    return (y_prompt, y_sample, new_cache_k, new_cache_v, new_state_ssm_re, new_state_ssm_im)
```

```python
import functools
import math

import jax
import jax.numpy as jnp
from jax import lax
from jax.experimental import pallas as pl
from jax.experimental.pallas import tpu as pltpu

F32 = jnp.float32
BF16 = jnp.bfloat16

D = 1024
DEPTH = 2
GRID_W = 64
ATTN_DIM = 512
SC_DIM = 256
SSM_DIM = 256
HEAD_DIM = 64
N_HEADS = 8
KV_HEADS = 2
KV_DIM = 128
WINDOW = 128
Q_BLOCK = 128
ROPE_BASE = 10000.0
ROPE_FREQS = 16
SSM_CH = 16
SSM_GROUPS = 16
SSM_STATE = 64
SSM_N = SSM_GROUPS * SSM_STATE
IN_DIM = ATTN_DIM + 2 * KV_DIM + 3 * SC_DIM + SSM_DIM
D_FF = 2816
RMS_EPS = 1e-6
NEG_BIG = -1e30

LANES = 128
SCAN_ROWS = 16
SCAN_LEN = 256
SCAN_TT = 32
SCAN_LW = 256
VMEM_LIMIT = 56 * 1024 * 1024


def _cparams(sem):
    return pltpu.CompilerParams(dimension_semantics=sem, vmem_limit_bytes=VMEM_LIMIT)


def _sigmoid(x):
    return 1.0 / (1.0 + jnp.exp(-x))


def _rms_mod(x, g, scale, shift):
    y = x * lax.rsqrt(jnp.mean(x * x, axis=-1, keepdims=True) + RMS_EPS) * g
    return y * (1.0 + scale) + shift


def _split_bf16(v):
    hi = v.astype(BF16)
    lo = (v - hi.astype(F32)).astype(BF16)
    return hi, lo


def _dot(a, b):
    return jnp.dot(a, b, preferred_element_type=F32)


def _dot_t(a, b):
    return lax.dot_general(a, b, (((1,), (1,)), ((), ())), preferred_element_type=F32)


def _adaln_kernel(c_ref, w_ref, b_ref, o_ref):
    c = c_ref[...]
    s = c * _sigmoid(c)
    s_hi, s_lo = _split_bf16(s)
    w_hi, w_lo = _split_bf16(w_ref[...])
    o_ref[...] = _dot(s_hi, w_hi) + _dot(s_lo, w_hi) + _dot(s_hi, w_lo) + b_ref[...]


def _adaln(cs, w_ada, b_ada):
    tn = 1024
    return pl.pallas_call(
        _adaln_kernel,
        grid=(DEPTH, 6 * D // tn),
        in_specs=[
            pl.BlockSpec((8, D), lambda l, j: (0, 0)),
            pl.BlockSpec((None, D, tn), lambda l, j: (l, 0, j)),
            pl.BlockSpec((None, 1, tn), lambda l, j: (l, 0, j)),
        ],
        out_specs=pl.BlockSpec((None, 8, tn), lambda l, j: (l, 0, j)),
        out_shape=jax.ShapeDtypeStruct((DEPTH, 8, 6 * D), F32),
        compiler_params=_cparams(("parallel", "parallel")),
        name="adaln",
    )(cs, w_ada, b_ada.reshape(DEPTH, 1, 6 * D))


def _inproj_kernel(x_ref, mod_ref, g_ref, w_ref, *rest, rope):
    if rope:
        cos_ref, sneg_ref, spos_ref, q_ref, kd_ref, vd_ref, g3_ref, su_ref = rest
    else:
        q_ref, kd_ref, vd_ref, kv_ref, g3_ref, su_ref = rest
    m = mod_ref[...]
    h = _rms_mod(x_ref[...], g_ref[...], m[:, D:2 * D], m[:, 0:D]).astype(BF16)
    acc = _dot(h, w_ref[...])
    k = acc[:, 512:640]
    v = acc[:, 640:768]
    if rope:
        c, sn, sp = cos_ref[...], sneg_ref[...], spos_ref[...]

        def rot(t):
            return t * c + pltpu.roll(t, LANES - ROPE_FREQS, 1) * sn + pltpu.roll(t, ROPE_FREQS, 1) * sp

        for p in range(4):
            q_ref[:, LANES * p:LANES * (p + 1)] = rot(acc[:, LANES * p:LANES * (p + 1)]).astype(BF16)
        k = rot(k)
    else:
        q_ref[...] = acc[:, 0:512].astype(BF16)
        kv_ref[...] = acc[:, 512:768]
    lo = lax.broadcasted_iota(jnp.int32, k.shape, 1) < HEAD_DIM
    kr = pltpu.roll(k, HEAD_DIM, 1)
    vr = pltpu.roll(v, HEAD_DIM, 1)
    kd_ref[:, 0:LANES] = jnp.where(lo, k, kr).astype(BF16)
    kd_ref[:, LANES:2 * LANES] = jnp.where(lo, kr, k).astype(BF16)
    vd_ref[:, 0:LANES] = jnp.where(lo, v, vr).astype(BF16)
    vd_ref[:, LANES:2 * LANES] = jnp.where(lo, vr, v).astype(BF16)
    g3_ref[...] = acc[:, 768:1536]
    su_ref[...] = acc[:, 1536:1792]


def _inproj(x, mod3, g, w, rope_tabs, tokens_per_mod, seq_len, tm=512):
    t = x.shape[0]
    rope = rope_tabs is not None
    tiles_per_mod = tokens_per_mod // tm
    tiles_per_seq = seq_len // tm
    if rope:
        mod_map = lambda i: (1 + i // tiles_per_mod, 0, 0)
    else:
        mod_map = lambda i: (0, 0, 0)
    row = lambda i: (i, 0)
    in_specs = [
        pl.BlockSpec((tm, D), row),
        pl.BlockSpec((None, 1, 6 * D), mod_map),
        pl.BlockSpec((1, D), lambda i: (0, 0)),
        pl.BlockSpec((D, IN_DIM), lambda i: (0, 0)),
    ]
    args = [x, mod3, g, w]
    if rope:
        in_specs += [pl.BlockSpec((tm, LANES), lambda i: (i % tiles_per_seq, 0))] * 3
        args += list(rope_tabs)
    out_shape = [jax.ShapeDtypeStruct((t, 512), BF16), jax.ShapeDtypeStruct((t, 256), BF16),
                 jax.ShapeDtypeStruct((t, 256), BF16)]
    out_specs = [pl.BlockSpec((tm, 512), row), pl.BlockSpec((tm, 256), row), pl.BlockSpec((tm, 256), row)]
    if not rope:
        out_shape.append(jax.ShapeDtypeStruct((t, 256), F32))
        out_specs.append(pl.BlockSpec((tm, 256), row))
    out_shape += [jax.ShapeDtypeStruct((t, 768), F32), jax.ShapeDtypeStruct((t, 256), F32)]
    out_specs += [pl.BlockSpec((tm, 768), row), pl.BlockSpec((tm, 256), row)]
    return pl.pallas_call(
        functools.partial(_inproj_kernel, rope=rope),
        grid=(t // tm,),
        in_specs=in_specs,
        out_specs=out_specs,
        out_shape=out_shape,
        compiler_params=_cparams(("parallel",)),
        name="inproj_lat" if rope else "inproj_ctx",
    )(*args)


def _half_masks():
    lane = lax.broadcasted_iota(jnp.int32, (1, LANES), 1)
    lo = (lane < HEAD_DIM).astype(BF16)
    return lo, 1.0 - lo


def _attn_ctx_kernel(sink_ref, q_ref, kd_ref, vd_ref, o_ref):
    halves = _half_masks()
    for kvh in range(KV_HEADS):
        sl = slice(LANES * kvh, LANES * (kvh + 1))
        kk, vv = kd_ref[:, sl], vd_ref[:, sl]
        kms = [kk * hm for hm in halves]
        vms = [vv * hm for hm in halves]
        for pp in range(2):
            p = 2 * kvh + pp
            qp = q_ref[:, LANES * p:LANES * (p + 1)]
            out = None
            for half in range(2):
                sk = sink_ref[2 * p + half]
                s = _dot_t(qp, kms[half])
                m = jnp.maximum(jnp.max(s, axis=-1, keepdims=True), sk)
                e = jnp.exp(s - m)
                den = jnp.sum(e, axis=-1, keepdims=True) + jnp.exp(sk - m)
                o = _dot(e.astype(BF16), vms[half]) / den
                out = o if out is None else out + o
            o_ref[:, LANES * p:LANES * (p + 1)] = out.astype(BF16)


def _attn_ctx(sink, q, kd, vd, seq_len):
    t = q.shape[0]
    row = lambda b: (b, 0)
    return pl.pallas_call(
        _attn_ctx_kernel,
        grid=(t // seq_len,),
        in_specs=[
            pl.BlockSpec(memory_space=pltpu.SMEM),
            pl.BlockSpec((seq_len, 512), row),
            pl.BlockSpec((seq_len, 256), row),
            pl.BlockSpec((seq_len, 256), row),
        ],
        out_specs=pl.BlockSpec((seq_len, 512), row),
        out_shape=jax.ShapeDtypeStruct((t, 512), BF16),
        compiler_params=_cparams(("parallel",)),
        name="attn_ctx",
    )(sink, q, kd, vd)


def _attn_lat_kernel(sink_ref, q_ref, kp_ref, kc_ref, kn_ref, vp_ref, vc_ref, vn_ref, ck_ref, cv_ref, o_ref):
    i = pl.program_id(1)
    nb = pl.num_programs(1)
    r = lax.broadcasted_iota(jnp.int32, (Q_BLOCK, Q_BLOCK), 0)
    j = lax.broadcasted_iota(jnp.int32, (Q_BLOCK, Q_BLOCK), 1)
    mask_prev = jnp.logical_and(j >= r, i > 0)
    mask_next = jnp.logical_and(j <= r, i < nb - 1)
    halves = _half_masks()
    for kvh in range(KV_HEADS):
        sl = slice(LANES * kvh, LANES * (kvh + 1))
        ks = [ref[:, sl] for ref in (kp_ref, kc_ref, kn_ref, ck_ref)]
        vs = [ref[:, sl] for ref in (vp_ref, vc_ref, vn_ref, cv_ref)]
        kms = [[k * hm for k in ks] for hm in halves]
        vms = [[v * hm for v in vs] for hm in halves]
        for pp in range(2):
            p = 2 * kvh + pp
            qp = q_ref[:, LANES * p:LANES * (p + 1)]
            out = None
            for half in range(2):
                sk = sink_ref[2 * p + half]
                s_prev = jnp.where(mask_prev, _dot_t(qp, kms[half][0]), NEG_BIG)
                s_cur = _dot_t(qp, kms[half][1])
                s_next = jnp.where(mask_next, _dot_t(qp, kms[half][2]), NEG_BIG)
                s_ctx = _dot_t(qp, kms[half][3])
                ss = (s_prev, s_cur, s_next, s_ctx)
                m = sk
                for s in ss:
                    m = jnp.maximum(jnp.max(s, axis=-1, keepdims=True), m)
                den = jnp.exp(sk - m)
                acc = None
                for s, vm in zip(ss, vms[half]):
                    e = jnp.exp(s - m)
                    den = den + jnp.sum(e, axis=-1, keepdims=True)
                    d = _dot(e.astype(BF16), vm)
                    acc = d if acc is None else acc + d
                o = acc / den
                out = o if out is None else out + o
            o_ref[:, LANES * p:LANES * (p + 1)] = out.astype(BF16)


def _attn_lat(sink, q, kd, vd, ckd, cvd, batch, seq_len):
    nb = seq_len // Q_BLOCK
    past = ckd.shape[1]
    cur = lambda b, i: (b * nb + i, 0)
    prev = lambda b, i: (b * nb + jnp.maximum(i - 1, 0), 0)
    nxt = lambda b, i: (b * nb + jnp.minimum(i + 1, nb - 1), 0)
    kvspec = lambda f: pl.BlockSpec((Q_BLOCK, 256), f)
    cspec = pl.BlockSpec((None, past, 256), lambda b, i: (b, 0, 0))
    return pl.pallas_call(
        _attn_lat_kernel,
        grid=(batch, nb),
        in_specs=[
            pl.BlockSpec(memory_space=pltpu.SMEM),
            pl.BlockSpec((Q_BLOCK, 512), cur),
            kvspec(prev), kvspec(cur), kvspec(nxt),
            kvspec(prev), kvspec(cur), kvspec(nxt),
            cspec, cspec,
        ],
        out_specs=pl.BlockSpec((Q_BLOCK, 512), cur),
        out_shape=jax.ShapeDtypeStruct((batch * seq_len, 512), BF16),
        compiler_params=_cparams(("parallel", "parallel")),
        name="attn_lat",
    )(sink, q, kd, kd, kd, vd, vd, vd, ckd, cvd)


def _scan_kernel(lre_ref, lim_ref, ldt_ref, bre_ref, bim_ref, cblk_ref, *rest, emit_y, chain):
    rest = list(rest)
    if chain:
        s0_ref, ez_ref = rest[:2]
        rest = rest[2:]
    suf_ref, sub_ref = rest[:2]
    rest = rest[2:]
    if emit_y:
        y_refs = rest[:2]
        rest = rest[2:]
    fin_ref, a_scr, bb_scr, h_scr, bu_scr = rest
    i = pl.program_id(0)
    tt = SCAN_TT
    n = SSM_N

    @pl.when(i == 0)
    def _prologue():
        for d in range(2):
            lr, li = lre_ref[d], lim_ref[d]
            dt = jnp.exp(ldt_ref[d])
            mag = jnp.exp(lr * dt)
            ar, ai = mag * jnp.cos(li * dt), mag * jnp.sin(li * dt)
            den = lr * lr + li * li
            fr = ((ar - 1.0) * lr + ai * li) / den
            fi = (ai * lr - (ar - 1.0) * li) / den
            bre, bim = bre_ref[d], bim_ref[d]
            bb_scr[d, :, 0:n] = (fr * bre - fi * bim).astype(BF16)
            bb_scr[d, :, n:2 * n] = (fr * bim + fi * bre).astype(BF16)
            a_scr[d, 0] = jnp.broadcast_to(ar, (SCAN_ROWS, n))
            a_scr[d, 1] = jnp.broadcast_to(ai, (SCAN_ROWS, n))
            if chain:
                pr, pi_ = ar, ai
                for _ in range(8):
                    pr, pi_ = pr * pr - pi_ * pi_, 2.0 * pr * pi_
                chunks = SCAN_ROWS // 2
                for b in range(2):
                    hr = s0_ref[d, b:b + 1, 0:n]
                    hi = s0_ref[d, b:b + 1, n:2 * n]
                    order = range(chunks) if d == 0 else range(chunks - 1, -1, -1)
                    for c in order:
                        rw = b * chunks + c
                        h_scr[d, rw:rw + 1, 0:n] = hr
                        h_scr[d, rw:rw + 1, n:2 * n] = hi
                        er = ez_ref[d, rw:rw + 1, 0:n]
                        ei = ez_ref[d, rw:rw + 1, n:2 * n]
                        hr, hi = pr * hr - pi_ * hi + er, pr * hi + pi_ * hr + ei
            else:
                h_scr[d] = jnp.zeros((SCAN_ROWS, 2 * n), F32)

    for d in range(2):
        su_ref = suf_ref if d == 0 else sub_ref
        u = su_ref[...].reshape(SCAN_ROWS * tt, SSM_DIM).astype(BF16)
        bu = _dot(u, bb_scr[d])
        ntile = n // LANES
        for c in range(2 * ntile):
            bu_scr[d, c] = bu[:, c * LANES:(c + 1) * LANES]
        group = SCAN_LW // LANES
        for c0 in range(0, ntile, group):
            tiles = range(c0, c0 + group)
            ar = [a_scr[d, 0, :, c * LANES:(c + 1) * LANES] for c in tiles]
            ai = [a_scr[d, 1, :, c * LANES:(c + 1) * LANES] for c in tiles]
            hr = [h_scr[d, :, c * LANES:(c + 1) * LANES] for c in tiles]
            hi = [h_scr[d, :, n + c * LANES:n + (c + 1) * LANES] for c in tiles]
            steps = range(tt) if d == 0 else range(tt - 1, -1, -1)
            for s in steps:
                rows = pl.ds(s, SCAN_ROWS, stride=tt)
                for k, c in enumerate(tiles):
                    br, bi = bu_scr[d, c, rows, :], bu_scr[d, ntile + c, rows, :]
                    hr[k], hi[k] = ar[k] * hr[k] - ai[k] * hi[k] + br, ar[k] * hi[k] + ai[k] * hr[k] + bi
                    bu_scr[d, c, rows, :] = hr[k]
                    bu_scr[d, ntile + c, rows, :] = hi[k]
            for k, c in enumerate(tiles):
                h_scr[d, :, c * LANES:(c + 1) * LANES] = hr[k]
                h_scr[d, :, n + c * LANES:n + (c + 1) * LANES] = hi[k]
        if emit_y:
            hs = jnp.concatenate([bu_scr[d, c].astype(BF16) for c in range(2 * ntile)], axis=1)
            y = _dot(hs, cblk_ref[d])
            y_refs[d][...] = y.reshape(SCAN_ROWS, tt, SSM_DIM)

    @pl.when(i == pl.num_programs(0) - 1)
    def _final():
        fin_ref[...] = h_scr[...]


def _scan(sp, su3, emit_y, s0=None, ez=None):
    chain = s0 is not None
    nt = SCAN_LEN // SCAN_TT
    n = SSM_N
    full3 = lambda shape: pl.BlockSpec(shape, lambda i: (0, 0, 0))
    in_specs = [full3((2, 1, n)), full3((2, 1, n)), full3((2, 1, n)),
                full3((2, SSM_DIM, n)), full3((2, SSM_DIM, n)), full3((2, 2 * n, SSM_DIM))]
    args = [sp["lam_re"], sp["lam_im"], sp["log_dt"], sp["b_re"], sp["b_im"], sp["cblk"]]
    if chain:
        in_specs += [full3((2, 2, 2 * n)), full3((2, SCAN_ROWS, 2 * n))]
        args += [s0, ez]
    tblk = (SCAN_ROWS, SCAN_TT, SSM_DIM)
    fwd = lambda i: (0, i, 0)
    bwd = lambda i: (0, nt - 1 - i, 0)
    in_specs += [pl.BlockSpec(tblk, fwd), pl.BlockSpec(tblk, bwd)]
    args += [su3, su3]
    out_shape, out_specs = [], []
    if emit_y:
        yshape = jax.ShapeDtypeStruct((SCAN_ROWS, SCAN_LEN, SSM_DIM), F32)
        out_shape += [yshape, yshape]
        out_specs += [pl.BlockSpec(tblk, fwd), pl.BlockSpec(tblk, bwd)]
    out_shape.append(jax.ShapeDtypeStruct((2, SCAN_ROWS, 2 * n), F32))
    out_specs.append(full3((2, SCAN_ROWS, 2 * n)))
    return pl.pallas_call(
        functools.partial(_scan_kernel, emit_y=emit_y, chain=chain),
        grid=(nt,),
        in_specs=in_specs,
        out_specs=out_specs,
        out_shape=out_shape,
        scratch_shapes=[
            pltpu.VMEM((2, 2, SCAN_ROWS, n), F32),
            pltpu.VMEM((2, SSM_DIM, 2 * n), BF16),
            pltpu.VMEM((2, SCAN_ROWS, 2 * n), F32),
            pltpu.VMEM((2, 2 * n // LANES, SCAN_ROWS * SCAN_TT, LANES), F32),
        ],
        compiler_params=_cparams(("arbitrary",)),
        name="scan_chain" if chain else ("scan_y" if emit_y else "scan_state"),
    )(*args)


def _shifted(scr_ref, off, rows, pos, seq_len, w):
    mid = scr_ref[off:off + rows, :]
    up = jnp.where(pos != 0, scr_ref[off - 1:off - 1 + rows, :], 0.0)
    dn = jnp.where(pos != seq_len - 1, scr_ref[off + 1:off + 1 + rows, :], 0.0)
    return w[0:1, :] * up + w[1:2, :] * mid + w[2:3, :] * dn


def _mix_kernel(x_ref, mod_ref, attn_ref, g3_ref, *rest, halo, tm, seq_len):
    rest = list(rest)
    if halo:
        gp_ref, gn_ref = rest[:2]
        rest = rest[2:]
    (su_ref, yf_ref, yb_ref, wc_ref, dsk_ref, wglu_ref, woa_ref, woc_ref, wos_ref, gn2_ref,
     x1_ref, h2_ref, z_scr) = rest
    i = pl.program_id(0)
    m = mod_ref[...]
    g1, sh2, sc2 = m[:, 2 * D:3 * D], m[:, 3 * D:4 * D], m[:, 4 * D:5 * D]
    pos = (i * tm + lax.broadcasted_iota(jnp.int32, (tm, 1), 0)) % seq_len

    g3 = g3_ref[...]
    gb = g3[:, 0:SC_DIM]
    z_scr[8:8 + tm, :] = g3[:, SC_DIM:2 * SC_DIM] * g3[:, 2 * SC_DIM:3 * SC_DIM]
    if halo:
        z_scr[0:8, :] = gp_ref[:, SC_DIM:2 * SC_DIM] * gp_ref[:, 2 * SC_DIM:3 * SC_DIM]
        z_scr[8 + tm:16 + tm, :] = gn_ref[:, SC_DIM:2 * SC_DIM] * gn_ref[:, 2 * SC_DIM:3 * SC_DIM]
    else:
        z_scr[0:8, :] = jnp.zeros((8, SC_DIM), F32)
        z_scr[8 + tm:16 + tm, :] = jnp.zeros((8, SC_DIM), F32)
    conv = gb * _shifted(z_scr, 8, tm, pos, seq_len, wc_ref[...])

    y = dsk_ref[...] * su_ref[...] + yf_ref[...] + yb_ref[...]
    zz = 0.5 * y * (1.0 + jnp.tanh(math.sqrt(2.0 / math.pi) * (y + 0.044715 * (y * y * y))))
    ssm = zz * _sigmoid(_dot(zz.astype(BF16), wglu_ref[...]))

    mix = (_dot(attn_ref[...], woa_ref[...]) + _dot(conv.astype(BF16), woc_ref[...])
           + _dot(ssm.astype(BF16), wos_ref[...]))
    x1 = x_ref[...] + g1 * mix
    x1_ref[...] = x1
    h2_ref[...] = _rms_mod(x1, gn2_ref[...], sc2, sh2).astype(BF16)


def _mix(x, mod3, attn, g3, su, yf, yb, lp, lat, tokens_per_mod, seq_len, tm=256):
    t = x.shape[0]
    halo = seq_len > tm
    tiles_per_mod = tokens_per_mod // tm
    mod_map = (lambda i: (1 + i // tiles_per_mod, 0, 0)) if lat else (lambda i: (0, 0, 0))
    row = lambda i: (i, 0)
    const = lambda i: (0, 0)
    in_specs = [pl.BlockSpec((tm, D), row), pl.BlockSpec((None, 1, 6 * D), mod_map),
                pl.BlockSpec((tm, 512), row), pl.BlockSpec((tm, 768), row)]
    args = [x, mod3, attn, g3]
    if halo:
        r8 = tm // 8
        in_specs += [pl.BlockSpec((8, 768), lambda i: (jnp.maximum(i * r8 - 1, 0), 0)),
                     pl.BlockSpec((8, 768), lambda i: (jnp.minimum((i + 1) * r8, t // 8 - 1), 0))]
        args += [g3, g3]
    in_specs += [pl.BlockSpec((tm, 256), row)] * 3
    args += [su, yf, yb]
    in_specs += [pl.BlockSpec((3, SC_DIM), const), pl.BlockSpec((1, SSM_DIM), const),
                 pl.BlockSpec((SSM_DIM, SSM_DIM), const), pl.BlockSpec((ATTN_DIM, D), const),
                 pl.BlockSpec((SC_DIM, D), const), pl.BlockSpec((SSM_DIM, D), const),
                 pl.BlockSpec((1, D), const)]
    args += [lp["sc_conv"], lp["ssm_d"], lp["w_glu"], lp["wo_a"], lp["wo_c"], lp["wo_s"], lp["norm_ffn"]]
    return pl.pallas_call(
        functools.partial(_mix_kernel, halo=halo, tm=tm, seq_len=seq_len),
        grid=(t // tm,),
        in_specs=in_specs,
        out_specs=[pl.BlockSpec((tm, D), row), pl.BlockSpec((tm, D), row)],
        out_shape=[jax.ShapeDtypeStruct((t, D), F32), jax.ShapeDtypeStruct((t, D), BF16)],
        scratch_shapes=[pltpu.VMEM((tm + 16, SC_DIM), F32)],
        compiler_params=_cparams(("parallel",)),
        name="mix_lat" if lat else "mix_ctx",
    )(*args)


def _ffn_kernel(h_ref, *rest, halo, tm, seq_len, final):
    rest = list(rest)
    if halo:
        hp_ref, hn_ref = rest[:2]
        rest = rest[2:]
    x1_ref, mod_ref, wa_ref, wg_ref, ca_ref, cg_ref, wd_ref = rest[:7]
    rest = rest[7:]
    if final:
        gf_ref = rest[0]
        rest = rest[1:]
    o_ref, acc_scr, ua_scr, ug_scr = rest[:4]
    i = pl.program_id(0)
    j = pl.program_id(1)
    off = 16 if halo else 8
    pos = (i * tm + lax.broadcasted_iota(jnp.int32, (tm, 1), 0)) % seq_len

    if halo:
        hcat_scr = rest[4]

        @pl.when(j == 0)
        def _stage():
            hcat_scr[0:16, :] = hp_ref[...]
            hcat_scr[16:16 + tm, :] = h_ref[...]
            hcat_scr[16 + tm:32 + tm, :] = hn_ref[...]

        hh = hcat_scr[...]
        ua_scr[...] = _dot(hh, wa_ref[...])
        ug_scr[...] = _dot(hh, wg_ref[...])
    else:
        hh = h_ref[...]
        tf = ua_scr.shape[1]
        for scr in (ua_scr, ug_scr):
            scr[0:8, :] = jnp.zeros((8, tf), F32)
            scr[8 + tm:16 + tm, :] = jnp.zeros((8, tf), F32)
        ua_scr[8:8 + tm, :] = _dot(hh, wa_ref[...])
        ug_scr[8:8 + tm, :] = _dot(hh, wg_ref[...])

    a = _shifted(ua_scr, off, tm, pos, seq_len, ca_ref[...])
    g = _shifted(ug_scr, off, tm, pos, seq_len, cg_ref[...])
    act = (a * (g * _sigmoid(g))).astype(BF16)
    part = _dot(act, wd_ref[...])

    @pl.when(j == 0)
    def _init():
        acc_scr[...] = part

    @pl.when(j != 0)
    def _accum():
        acc_scr[...] += part

    @pl.when(j == pl.num_programs(1) - 1)
    def _finish():
        g2 = mod_ref[...][:, 5 * D:6 * D]
        x2 = x1_ref[...] + g2 * acc_scr[...]
        if final:
            x2 = x2 * lax.rsqrt(jnp.mean(x2 * x2, axis=-1, keepdims=True) + RMS_EPS) * gf_ref[...]
        o_ref[...] = x2


def _ffn(h2, x1, mod3, lp, lat, tokens_per_mod, seq_len, final_g, tm=512, tf=1408):
    t = h2.shape[0]
    halo = seq_len > tm
    final = final_g is not None
    nj = D_FF // tf
    tiles_per_mod = tokens_per_mod // tm
    mod_map = (lambda i, j: (1 + i // tiles_per_mod, 0, 0)) if lat else (lambda i, j: (0, 0, 0))
    row = lambda i, j: (i, 0)
    in_specs = [pl.BlockSpec((tm, D), row)]
    args = [h2]
    if halo:
        r16 = tm // 16
        in_specs += [pl.BlockSpec((16, D), lambda i, j: (jnp.maximum(i * r16 - 1, 0), 0)),
                     pl.BlockSpec((16, D), lambda i, j: (jnp.minimum((i + 1) * r16, t // 16 - 1), 0))]
        args += [h2, h2]
    in_specs += [pl.BlockSpec((tm, D), row), pl.BlockSpec((None, 1, 6 * D), mod_map),
                 pl.BlockSpec((D, tf), lambda i, j: (0, j)), pl.BlockSpec((D, tf), lambda i, j: (0, j + nj)),
                 pl.BlockSpec((3, tf), lambda i, j: (0, j)), pl.BlockSpec((3, tf), lambda i, j: (0, j + nj)),
                 pl.BlockSpec((tf, D), lambda i, j: (j, 0))]
    args += [x1, mod3, lp["w_up"], lp["w_up"], lp["ffn_conv"], lp["ffn_conv"], lp["w_down"]]
    if final:
        in_specs.append(pl.BlockSpec((1, D), lambda i, j: (0, 0)))
        args.append(final_g)
    urows = tm + (32 if halo else 16)
    scratch = [pltpu.VMEM((tm, D), F32), pltpu.VMEM((urows, tf), F32), pltpu.VMEM((urows, tf), F32)]
    if halo:
        scratch.append(pltpu.VMEM((tm + 32, D), BF16))
    return pl.pallas_call(
        functools.partial(_ffn_kernel, halo=halo, tm=tm, seq_len=seq_len, final=final),
        grid=(t // tm, nj),
        in_specs=in_specs,
        out_specs=pl.BlockSpec((tm, D), row),
        out_shape=jax.ShapeDtypeStruct((t, D), F32),
        scratch_shapes=scratch,
        compiler_params=_cparams(("parallel", "arbitrary")),
        name="ffn_lat" if lat else "ffn_ctx",
    )(*args)


def _rope_tables(seq_len):
    rows = seq_len // GRID_W
    row = jnp.repeat(jnp.arange(rows, dtype=F32), GRID_W)
    col = jnp.tile(jnp.arange(GRID_W, dtype=F32), rows)
    freqs = ROPE_BASE ** (-jnp.arange(ROPE_FREQS, dtype=F32) / ROPE_FREQS)
    ang = jnp.stack([row[:, None] * freqs, col[:, None] * freqs], axis=1)
    cos, sin = jnp.cos(ang), jnp.sin(ang)
    zero = jnp.zeros_like(sin)
    c64 = jnp.stack([cos, cos], axis=2).reshape(seq_len, HEAD_DIM)
    sneg64 = jnp.stack([-sin, zero], axis=2).reshape(seq_len, HEAD_DIM)
    spos64 = jnp.stack([zero, sin], axis=2).reshape(seq_len, HEAD_DIM)
    return tuple(jnp.tile(tb, (1, 2)) for tb in (c64, sneg64, spos64))


def _layer_params(l, w_in, w_out, norm_mix, norm_ffn, attn_sink, sc_conv, ssm_lam_re, ssm_lam_im,
                  ssm_log_dt, ssm_b_re, ssm_b_im, ssm_c_re, ssm_c_im, ssm_d, ssm_w_glu,
                  ffn_w_up, ffn_conv, ffn_w_down):
    qscale = jnp.concatenate([jnp.full((ATTN_DIM,), 1.0 / math.sqrt(HEAD_DIM), F32),
                              jnp.ones((IN_DIM - ATTN_DIM,), F32)])
    wo = w_out[l].astype(BF16)
    eye = jnp.eye(SSM_GROUPS, dtype=F32)
    bdiag = lambda b: jnp.einsum("dgnc,gh->dgchn", b, eye).reshape(2, SSM_DIM, SSM_N)
    cdiag = lambda c: jnp.einsum("dgcn,gh->dgnhc", c, eye).reshape(2, SSM_N, SSM_DIM)
    scan = {
        "lam_re": ssm_lam_re[l].reshape(2, 1, SSM_N),
        "lam_im": ssm_lam_im[l].reshape(2, 1, SSM_N),
        "log_dt": jnp.repeat(ssm_log_dt[l], SSM_STATE, axis=-1).reshape(2, 1, SSM_N),
        "b_re": bdiag(ssm_b_re[l]),
        "b_im": bdiag(ssm_b_im[l]),
        "cblk": jnp.concatenate([cdiag(ssm_c_re[l]), -cdiag(ssm_c_im[l])], axis=1).astype(BF16),
    }
    return {
        "w_in": (w_in[l] * qscale).astype(BF16),
        "norm_mix": norm_mix[l].reshape(1, D),
        "norm_ffn": norm_ffn[l].reshape(1, D),
        "sink": attn_sink[l],
        "sc_conv": sc_conv[l].T,
        "ssm_d": ssm_d[l].reshape(1, SSM_DIM),
        "w_glu": ssm_w_glu[l].astype(BF16),
        "wo_a": wo[0:ATTN_DIM], "wo_c": wo[ATTN_DIM:ATTN_DIM + SC_DIM], "wo_s": wo[ATTN_DIM + SC_DIM:],
        "w_up": ffn_w_up[l].astype(BF16),
        "ffn_conv": ffn_conv[l].T,
        "w_down": ffn_w_down[l].astype(BF16),
        "scan": scan,
    }


def _dup_heads(c):
    b, s = c.shape[0], c.shape[1]
    return jnp.broadcast_to(c[:, :, :, None, :], (b, s, KV_HEADS, 2, HEAD_DIM)).reshape(b, s, 256).astype(BF16)


def kernel(x_prompt, x_sample, cache_k, cache_v, state_ssm_re, state_ssm_im, c, c_ctx, norm_mix, norm_ffn, norm_final, w_ada, b_ada, w_in, w_out, attn_sink, sc_conv, ssm_lam_re, ssm_lam_im, ssm_log_dt, ssm_b_re, ssm_b_im, ssm_c_re, ssm_c_im, ssm_d, ssm_w_glu, ffn_w_up, ffn_conv, ffn_w_down):
    batch, seq = x_prompt.shape[0], x_prompt.shape[1]
    dec_batch, dec_seq = x_sample.shape[0], x_sample.shape[1]
    assert batch == SCAN_ROWS and seq == SCAN_LEN
    assert dec_batch * (dec_seq // SCAN_LEN) == SCAN_ROWS and dec_batch == 2

    cs = jnp.concatenate([c_ctx[None, :], c, jnp.zeros((8 - 1 - dec_batch, D), F32)], axis=0)
    mods = _adaln(cs, w_ada, b_ada)
    rope_tabs = _rope_tables(dec_seq)
    gfin = norm_final.reshape(1, D)

    xp = x_prompt.reshape(batch * seq, D)
    xs = x_sample.reshape(dec_batch * dec_seq, D)
    ks_out, vs_out, sre_out, sim_out = [], [], [], []
    for l in range(DEPTH):
        lp = _layer_params(l, w_in, w_out, norm_mix, norm_ffn, attn_sink, sc_conv, ssm_lam_re, ssm_lam_im,
                           ssm_log_dt, ssm_b_re, ssm_b_im, ssm_c_re, ssm_c_im, ssm_d, ssm_w_glu,
                           ffn_w_up, ffn_conv, ffn_w_down)
        mod3 = mods[l].reshape(8, 1, 6 * D)
        last = gfin if l == DEPTH - 1 else None

        q, kd, vd, kv, g3, su = _inproj(xp, mod3, lp["norm_mix"], lp["w_in"], None, batch * seq, seq)
        attn = _attn_ctx(lp["sink"], q, kd, vd, seq)
        yf, yb, fin = _scan(lp["scan"], su.reshape(SCAN_ROWS, SCAN_LEN, SSM_DIM), True)
        x1, h2 = _mix(xp, mod3, attn, g3, su, yf.reshape(-1, SSM_DIM), yb.reshape(-1, SSM_DIM), lp,
                      False, batch * seq, seq)
        xp = _ffn(h2, x1, mod3, lp, False, batch * seq, seq, last)
        ks_out.append(kv[:, 0:KV_DIM].reshape(batch, seq, KV_HEADS, HEAD_DIM))
        vs_out.append(kv[:, KV_DIM:].reshape(batch, seq, KV_HEADS, HEAD_DIM))
        fin = fin.reshape(2, batch, 2, SSM_GROUPS, SSM_STATE)
        sre_out.append(jnp.transpose(fin[:, :, 0], (1, 0, 2, 3)))
        sim_out.append(jnp.transpose(fin[:, :, 1], (1, 0, 2, 3)))

        q, kd, vd, g3, su = _inproj(xs, mod3, lp["norm_mix"], lp["w_in"], rope_tabs, dec_seq, dec_seq)
        attn = _attn_lat(lp["sink"], q, kd, vd, _dup_heads(cache_k[:, l]), _dup_heads(cache_v[:, l]),
                         dec_batch, dec_seq)
        su3 = su.reshape(SCAN_ROWS, SCAN_LEN, SSM_DIM)
        (ez,) = _scan(lp["scan"], su3, False)
        s0 = jnp.concatenate([state_ssm_re[:, l].reshape(dec_batch, 2, SSM_N),
                              state_ssm_im[:, l].reshape(dec_batch, 2, SSM_N)], axis=-1)
        yf, yb, _ = _scan(lp["scan"], su3, True, s0=jnp.transpose(s0, (1, 0, 2)), ez=ez)
        x1, h2 = _mix(xs, mod3, attn, g3, su, yf.reshape(-1, SSM_DIM), yb.reshape(-1, SSM_DIM), lp,
                      True, dec_seq, dec_seq)
        xs = _ffn(h2, x1, mod3, lp, True, dec_seq, dec_seq, last)

    return (xp.reshape(batch, seq, D), xs.reshape(dec_batch, dec_seq, D),
            jnp.stack(ks_out, axis=1), jnp.stack(vs_out, axis=1),
            jnp.stack(sre_out, axis=1), jnp.stack(sim_out, axis=1))
```

```python
import functools
import math

import jax
import jax.numpy as jnp
from jax import lax
from jax.experimental import pallas as pl
from jax.experimental.pallas import tpu as pltpu

F32 = jnp.float32
BF16 = jnp.bfloat16

D = 1024
DEPTH = 2
GRID_W = 64
ATTN_DIM = 512
SC_DIM = 256
SSM_DIM = 256
HEAD_DIM = 64
N_HEADS = 8
KV_HEADS = 2
KV_DIM = 128
WINDOW = 128
Q_BLOCK = 128
ROPE_BASE = 10000.0
ROPE_FREQS = 16
SSM_CH = 16
SSM_GROUPS = 16
SSM_STATE = 64
SSM_N = SSM_GROUPS * SSM_STATE
IN_DIM = ATTN_DIM + 2 * KV_DIM + 3 * SC_DIM + SSM_DIM
D_FF = 2816
RMS_EPS = 1e-6
NEG_BIG = -1e30

LANES = 128
SCAN_ROWS = 16
SCAN_LEN = 256
SCAN_TT = 32
SCAN_LW = 256
VMEM_LIMIT = 56 * 1024 * 1024


def _cparams(sem):
    return pltpu.CompilerParams(dimension_semantics=sem, vmem_limit_bytes=VMEM_LIMIT)


def _sigmoid(x):
    return 1.0 / (1.0 + jnp.exp(-x))


def _rms_mod(x, g, scale, shift):
    y = x * lax.rsqrt(jnp.mean(x * x, axis=-1, keepdims=True) + RMS_EPS) * g
    return y * (1.0 + scale) + shift


def _split_bf16(v):
    hi = v.astype(BF16)
    lo = (v - hi.astype(F32)).astype(BF16)
    return hi, lo


def _dot(a, b):
    return jnp.dot(a, b, preferred_element_type=F32)


def _dot_t(a, b):
    return lax.dot_general(a, b, (((1,), (1,)), ((), ())), preferred_element_type=F32)


def _adaln_kernel(c_ref, w_ref, b_ref, o_ref):
    c = c_ref[...]
    s = c * _sigmoid(c)
    s_hi, s_lo = _split_bf16(s)
    w_hi, w_lo = _split_bf16(w_ref[...])
    o_ref[...] = _dot(s_hi, w_hi) + _dot(s_lo, w_hi) + _dot(s_hi, w_lo) + b_ref[...]


def _adaln(cs, w_ada, b_ada):
    tn = 1024
    return pl.pallas_call(
        _adaln_kernel,
        grid=(DEPTH, 6 * D // tn),
        in_specs=[
            pl.BlockSpec((8, D), lambda l, j: (0, 0)),
            pl.BlockSpec((None, D, tn), lambda l, j: (l, 0, j)),
            pl.BlockSpec((None, 1, tn), lambda l, j: (l, 0, j)),
        ],
        out_specs=pl.BlockSpec((None, 8, tn), lambda l, j: (l, 0, j)),
        out_shape=jax.ShapeDtypeStruct((DEPTH, 8, 6 * D), F32),
        compiler_params=_cparams(("parallel", "parallel")),
        name="adaln",
    )(cs, w_ada, b_ada.reshape(DEPTH, 1, 6 * D))


def _inproj_kernel(x_ref, mod_ref, g_ref, w_ref, *rest, rope):
    if rope:
        cos_ref, sneg_ref, spos_ref, q_ref, kd_ref, vd_ref, g3_ref, su_ref = rest
    else:
        q_ref, kd_ref, vd_ref, kv_ref, g3_ref, su_ref = rest
    m = mod_ref[...]
    h = _rms_mod(x_ref[...], g_ref[...], m[:, D:2 * D], m[:, 0:D]).astype(BF16)
    acc = _dot(h, w_ref[...])
    k = acc[:, 512:640]
    v = acc[:, 640:768]
    if rope:
        c, sn, sp = cos_ref[...], sneg_ref[...], spos_ref[...]

        def rot(t):
            return t * c + pltpu.roll(t, LANES - ROPE_FREQS, 1) * sn + pltpu.roll(t, ROPE_FREQS, 1) * sp

        for p in range(4):
            q_ref[:, LANES * p:LANES * (p + 1)] = rot(acc[:, LANES * p:LANES * (p + 1)]).astype(BF16)
        k = rot(k)
    else:
        q_ref[...] = acc[:, 0:512].astype(BF16)
        kv_ref[...] = acc[:, 512:768]
    lo = lax.broadcasted_iota(jnp.int32, k.shape, 1) < HEAD_DIM
    kr = pltpu.roll(k, HEAD_DIM, 1)
    vr = pltpu.roll(v, HEAD_DIM, 1)
    for ref, t, tr in ((kd_ref, k, kr), (vd_ref, v, vr)):
        ref[:, 0:LANES] = jnp.where(lo, t, 0.0).astype(BF16)
        ref[:, LANES:2 * LANES] = jnp.where(lo, 0.0, tr).astype(BF16)
        ref[:, 2 * LANES:3 * LANES] = jnp.where(lo, tr, 0.0).astype(BF16)
        ref[:, 3 * LANES:4 * LANES] = jnp.where(lo, 0.0, t).astype(BF16)
    g3_ref[...] = acc[:, 768:1536]
    for r in range(acc.shape[0] // SCAN_LEN):
        su_ref[:, r * SSM_DIM:(r + 1) * SSM_DIM] = acc[r * SCAN_LEN:(r + 1) * SCAN_LEN, 1536:1792]


def _inproj(x, mod3, g, w, rope_tabs, tokens_per_mod, seq_len, tm=512):
    t = x.shape[0]
    rope = rope_tabs is not None
    tiles_per_mod = tokens_per_mod // tm
    tiles_per_seq = seq_len // tm
    if rope:
        mod_map = lambda i: (1 + i // tiles_per_mod, 0, 0)
    else:
        mod_map = lambda i: (0, 0, 0)
    row = lambda i: (i, 0)
    in_specs = [
        pl.BlockSpec((tm, D), row),
        pl.BlockSpec((None, 1, 6 * D), mod_map),
        pl.BlockSpec((1, D), lambda i: (0, 0)),
        pl.BlockSpec((D, IN_DIM), lambda i: (0, 0)),
    ]
    args = [x, mod3, g, w]
    if rope:
        in_specs += [pl.BlockSpec((tm, LANES), lambda i: (i % tiles_per_seq, 0))] * 3
        args += list(rope_tabs)
    out_shape = [jax.ShapeDtypeStruct((t, 512), BF16)] * 3
    out_specs = [pl.BlockSpec((tm, 512), row)] * 3
    if not rope:
        out_shape.append(jax.ShapeDtypeStruct((t, 256), F32))
        out_specs.append(pl.BlockSpec((tm, 256), row))
    out_shape += [jax.ShapeDtypeStruct((t, 768), F32),
                  jax.ShapeDtypeStruct((SCAN_LEN, t // SCAN_LEN * SSM_DIM), F32)]
    out_specs += [pl.BlockSpec((tm, 768), row),
                  pl.BlockSpec((SCAN_LEN, tm // SCAN_LEN * SSM_DIM), lambda i: (0, i))]
    return pl.pallas_call(
        functools.partial(_inproj_kernel, rope=rope),
        grid=(t // tm,),
        in_specs=in_specs,
        out_specs=out_specs,
        out_shape=out_shape,
        compiler_params=_cparams(("parallel",)),
        name="inproj_lat" if rope else "inproj_ctx",
    )(*args)


def _attn_group(sink_ref, q_ref, kvh, keys, vals, bias, nq):
    q2 = jnp.concatenate([q_ref[:, LANES * (2 * kvh + pp):LANES * (2 * kvh + pp + 1)] for pp in range(2)], axis=0)
    top = lax.broadcasted_iota(jnp.int32, (2 * nq, 1), 0) < nq
    acc = None
    for half in range(2):
        sk = jnp.where(top, sink_ref[4 * kvh + half], sink_ref[4 * kvh + 2 + half])
        s = _dot_t(q2, keys[half])
        if bias is not None:
            s = s + bias
        m = jnp.maximum(jnp.max(s, axis=-1, keepdims=True), sk)
        e = jnp.exp(s - m)
        den = jnp.sum(e, axis=-1, keepdims=True) + jnp.exp(sk - m)
        o = _dot(e.astype(BF16), vals[half]) / den
        acc = o if acc is None else acc + o
    return acc[0:nq], acc[nq:2 * nq]


def _attn_ctx_kernel(sink_ref, q_ref, kd_ref, vd_ref, o_ref):
    nq = q_ref.shape[0]
    for kvh in range(KV_HEADS):
        tiles = [slice(LANES * (2 * kvh + h), LANES * (2 * kvh + h + 1)) for h in range(2)]
        o0, o1 = _attn_group(sink_ref, q_ref, kvh, [kd_ref[:, t] for t in tiles], [vd_ref[:, t] for t in tiles],
                             None, nq)
        o_ref[:, LANES * 2 * kvh:LANES * (2 * kvh + 1)] = o0.astype(BF16)
        o_ref[:, LANES * (2 * kvh + 1):LANES * (2 * kvh + 2)] = o1.astype(BF16)


def _attn_ctx(sink, q, kd, vd, seq_len):
    t = q.shape[0]
    row = lambda b: (b, 0)
    return pl.pallas_call(
        _attn_ctx_kernel,
        grid=(t // seq_len,),
        in_specs=[
            pl.BlockSpec(memory_space=pltpu.SMEM),
            pl.BlockSpec((seq_len, 512), row),
            pl.BlockSpec((seq_len, 512), row),
            pl.BlockSpec((seq_len, 512), row),
        ],
        out_specs=pl.BlockSpec((seq_len, 512), row),
        out_shape=jax.ShapeDtypeStruct((t, 512), BF16),
        compiler_params=_cparams(("parallel",)),
        name="attn_ctx",
    )(sink, q, kd, vd)


def _attn_lat_kernel(sink_ref, q_ref, kp_ref, kc_ref, kn_ref, vp_ref, vc_ref, vn_ref, ck_ref, cv_ref, o_ref):
    i = pl.program_id(1)
    nb = pl.num_programs(1)
    r = lax.broadcasted_iota(jnp.int32, (Q_BLOCK, Q_BLOCK), 0)
    j = lax.broadcasted_iota(jnp.int32, (Q_BLOCK, Q_BLOCK), 1)
    past = ck_ref.shape[0]
    zero = jnp.zeros((Q_BLOCK, Q_BLOCK), F32)
    bias = jnp.concatenate([
        jnp.where(jnp.logical_and(j >= r, i > 0), 0.0, NEG_BIG), zero,
        jnp.where(jnp.logical_and(j <= r, i < nb - 1), 0.0, NEG_BIG), jnp.zeros((Q_BLOCK, past), F32)], axis=1)
    bias = jnp.concatenate([bias, bias], axis=0)
    for kvh in range(KV_HEADS):
        tiles = [slice(LANES * (2 * kvh + h), LANES * (2 * kvh + h + 1)) for h in range(2)]
        keys = [jnp.concatenate([ref[:, t] for ref in (kp_ref, kc_ref, kn_ref, ck_ref)], axis=0) for t in tiles]
        vals = [jnp.concatenate([ref[:, t] for ref in (vp_ref, vc_ref, vn_ref, cv_ref)], axis=0) for t in tiles]
        o0, o1 = _attn_group(sink_ref, q_ref, kvh, keys, vals, bias, Q_BLOCK)
        o_ref[:, LANES * 2 * kvh:LANES * (2 * kvh + 1)] = o0.astype(BF16)
        o_ref[:, LANES * (2 * kvh + 1):LANES * (2 * kvh + 2)] = o1.astype(BF16)


def _attn_lat(sink, q, kd, vd, ckd, cvd, batch, seq_len):
    nb = seq_len // Q_BLOCK
    past = ckd.shape[1]
    cur = lambda b, i: (b * nb + i, 0)
    prev = lambda b, i: (b * nb + jnp.maximum(i - 1, 0), 0)
    nxt = lambda b, i: (b * nb + jnp.minimum(i + 1, nb - 1), 0)
    kvspec = lambda f: pl.BlockSpec((Q_BLOCK, 512), f)
    cspec = pl.BlockSpec((None, past, 512), lambda b, i: (b, 0, 0))
    return pl.pallas_call(
        _attn_lat_kernel,
        grid=(batch, nb),
        in_specs=[
            pl.BlockSpec(memory_space=pltpu.SMEM),
            pl.BlockSpec((Q_BLOCK, 512), cur),
            kvspec(prev), kvspec(cur), kvspec(nxt),
            kvspec(prev), kvspec(cur), kvspec(nxt),
            cspec, cspec,
        ],
        out_specs=pl.BlockSpec((Q_BLOCK, 512), cur),
        out_shape=jax.ShapeDtypeStruct((batch * seq_len, 512), BF16),
        compiler_params=_cparams(("parallel", "parallel")),
        name="attn_lat",
    )(sink, q, kd, kd, kd, vd, vd, vd, ckd, cvd)


def _scan_kernel(lre_ref, lim_ref, ldt_ref, bre_ref, bim_ref, cblk_ref, *rest, emit_y, chain):
    rest = list(rest)
    if chain:
        s0_ref, ez_ref = rest[:2]
        rest = rest[2:]
    suf_ref, sub_ref = rest[:2]
    rest = rest[2:]
    if emit_y:
        y_refs = rest[:2]
        rest = rest[2:]
    fin_ref, a_scr, bb_scr, h_scr, bu_scr, u_scr = rest[:6]
    if emit_y:
        y_scr = rest[6]
    i = pl.program_id(0)
    tt = SCAN_TT
    n = SSM_N

    @pl.when(i == 0)
    def _prologue():
        for d in range(2):
            lr, li = lre_ref[d], lim_ref[d]
            dt = jnp.exp(ldt_ref[d])
            mag = jnp.exp(lr * dt)
            ar, ai = mag * jnp.cos(li * dt), mag * jnp.sin(li * dt)
            den = lr * lr + li * li
            fr = ((ar - 1.0) * lr + ai * li) / den
            fi = (ai * lr - (ar - 1.0) * li) / den
            bre, bim = bre_ref[d], bim_ref[d]
            bb_scr[d, :, 0:n] = (fr * bre - fi * bim).astype(BF16)
            bb_scr[d, :, n:2 * n] = (fr * bim + fi * bre).astype(BF16)
            a_scr[d, 0] = jnp.broadcast_to(ar, (8, n))
            a_scr[d, 1] = jnp.broadcast_to(ai, (8, n))
            if chain:
                pr, pi_ = ar, ai
                for _ in range(8):
                    pr, pi_ = pr * pr - pi_ * pi_, 2.0 * pr * pi_
                chunks = SCAN_ROWS // 2
                for b in range(2):
                    hr = s0_ref[d, b:b + 1, 0:n]
                    hi = s0_ref[d, b:b + 1, n:2 * n]
                    order = range(chunks) if d == 0 else range(chunks - 1, -1, -1)
                    for c in order:
                        rw = b * chunks + c
                        h_scr[d, rw:rw + 1, 0:n] = hr
                        h_scr[d, rw:rw + 1, n:2 * n] = hi
                        er = ez_ref[d, rw:rw + 1, 0:n]
                        ei = ez_ref[d, rw:rw + 1, n:2 * n]
                        hr, hi = pr * hr - pi_ * hi + er, pr * hi + pi_ * hr + ei
            else:
                h_scr[d] = jnp.zeros((SCAN_ROWS, 2 * n), F32)

    ntile = n // LANES
    group = SCAN_LW // LANES
    for d in range(2):
        su_ref = suf_ref if d == 0 else sub_ref
        for r in range(SCAN_ROWS):
            for sl in range(SSM_DIM // LANES):
                col = r * SSM_DIM + sl * LANES
                u_scr[d, sl, pl.ds(r, tt, stride=SCAN_ROWS), :] = su_ref[:, col:col + LANES]
        u = jnp.concatenate([u_scr[d, sl] for sl in range(SSM_DIM // LANES)], axis=1).astype(BF16)
        bu_scr[d] = _dot(u, bb_scr[d])
        for c0 in range(0, ntile, group):
            lre = slice(c0 * LANES, (c0 + group) * LANES)
            lim = slice(n + c0 * LANES, n + (c0 + group) * LANES)
            ar, ai = a_scr[d, 0, :, lre], a_scr[d, 1, :, lre]
            halves = [slice(0, 8), slice(8, 16)]
            hr = [h_scr[d, hs, lre] for hs in halves]
            hi = [h_scr[d, hs, lim] for hs in halves]
            steps = range(tt) if d == 0 else range(tt - 1, -1, -1)
            for s in steps:
                for k in range(2):
                    rows = slice(s * SCAN_ROWS + 8 * k, s * SCAN_ROWS + 8 * k + 8)
                    br, bi = bu_scr[d, rows, lre], bu_scr[d, rows, lim]
                    hr[k], hi[k] = ar * hr[k] - ai * hi[k] + br, ar * hi[k] + ai * hr[k] + bi
                    bu_scr[d, rows, lre] = hr[k]
                    bu_scr[d, rows, lim] = hi[k]
            for k, hs in enumerate(halves):
                h_scr[d, hs, lre] = hr[k]
                h_scr[d, hs, lim] = hi[k]
        if emit_y:
            y = _dot(bu_scr[d].astype(BF16), cblk_ref[d])
            for sl in range(SSM_DIM // LANES):
                y_scr[d, sl] = y[:, sl * LANES:(sl + 1) * LANES]
            for r in range(SCAN_ROWS):
                for sl in range(SSM_DIM // LANES):
                    col = r * SSM_DIM + sl * LANES
                    y_refs[d][:, col:col + LANES] = y_scr[d, sl, pl.ds(r, tt, stride=SCAN_ROWS), :]

    @pl.when(i == pl.num_programs(0) - 1)
    def _final():
        fin_ref[...] = h_scr[...]


def _scan(sp, su_tm, emit_y, s0=None, ez=None):
    chain = s0 is not None
    nt = SCAN_LEN // SCAN_TT
    n = SSM_N
    full3 = lambda shape: pl.BlockSpec(shape, lambda i: (0, 0, 0))
    in_specs = [full3((2, 1, n)), full3((2, 1, n)), full3((2, 1, n)),
                full3((2, SSM_DIM, n)), full3((2, SSM_DIM, n)), full3((2, 2 * n, SSM_DIM))]
    args = [sp["lam_re"], sp["lam_im"], sp["log_dt"], sp["b_re"], sp["b_im"], sp["cblk"]]
    if chain:
        in_specs += [full3((2, 2, 2 * n)), full3((2, SCAN_ROWS, 2 * n))]
        args += [s0, ez]
    tblk = (SCAN_TT, SCAN_ROWS * SSM_DIM)
    fwd = lambda i: (i, 0)
    bwd = lambda i: (nt - 1 - i, 0)
    in_specs += [pl.BlockSpec(tblk, fwd), pl.BlockSpec(tblk, bwd)]
    args += [su_tm, su_tm]
    out_shape, out_specs = [], []
    scratch = [
        pltpu.VMEM((2, 2, 8, n), F32),
        pltpu.VMEM((2, SSM_DIM, 2 * n), BF16),
        pltpu.VMEM((2, SCAN_ROWS, 2 * n), F32),
        pltpu.VMEM((2, SCAN_ROWS * SCAN_TT, 2 * n), F32),
        pltpu.VMEM((2, SSM_DIM // LANES, SCAN_ROWS * SCAN_TT, LANES), F32),
    ]
    if emit_y:
        yshape = jax.ShapeDtypeStruct((SCAN_LEN, SCAN_ROWS * SSM_DIM), F32)
        out_shape += [yshape, yshape]
        out_specs += [pl.BlockSpec(tblk, fwd), pl.BlockSpec(tblk, bwd)]
        scratch.append(pltpu.VMEM((2, SSM_DIM // LANES, SCAN_ROWS * SCAN_TT, LANES), F32))
    out_shape.append(jax.ShapeDtypeStruct((2, SCAN_ROWS, 2 * n), F32))
    out_specs.append(full3((2, SCAN_ROWS, 2 * n)))
    return pl.pallas_call(
        functools.partial(_scan_kernel, emit_y=emit_y, chain=chain),
        grid=(nt,),
        in_specs=in_specs,
        out_specs=out_specs,
        out_shape=out_shape,
        scratch_shapes=scratch,
        compiler_params=_cparams(("arbitrary",)),
        name="scan_chain" if chain else ("scan_y" if emit_y else "scan_state"),
    )(*args)


def _shifted(scr_ref, off, rows, pos, seq_len, w):
    mid = scr_ref[off:off + rows, :]
    up = jnp.where(pos != 0, scr_ref[off - 1:off - 1 + rows, :], 0.0)
    dn = jnp.where(pos != seq_len - 1, scr_ref[off + 1:off + 1 + rows, :], 0.0)
    return w[0:1, :] * up + w[1:2, :] * mid + w[2:3, :] * dn


def _mix_kernel(x_ref, mod_ref, attn_ref, g3_ref, *rest, halo, tm, seq_len):
    rest = list(rest)
    if halo:
        gp_ref, gn_ref = rest[:2]
        rest = rest[2:]
    (su_ref, yf_ref, yb_ref, wc_ref, dsk_ref, wglu_ref, woa_ref, woc_ref, wos_ref, gn2_ref,
     x1_ref, h2_ref, z_scr) = rest
    i = pl.program_id(0)
    m = mod_ref[...]
    g1, sh2, sc2 = m[:, 2 * D:3 * D], m[:, 3 * D:4 * D], m[:, 4 * D:5 * D]
    pos = (i * tm + lax.broadcasted_iota(jnp.int32, (tm, 1), 0)) % seq_len

    g3 = g3_ref[...]
    gb = g3[:, 0:SC_DIM]
    z_scr[8:8 + tm, :] = g3[:, SC_DIM:2 * SC_DIM] * g3[:, 2 * SC_DIM:3 * SC_DIM]
    if halo:
        z_scr[0:8, :] = gp_ref[:, SC_DIM:2 * SC_DIM] * gp_ref[:, 2 * SC_DIM:3 * SC_DIM]
        z_scr[8 + tm:16 + tm, :] = gn_ref[:, SC_DIM:2 * SC_DIM] * gn_ref[:, 2 * SC_DIM:3 * SC_DIM]
    else:
        z_scr[0:8, :] = jnp.zeros((8, SC_DIM), F32)
        z_scr[8 + tm:16 + tm, :] = jnp.zeros((8, SC_DIM), F32)
    conv = gb * _shifted(z_scr, 8, tm, pos, seq_len, wc_ref[...])

    y = dsk_ref[...] * su_ref[...] + yf_ref[...] + yb_ref[...]
    zz = 0.5 * y * (1.0 + jnp.tanh(math.sqrt(2.0 / math.pi) * (y + 0.044715 * (y * y * y))))
    ssm = zz * _sigmoid(_dot(zz.astype(BF16), wglu_ref[...]))

    mix = (_dot(attn_ref[...], woa_ref[...]) + _dot(conv.astype(BF16), woc_ref[...])
           + _dot(ssm.astype(BF16), wos_ref[...]))
    x1 = x_ref[...] + g1 * mix
    x1_ref[...] = x1
    h2_ref[...] = _rms_mod(x1, gn2_ref[...], sc2, sh2).astype(BF16)


def _mix(x, mod3, attn, g3, su, yf, yb, lp, lat, tokens_per_mod, seq_len, tm=256):
    t = x.shape[0]
    halo = seq_len > tm
    tiles_per_mod = tokens_per_mod // tm
    mod_map = (lambda i: (1 + i // tiles_per_mod, 0, 0)) if lat else (lambda i: (0, 0, 0))
    row = lambda i: (i, 0)
    const = lambda i: (0, 0)
    in_specs = [pl.BlockSpec((tm, D), row), pl.BlockSpec((None, 1, 6 * D), mod_map),
                pl.BlockSpec((tm, 512), row), pl.BlockSpec((tm, 768), row)]
    args = [x, mod3, attn, g3]
    if halo:
        r8 = tm // 8
        in_specs += [pl.BlockSpec((8, 768), lambda i: (jnp.maximum(i * r8 - 1, 0), 0)),
                     pl.BlockSpec((8, 768), lambda i: (jnp.minimum((i + 1) * r8, t // 8 - 1), 0))]
        args += [g3, g3]
    assert tm == SCAN_LEN
    in_specs += [pl.BlockSpec((SCAN_LEN, SSM_DIM), lambda i: (0, i))] * 3
    args += [su, yf, yb]
    in_specs += [pl.BlockSpec((3, SC_DIM), const), pl.BlockSpec((1, SSM_DIM), const),
                 pl.BlockSpec((SSM_DIM, SSM_DIM), const), pl.BlockSpec((ATTN_DIM, D), const),
                 pl.BlockSpec((SC_DIM, D), const), pl.BlockSpec((SSM_DIM, D), const),
                 pl.BlockSpec((1, D), const)]
    args += [lp["sc_conv"], lp["ssm_d"], lp["w_glu"], lp["wo_a"], lp["wo_c"], lp["wo_s"], lp["norm_ffn"]]
    return pl.pallas_call(
        functools.partial(_mix_kernel, halo=halo, tm=tm, seq_len=seq_len),
        grid=(t // tm,),
        in_specs=in_specs,
        out_specs=[pl.BlockSpec((tm, D), row), pl.BlockSpec((tm, D), row)],
        out_shape=[jax.ShapeDtypeStruct((t, D), F32), jax.ShapeDtypeStruct((t, D), BF16)],
        scratch_shapes=[pltpu.VMEM((tm + 16, SC_DIM), F32)],
        compiler_params=_cparams(("parallel",)),
        name="mix_lat" if lat else "mix_ctx",
    )(*args)


def _ffn_kernel(h_ref, *rest, halo, tm, seq_len, final):
    rest = list(rest)
    if halo:
        hp_ref, hn_ref = rest[:2]
        rest = rest[2:]
    x1_ref, mod_ref, wa_ref, wg_ref, ca_ref, cg_ref, wd_ref = rest[:7]
    rest = rest[7:]
    if final:
        gf_ref = rest[0]
        rest = rest[1:]
    o_ref, acc_scr, ua_scr, ug_scr = rest[:4]
    i = pl.program_id(0)
    j = pl.program_id(1)
    off = 16 if halo else 8
    pos = (i * tm + lax.broadcasted_iota(jnp.int32, (tm, 1), 0)) % seq_len

    if halo:
        hcat_scr = rest[4]

        @pl.when(j == 0)
        def _stage():
            hcat_scr[0:16, :] = hp_ref[...]
            hcat_scr[16:16 + tm, :] = h_ref[...]
            hcat_scr[16 + tm:32 + tm, :] = hn_ref[...]

        hh = hcat_scr[...]
        ua_scr[...] = _dot(hh, wa_ref[...])
        ug_scr[...] = _dot(hh, wg_ref[...])
    else:
        hh = h_ref[...]
        tf = ua_scr.shape[1]
        for scr in (ua_scr, ug_scr):
            scr[0:8, :] = jnp.zeros((8, tf), F32)
            scr[8 + tm:16 + tm, :] = jnp.zeros((8, tf), F32)
        ua_scr[8:8 + tm, :] = _dot(hh, wa_ref[...])
        ug_scr[8:8 + tm, :] = _dot(hh, wg_ref[...])

    a = _shifted(ua_scr, off, tm, pos, seq_len, ca_ref[...])
    g = _shifted(ug_scr, off, tm, pos, seq_len, cg_ref[...])
    act = (a * (g * _sigmoid(g))).astype(BF16)
    part = _dot(act, wd_ref[...])

    @pl.when(j == 0)
    def _init():
        acc_scr[...] = part

    @pl.when(j != 0)
    def _accum():
        acc_scr[...] += part

    @pl.when(j == pl.num_programs(1) - 1)
    def _finish():
        g2 = mod_ref[...][:, 5 * D:6 * D]
        x2 = x1_ref[...] + g2 * acc_scr[...]
        if final:
            x2 = x2 * lax.rsqrt(jnp.mean(x2 * x2, axis=-1, keepdims=True) + RMS_EPS) * gf_ref[...]
        o_ref[...] = x2


def _ffn(h2, x1, mod3, lp, lat, tokens_per_mod, seq_len, final_g, tm=512, tf=1408):
    t = h2.shape[0]
    halo = seq_len > tm
    final = final_g is not None
    nj = D_FF // tf
    tiles_per_mod = tokens_per_mod // tm
    mod_map = (lambda i, j: (1 + i // tiles_per_mod, 0, 0)) if lat else (lambda i, j: (0, 0, 0))
    row = lambda i, j: (i, 0)
    in_specs = [pl.BlockSpec((tm, D), row)]
    args = [h2]
    if halo:
        r16 = tm // 16
        in_specs += [pl.BlockSpec((16, D), lambda i, j: (jnp.maximum(i * r16 - 1, 0), 0)),
                     pl.BlockSpec((16, D), lambda i, j: (jnp.minimum((i + 1) * r16, t // 16 - 1), 0))]
        args += [h2, h2]
    in_specs += [pl.BlockSpec((tm, D), row), pl.BlockSpec((None, 1, 6 * D), mod_map),
                 pl.BlockSpec((D, tf), lambda i, j: (0, j)), pl.BlockSpec((D, tf), lambda i, j: (0, j + nj)),
                 pl.BlockSpec((3, tf), lambda i, j: (0, j)), pl.BlockSpec((3, tf), lambda i, j: (0, j + nj)),
                 pl.BlockSpec((tf, D), lambda i, j: (j, 0))]
    args += [x1, mod3, lp["w_up"], lp["w_up"], lp["ffn_conv"], lp["ffn_conv"], lp["w_down"]]
    if final:
        in_specs.append(pl.BlockSpec((1, D), lambda i, j: (0, 0)))
        args.append(final_g)
    urows = tm + (32 if halo else 16)
    scratch = [pltpu.VMEM((tm, D), F32), pltpu.VMEM((urows, tf), F32), pltpu.VMEM((urows, tf), F32)]
    if halo:
        scratch.append(pltpu.VMEM((tm + 32, D), BF16))
    return pl.pallas_call(
        functools.partial(_ffn_kernel, halo=halo, tm=tm, seq_len=seq_len, final=final),
        grid=(t // tm, nj),
        in_specs=in_specs,
        out_specs=pl.BlockSpec((tm, D), row),
        out_shape=jax.ShapeDtypeStruct((t, D), F32),
        scratch_shapes=scratch,
        compiler_params=_cparams(("parallel", "arbitrary")),
        name="ffn_lat" if lat else "ffn_ctx",
    )(*args)


def _rope_tables(seq_len):
    rows = seq_len // GRID_W
    row = jnp.repeat(jnp.arange(rows, dtype=F32), GRID_W)
    col = jnp.tile(jnp.arange(GRID_W, dtype=F32), rows)
    freqs = ROPE_BASE ** (-jnp.arange(ROPE_FREQS, dtype=F32) / ROPE_FREQS)
    ang = jnp.stack([row[:, None] * freqs, col[:, None] * freqs], axis=1)
    cos, sin = jnp.cos(ang), jnp.sin(ang)
    zero = jnp.zeros_like(sin)
    c64 = jnp.stack([cos, cos], axis=2).reshape(seq_len, HEAD_DIM)
    sneg64 = jnp.stack([-sin, zero], axis=2).reshape(seq_len, HEAD_DIM)
    spos64 = jnp.stack([zero, sin], axis=2).reshape(seq_len, HEAD_DIM)
    return tuple(jnp.tile(tb, (1, 2)) for tb in (c64, sneg64, spos64))


def _layer_params(l, w_in, w_out, norm_mix, norm_ffn, attn_sink, sc_conv, ssm_lam_re, ssm_lam_im,
                  ssm_log_dt, ssm_b_re, ssm_b_im, ssm_c_re, ssm_c_im, ssm_d, ssm_w_glu,
                  ffn_w_up, ffn_conv, ffn_w_down):
    qscale = jnp.concatenate([jnp.full((ATTN_DIM,), 1.0 / math.sqrt(HEAD_DIM), F32),
                              jnp.ones((IN_DIM - ATTN_DIM,), F32)])
    wo = w_out[l].astype(BF16)
    eye = jnp.eye(SSM_GROUPS, dtype=F32)
    bdiag = lambda b: jnp.einsum("dgnc,gh->dgchn", b, eye).reshape(2, SSM_DIM, SSM_N)
    cdiag = lambda c: jnp.einsum("dgcn,gh->dgnhc", c, eye).reshape(2, SSM_N, SSM_DIM)
    scan = {
        "lam_re": ssm_lam_re[l].reshape(2, 1, SSM_N),
        "lam_im": ssm_lam_im[l].reshape(2, 1, SSM_N),
        "log_dt": jnp.repeat(ssm_log_dt[l], SSM_STATE, axis=-1).reshape(2, 1, SSM_N),
        "b_re": bdiag(ssm_b_re[l]),
        "b_im": bdiag(ssm_b_im[l]),
        "cblk": jnp.concatenate([cdiag(ssm_c_re[l]), -cdiag(ssm_c_im[l])], axis=1).astype(BF16),
    }
    return {
        "w_in": (w_in[l] * qscale).astype(BF16),
        "norm_mix": norm_mix[l].reshape(1, D),
        "norm_ffn": norm_ffn[l].reshape(1, D),
        "sink": attn_sink[l],
        "sc_conv": sc_conv[l].T,
        "ssm_d": ssm_d[l].reshape(1, SSM_DIM),
        "w_glu": ssm_w_glu[l].astype(BF16),
        "wo_a": wo[0:ATTN_DIM], "wo_c": wo[ATTN_DIM:ATTN_DIM + SC_DIM], "wo_s": wo[ATTN_DIM + SC_DIM:],
        "w_up": ffn_w_up[l].astype(BF16),
        "ffn_conv": ffn_conv[l].T,
        "w_down": ffn_w_down[l].astype(BF16),
        "scan": scan,
    }


def _dup_heads(c):
    b, s = c.shape[0], c.shape[1]
    c = c.astype(BF16)
    z = jnp.zeros_like(c)
    tiles = jnp.stack([jnp.concatenate([c, z], axis=-1), jnp.concatenate([z, c], axis=-1)], axis=3)
    return tiles.reshape(b, s, KV_HEADS * 2 * LANES)


def kernel(x_prompt, x_sample, cache_k, cache_v, state_ssm_re, state_ssm_im, c, c_ctx, norm_mix, norm_ffn, norm_final, w_ada, b_ada, w_in, w_out, attn_sink, sc_conv, ssm_lam_re, ssm_lam_im, ssm_log_dt, ssm_b_re, ssm_b_im, ssm_c_re, ssm_c_im, ssm_d, ssm_w_glu, ffn_w_up, ffn_conv, ffn_w_down):
    batch, seq = x_prompt.shape[0], x_prompt.shape[1]
    dec_batch, dec_seq = x_sample.shape[0], x_sample.shape[1]
    assert batch == SCAN_ROWS and seq == SCAN_LEN
    assert dec_batch * (dec_seq // SCAN_LEN) == SCAN_ROWS and dec_batch == 2

    cs = jnp.concatenate([c_ctx[None, :], c, jnp.zeros((8 - 1 - dec_batch, D), F32)], axis=0)
    mods = _adaln(cs, w_ada, b_ada)
    rope_tabs = _rope_tables(dec_seq)
    gfin = norm_final.reshape(1, D)

    xp = x_prompt.reshape(batch * seq, D)
    xs = x_sample.reshape(dec_batch * dec_seq, D)
    ks_out, vs_out, sre_out, sim_out = [], [], [], []
    for l in range(DEPTH):
        lp = _layer_params(l, w_in, w_out, norm_mix, norm_ffn, attn_sink, sc_conv, ssm_lam_re, ssm_lam_im,
                           ssm_log_dt, ssm_b_re, ssm_b_im, ssm_c_re, ssm_c_im, ssm_d, ssm_w_glu,
                           ffn_w_up, ffn_conv, ffn_w_down)
        mod3 = mods[l].reshape(8, 1, 6 * D)
        last = gfin if l == DEPTH - 1 else None

        q, kd, vd, kv, g3, su = _inproj(xp, mod3, lp["norm_mix"], lp["w_in"], None, batch * seq, seq)
        attn = _attn_ctx(lp["sink"], q, kd, vd, seq)
        yf, yb, fin = _scan(lp["scan"], su, True)
        x1, h2 = _mix(xp, mod3, attn, g3, su, yf, yb, lp, False, batch * seq, seq)
        xp = _ffn(h2, x1, mod3, lp, False, batch * seq, seq, last)
        ks_out.append(kv[:, 0:KV_DIM].reshape(batch, seq, KV_HEADS, HEAD_DIM))
        vs_out.append(kv[:, KV_DIM:].reshape(batch, seq, KV_HEADS, HEAD_DIM))
        fin = fin.reshape(2, batch, 2, SSM_GROUPS, SSM_STATE)
        sre_out.append(jnp.transpose(fin[:, :, 0], (1, 0, 2, 3)))
        sim_out.append(jnp.transpose(fin[:, :, 1], (1, 0, 2, 3)))

        q, kd, vd, g3, su = _inproj(xs, mod3, lp["norm_mix"], lp["w_in"], rope_tabs, dec_seq, dec_seq)
        attn = _attn_lat(lp["sink"], q, kd, vd, _dup_heads(cache_k[:, l]), _dup_heads(cache_v[:, l]),
                         dec_batch, dec_seq)
        (ez,) = _scan(lp["scan"], su, False)
        s0 = jnp.concatenate([state_ssm_re[:, l].reshape(dec_batch, 2, SSM_N),
                              state_ssm_im[:, l].reshape(dec_batch, 2, SSM_N)], axis=-1)
        yf, yb, _ = _scan(lp["scan"], su, True, s0=jnp.transpose(s0, (1, 0, 2)), ez=ez)
        x1, h2 = _mix(xs, mod3, attn, g3, su, yf, yb, lp, True, dec_seq, dec_seq)
        xs = _ffn(h2, x1, mod3, lp, True, dec_seq, dec_seq, last)

    return (xp.reshape(batch, seq, D), xs.reshape(dec_batch, dec_seq, D),
            jnp.stack(ks_out, axis=1), jnp.stack(vs_out, axis=1),
            jnp.stack(sre_out, axis=1), jnp.stack(sim_out, axis=1))
```

```python
import functools
import math

import jax
import jax.numpy as jnp
from jax import lax
from jax.experimental import pallas as pl
from jax.experimental.pallas import tpu as pltpu

F32 = jnp.float32
BF16 = jnp.bfloat16

D = 1024
DEPTH = 2
GRID_W = 64
ATTN_DIM = 512
SC_DIM = 256
SSM_DIM = 256
HEAD_DIM = 64
N_HEADS = 8
KV_HEADS = 2
KV_DIM = 128
WINDOW = 128
Q_BLOCK = 128
ROPE_BASE = 10000.0
ROPE_FREQS = 16
SSM_CH = 16
SSM_GROUPS = 16
SSM_STATE = 64
SSM_N = SSM_GROUPS * SSM_STATE
IN_DIM = ATTN_DIM + 2 * KV_DIM + 3 * SC_DIM + SSM_DIM
D_FF = 2816
ATTN_SCALE = 1.0 / math.sqrt(HEAD_DIM)
RMS_EPS = 1e-6
NEG_BIG = -1e30

LANES = 128
SCAN_ROWS = 16
SCAN_LEN = 256
SCAN_TT = 32
SCAN_LW = 256
FFN_CHUNK = 256
assert D_FF % FFN_CHUNK == 0
VMEM_LIMIT = 56 * 1024 * 1024


def _cparams(sem):
    return pltpu.CompilerParams(dimension_semantics=sem, vmem_limit_bytes=VMEM_LIMIT)


def _sigmoid(x):
    return 1.0 / (1.0 + jnp.exp(-x))


def _rms_mod(x, g, scale, shift):
    y = x * lax.rsqrt(jnp.mean(x * x, axis=-1, keepdims=True) + RMS_EPS) * g
    return y * (1.0 + scale) + shift


def _split_bf16(v):
    hi = v.astype(BF16)
    lo = (v - hi.astype(F32)).astype(BF16)
    return hi, lo


def _dot(a, b):
    return jnp.dot(a, b, preferred_element_type=F32)


def _dot_t(a, b):
    return lax.dot_general(a, b, (((1,), (1,)), ((), ())), preferred_element_type=F32)


def _adaln_kernel(c_ref, w_ref, b_ref, o_ref):
    c = c_ref[...]
    s = c * _sigmoid(c)
    s_hi, s_lo = _split_bf16(s)
    w_hi, w_lo = _split_bf16(w_ref[...])
    o_ref[...] = _dot(s_hi, w_hi) + _dot(s_lo, w_hi) + _dot(s_hi, w_lo) + b_ref[...]


def _adaln(cs, w_ada, b_ada):
    tn = 1024
    return pl.pallas_call(
        _adaln_kernel,
        grid=(DEPTH, 6 * D // tn),
        in_specs=[
            pl.BlockSpec((8, D), lambda l, j: (0, 0)),
            pl.BlockSpec((None, D, tn), lambda l, j: (l, 0, j)),
            pl.BlockSpec((None, 1, tn), lambda l, j: (l, 0, j)),
        ],
        out_specs=pl.BlockSpec((None, 8, tn), lambda l, j: (l, 0, j)),
        out_shape=jax.ShapeDtypeStruct((DEPTH, 8, 6 * D), F32),
        compiler_params=_cparams(("parallel", "parallel")),
        name="adaln",
    )(cs, w_ada, b_ada.reshape(DEPTH, 1, 6 * D))


def _inproj_kernel(x_ref, mod_ref, g_ref, w_ref, *rest, rope):
    if rope:
        cos_ref, sneg_ref, spos_ref, q_ref, kd_ref, vd_ref, g3_ref, su_ref = rest
    else:
        q_ref, kd_ref, vd_ref, kv_ref, g3_ref, su_ref = rest
    m = mod_ref[...]
    h = _rms_mod(x_ref[...], g_ref[...], m[:, D:2 * D], m[:, 0:D]).astype(BF16)
    acc = _dot(h, w_ref[...])
    k = acc[:, 512:640]
    v = acc[:, 640:768]
    if rope:
        c, sn, sp = cos_ref[...], sneg_ref[...], spos_ref[...]

        def rot(t):
            return t * c + pltpu.roll(t, LANES - ROPE_FREQS, 1) * sn + pltpu.roll(t, ROPE_FREQS, 1) * sp

        for p in range(4):
            q_ref[:, LANES * p:LANES * (p + 1)] = (rot(acc[:, LANES * p:LANES * (p + 1)]) * ATTN_SCALE).astype(BF16)
        k = rot(k)
    else:
        q_ref[...] = (acc[:, 0:512] * ATTN_SCALE).astype(BF16)
        kv_ref[...] = acc[:, 512:768]
    lo = lax.broadcasted_iota(jnp.int32, k.shape, 1) < HEAD_DIM
    kr = pltpu.roll(k, HEAD_DIM, 1)
    vr = pltpu.roll(v, HEAD_DIM, 1)
    for ref, t, tr in ((kd_ref, k, kr), (vd_ref, v, vr)):
        ref[:, 0:LANES] = jnp.where(lo, t, 0.0).astype(BF16)
        ref[:, LANES:2 * LANES] = jnp.where(lo, 0.0, tr).astype(BF16)
        ref[:, 2 * LANES:3 * LANES] = jnp.where(lo, tr, 0.0).astype(BF16)
        ref[:, 3 * LANES:4 * LANES] = jnp.where(lo, 0.0, t).astype(BF16)
    g3_ref[...] = acc[:, 768:1536]
    for r in range(acc.shape[0] // SCAN_LEN):
        su_ref[:, r * SSM_DIM:(r + 1) * SSM_DIM] = acc[r * SCAN_LEN:(r + 1) * SCAN_LEN, 1536:1792]


def _inproj(x, mod3, g, w, rope_tabs, tokens_per_mod, seq_len, tm=512):
    t = x.shape[0]
    rope = rope_tabs is not None
    tiles_per_mod = tokens_per_mod // tm
    tiles_per_seq = seq_len // tm
    if rope:
        mod_map = lambda i: (1 + i // tiles_per_mod, 0, 0)
    else:
        mod_map = lambda i: (0, 0, 0)
    row = lambda i: (i, 0)
    in_specs = [
        pl.BlockSpec((tm, D), row),
        pl.BlockSpec((None, 1, 6 * D), mod_map),
        pl.BlockSpec((1, D), lambda i: (0, 0)),
        pl.BlockSpec((D, IN_DIM), lambda i: (0, 0)),
    ]
    args = [x, mod3, g, w]
    if rope:
        in_specs += [pl.BlockSpec((tm, LANES), lambda i: (i % tiles_per_seq, 0))] * 3
        args += list(rope_tabs)
    out_shape = [jax.ShapeDtypeStruct((t, 512), BF16)] * 3
    out_specs = [pl.BlockSpec((tm, 512), row)] * 3
    if not rope:
        out_shape.append(jax.ShapeDtypeStruct((t, 256), F32))
        out_specs.append(pl.BlockSpec((tm, 256), row))
    out_shape += [jax.ShapeDtypeStruct((t, 768), F32),
                  jax.ShapeDtypeStruct((SCAN_LEN, t // SCAN_LEN * SSM_DIM), F32)]
    out_specs += [pl.BlockSpec((tm, 768), row),
                  pl.BlockSpec((SCAN_LEN, tm // SCAN_LEN * SSM_DIM), lambda i: (0, i))]
    return pl.pallas_call(
        functools.partial(_inproj_kernel, rope=rope),
        grid=(t // tm,),
        in_specs=in_specs,
        out_specs=out_specs,
        out_shape=out_shape,
        compiler_params=_cparams(("parallel",)),
        name="inproj_lat" if rope else "inproj_ctx",
    )(*args)


def _attention(sink_ref, q_ref, o_ref, keys, vals, bias, nq):
    top = lax.broadcasted_iota(jnp.int32, (2 * nq, 1), 0) < nq
    scores = []
    for kvh in range(KV_HEADS):
        q2 = jnp.concatenate([q_ref[:, LANES * (2 * kvh + pp):LANES * (2 * kvh + pp + 1)] for pp in range(2)],
                             axis=0)
        scores.append([_dot_t(q2, keys[kvh][half]) for half in range(2)])
    for kvh in range(KV_HEADS):
        acc = None
        for half in range(2):
            sk = jnp.where(top, sink_ref[4 * kvh + half], sink_ref[4 * kvh + 2 + half])
            s = scores[kvh][half]
            if bias is not None:
                nb_ = bias.shape[1]
                s = jnp.concatenate([s[:, 0:nb_] + bias, s[:, nb_:]], axis=1)
            m = jnp.maximum(jnp.max(s, axis=-1, keepdims=True), sk)
            e = jnp.exp(s - m)
            den = jnp.sum(e, axis=-1, keepdims=True) + jnp.exp(sk - m)
            o = _dot(e.astype(BF16), vals[kvh][half]) / den
            acc = o if acc is None else acc + o
        o_ref[:, LANES * 2 * kvh:LANES * (2 * kvh + 1)] = acc[0:nq].astype(BF16)
        o_ref[:, LANES * (2 * kvh + 1):LANES * (2 * kvh + 2)] = acc[nq:2 * nq].astype(BF16)


def _kv_tiles(kvh):
    return [slice(LANES * (2 * kvh + h), LANES * (2 * kvh + h + 1)) for h in range(2)]


def _attn_ctx_kernel(sink_ref, q_ref, kd_ref, vd_ref, o_ref):
    keys = [[kd_ref[:, t] for t in _kv_tiles(kvh)] for kvh in range(KV_HEADS)]
    vals = [[vd_ref[:, t] for t in _kv_tiles(kvh)] for kvh in range(KV_HEADS)]
    _attention(sink_ref, q_ref, o_ref, keys, vals, None, q_ref.shape[0])


def _attn_ctx(sink, q, kd, vd, seq_len):
    t = q.shape[0]
    row = lambda b: (b, 0)
    return pl.pallas_call(
        _attn_ctx_kernel,
        grid=(t // seq_len,),
        in_specs=[
            pl.BlockSpec(memory_space=pltpu.SMEM),
            pl.BlockSpec((seq_len, 512), row),
            pl.BlockSpec((seq_len, 512), row),
            pl.BlockSpec((seq_len, 512), row),
        ],
        out_specs=pl.BlockSpec((seq_len, 512), row),
        out_shape=jax.ShapeDtypeStruct((t, 512), BF16),
        compiler_params=_cparams(("parallel",)),
        name="attn_ctx",
    )(sink, q, kd, vd)


def _attn_lat_kernel(sink_ref, q_ref, kp_ref, kc_ref, kn_ref, vp_ref, vc_ref, vn_ref, ck_ref, cv_ref, o_ref):
    i = pl.program_id(1)
    nb = pl.num_programs(1)
    r = lax.broadcasted_iota(jnp.int32, (Q_BLOCK, Q_BLOCK), 0)
    j = lax.broadcasted_iota(jnp.int32, (Q_BLOCK, Q_BLOCK), 1)
    bias = jnp.concatenate([
        jnp.where(jnp.logical_and(j >= r, i > 0), 0.0, NEG_BIG),
        jnp.where(jnp.logical_and(j <= r, i < nb - 1), 0.0, NEG_BIG)], axis=1)
    bias = jnp.concatenate([bias, bias], axis=0)
    keys = [[jnp.concatenate([ref[:, t] for ref in (kp_ref, kn_ref, kc_ref, ck_ref)], axis=0)
             for t in _kv_tiles(kvh)] for kvh in range(KV_HEADS)]
    vals = [[jnp.concatenate([ref[:, t] for ref in (vp_ref, vn_ref, vc_ref, cv_ref)], axis=0)
             for t in _kv_tiles(kvh)] for kvh in range(KV_HEADS)]
    _attention(sink_ref, q_ref, o_ref, keys, vals, bias, Q_BLOCK)


def _attn_lat(sink, q, kd, vd, ckd, cvd, batch, seq_len):
    nb = seq_len // Q_BLOCK
    past = ckd.shape[1]
    cur = lambda b, i: (b * nb + i, 0)
    prev = lambda b, i: (b * nb + jnp.maximum(i - 1, 0), 0)
    nxt = lambda b, i: (b * nb + jnp.minimum(i + 1, nb - 1), 0)
    kvspec = lambda f: pl.BlockSpec((Q_BLOCK, 512), f)
    cspec = pl.BlockSpec((None, past, 512), lambda b, i: (b, 0, 0))
    return pl.pallas_call(
        _attn_lat_kernel,
        grid=(batch, nb),
        in_specs=[
            pl.BlockSpec(memory_space=pltpu.SMEM),
            pl.BlockSpec((Q_BLOCK, 512), cur),
            kvspec(prev), kvspec(cur), kvspec(nxt),
            kvspec(prev), kvspec(cur), kvspec(nxt),
            cspec, cspec,
        ],
        out_specs=pl.BlockSpec((Q_BLOCK, 512), cur),
        out_shape=jax.ShapeDtypeStruct((batch * seq_len, 512), BF16),
        compiler_params=_cparams(("parallel", "parallel")),
        name="attn_lat",
    )(sink, q, kd, kd, kd, vd, vd, vd, ckd, cvd)


def _scan_kernel(lre_ref, lim_ref, ldt_ref, bre_ref, bim_ref, cblk_ref, *rest, emit_y, chain):
    rest = list(rest)
    if chain:
        s0_ref, ez_ref = rest[:2]
        rest = rest[2:]
    suf_ref, sub_ref = rest[:2]
    rest = rest[2:]
    if emit_y:
        y_refs = rest[:2]
        rest = rest[2:]
    fin_ref, a_scr, bb_scr, h_scr, bu_scr, u_scr = rest[:6]
    if emit_y:
        y_scr = rest[6]
    i = pl.program_id(0)
    tt = SCAN_TT
    n = SSM_N

    @pl.when(i == 0)
    def _prologue():
        for d in range(2):
            lr, li = lre_ref[d], lim_ref[d]
            dt = jnp.exp(ldt_ref[d])
            mag = jnp.exp(lr * dt)
            ar, ai = mag * jnp.cos(li * dt), mag * jnp.sin(li * dt)
            den = lr * lr + li * li
            fr = ((ar - 1.0) * lr + ai * li) / den
            fi = (ai * lr - (ar - 1.0) * li) / den
            bre, bim = bre_ref[d], bim_ref[d]
            bb_scr[d, :, 0:n] = (fr * bre - fi * bim).astype(BF16)
            bb_scr[d, :, n:2 * n] = (fr * bim + fi * bre).astype(BF16)
            a_scr[d, 0] = jnp.broadcast_to(ar, (8, n))
            a_scr[d, 1] = jnp.broadcast_to(ai, (8, n))
            if chain:
                pr, pi_ = ar, ai
                for _ in range(8):
                    pr, pi_ = pr * pr - pi_ * pi_, 2.0 * pr * pi_
                chunks = SCAN_ROWS // 2
                for b in range(2):
                    hr = s0_ref[d, b:b + 1, 0:n]
                    hi = s0_ref[d, b:b + 1, n:2 * n]
                    order = range(chunks) if d == 0 else range(chunks - 1, -1, -1)
                    for c in order:
                        rw = b * chunks + c
                        h_scr[d, rw:rw + 1, 0:n] = hr
                        h_scr[d, rw:rw + 1, n:2 * n] = hi
                        er = ez_ref[d, rw:rw + 1, 0:n]
                        ei = ez_ref[d, rw:rw + 1, n:2 * n]
                        hr, hi = pr * hr - pi_ * hi + er, pr * hi + pi_ * hr + ei
            else:
                h_scr[d] = jnp.zeros((SCAN_ROWS, 2 * n), F32)

    ntile = n // LANES
    group = SCAN_LW // LANES

    def project_in(d):
        su_ref = suf_ref if d == 0 else sub_ref
        for r in range(SCAN_ROWS):
            for sl in range(SSM_DIM // LANES):
                col = r * SSM_DIM + sl * LANES
                u_scr[d, sl, pl.ds(r, tt, stride=SCAN_ROWS), :] = su_ref[:, col:col + LANES]
        u = jnp.concatenate([u_scr[d, sl] for sl in range(SSM_DIM // LANES)], axis=1).astype(BF16)
        bu_scr[d] = _dot(u, bb_scr[d])

    def recur(d):
        for c0 in range(0, ntile, group):
            lre = slice(c0 * LANES, (c0 + group) * LANES)
            lim = slice(n + c0 * LANES, n + (c0 + group) * LANES)
            ar, ai = a_scr[d, 0, :, lre], a_scr[d, 1, :, lre]
            halves = [slice(0, 8), slice(8, 16)]
            hr = [h_scr[d, hs, lre] for hs in halves]
            hi = [h_scr[d, hs, lim] for hs in halves]
            steps = range(tt) if d == 0 else range(tt - 1, -1, -1)
            for s in steps:
                for k in range(2):
                    rows = slice(s * SCAN_ROWS + 8 * k, s * SCAN_ROWS + 8 * k + 8)
                    br, bi = bu_scr[d, rows, lre], bu_scr[d, rows, lim]
                    hr[k], hi[k] = ar * hr[k] - ai * hi[k] + br, ar * hi[k] + ai * hr[k] + bi
                    bu_scr[d, rows, lre] = hr[k]
                    bu_scr[d, rows, lim] = hi[k]
            for k, hs in enumerate(halves):
                h_scr[d, hs, lre] = hr[k]
                h_scr[d, hs, lim] = hi[k]

    def project_out(d):
        y = _dot(bu_scr[d].astype(BF16), cblk_ref[d])
        for sl in range(SSM_DIM // LANES):
            y_scr[d, sl] = y[:, sl * LANES:(sl + 1) * LANES]
        for r in range(SCAN_ROWS):
            for sl in range(SSM_DIM // LANES):
                col = r * SSM_DIM + sl * LANES
                y_refs[d][:, col:col + LANES] = y_scr[d, sl, pl.ds(r, tt, stride=SCAN_ROWS), :]

    project_in(0)
    project_in(1)
    recur(0)
    if emit_y:
        project_out(0)
    recur(1)
    if emit_y:
        project_out(1)

    @pl.when(i == pl.num_programs(0) - 1)
    def _final():
        fin_ref[...] = h_scr[...]


def _scan(sp, su_tm, emit_y, s0=None, ez=None):
    chain = s0 is not None
    nt = SCAN_LEN // SCAN_TT
    n = SSM_N
    full3 = lambda shape: pl.BlockSpec(shape, lambda i: (0, 0, 0))
    in_specs = [full3((2, 1, n)), full3((2, 1, n)), full3((2, 1, n)),
                full3((2, SSM_DIM, n)), full3((2, SSM_DIM, n)), full3((2, 2 * n, SSM_DIM))]
    args = [sp["lam_re"], sp["lam_im"], sp["log_dt"], sp["b_re"], sp["b_im"], sp["cblk"]]
    if chain:
        in_specs += [full3((2, 2, 2 * n)), full3((2, SCAN_ROWS, 2 * n))]
        args += [s0, ez]
    tblk = (SCAN_TT, SCAN_ROWS * SSM_DIM)
    fwd = lambda i: (i, 0)
    bwd = lambda i: (nt - 1 - i, 0)
    in_specs += [pl.BlockSpec(tblk, fwd), pl.BlockSpec(tblk, bwd)]
    args += [su_tm, su_tm]
    out_shape, out_specs = [], []
    scratch = [
        pltpu.VMEM((2, 2, 8, n), F32),
        pltpu.VMEM((2, SSM_DIM, 2 * n), BF16),
        pltpu.VMEM((2, SCAN_ROWS, 2 * n), F32),
        pltpu.VMEM((2, SCAN_ROWS * SCAN_TT, 2 * n), F32),
        pltpu.VMEM((2, SSM_DIM // LANES, SCAN_ROWS * SCAN_TT, LANES), F32),
    ]
    if emit_y:
        yshape = jax.ShapeDtypeStruct((SCAN_LEN, SCAN_ROWS * SSM_DIM), F32)
        out_shape += [yshape, yshape]
        out_specs += [pl.BlockSpec(tblk, fwd), pl.BlockSpec(tblk, bwd)]
        scratch.append(pltpu.VMEM((2, SSM_DIM // LANES, SCAN_ROWS * SCAN_TT, LANES), F32))
    out_shape.append(jax.ShapeDtypeStruct((2, SCAN_ROWS, 2 * n), F32))
    out_specs.append(full3((2, SCAN_ROWS, 2 * n)))
    return pl.pallas_call(
        functools.partial(_scan_kernel, emit_y=emit_y, chain=chain),
        grid=(nt,),
        in_specs=in_specs,
        out_specs=out_specs,
        out_shape=out_shape,
        scratch_shapes=scratch,
        compiler_params=_cparams(("arbitrary",)),
        name="scan_chain" if chain else ("scan_y" if emit_y else "scan_state"),
    )(*args)


def _shifted(scr_ref, off, rows, pos, seq_len, w):
    mid = scr_ref[off:off + rows, :]
    up = jnp.where(pos != 0, scr_ref[off - 1:off - 1 + rows, :], 0.0)
    dn = jnp.where(pos != seq_len - 1, scr_ref[off + 1:off + 1 + rows, :], 0.0)
    return w[0:1, :] * up + w[1:2, :] * mid + w[2:3, :] * dn


def _mix_kernel(x_ref, mod_ref, attn_ref, g3_ref, *rest, halo, tm, seq_len):
    rest = list(rest)
    if halo:
        gp_ref, gn_ref = rest[:2]
        rest = rest[2:]
    (su_ref, yf_ref, yb_ref, wc_ref, dsk_ref, wglu_ref, woa_ref, woc_ref, wos_ref, gn2_ref,
     x1_ref, h2_ref, z_scr) = rest
    i = pl.program_id(0)
    m = mod_ref[...]
    g1, sh2, sc2 = m[:, 2 * D:3 * D], m[:, 3 * D:4 * D], m[:, 4 * D:5 * D]
    pos = (i * tm + lax.broadcasted_iota(jnp.int32, (tm, 1), 0)) % seq_len

    g3 = g3_ref[...]
    gb = g3[:, 0:SC_DIM]
    z_scr[8:8 + tm, :] = g3[:, SC_DIM:2 * SC_DIM] * g3[:, 2 * SC_DIM:3 * SC_DIM]
    if halo:
        z_scr[0:8, :] = gp_ref[:, SC_DIM:2 * SC_DIM] * gp_ref[:, 2 * SC_DIM:3 * SC_DIM]
        z_scr[8 + tm:16 + tm, :] = gn_ref[:, SC_DIM:2 * SC_DIM] * gn_ref[:, 2 * SC_DIM:3 * SC_DIM]
    else:
        z_scr[0:8, :] = jnp.zeros((8, SC_DIM), F32)
        z_scr[8 + tm:16 + tm, :] = jnp.zeros((8, SC_DIM), F32)
    conv = gb * _shifted(z_scr, 8, tm, pos, seq_len, wc_ref[...])

    y = dsk_ref[...] * su_ref[...] + yf_ref[...] + yb_ref[...]
    zz = 0.5 * y * (1.0 + jnp.tanh(math.sqrt(2.0 / math.pi) * (y + 0.044715 * (y * y * y))))
    ssm = zz * _sigmoid(_dot(zz.astype(BF16), wglu_ref[...]))

    mix = (_dot(attn_ref[...], woa_ref[...]) + _dot(conv.astype(BF16), woc_ref[...])
           + _dot(ssm.astype(BF16), wos_ref[...]))
    x1 = x_ref[...] + g1 * mix
    x1_ref[...] = x1
    h2_ref[...] = _rms_mod(x1, gn2_ref[...], sc2, sh2).astype(BF16)


def _mix(x, mod3, attn, g3, su, yf, yb, lp, lat, tokens_per_mod, seq_len, tm=256):
    t = x.shape[0]
    halo = seq_len > tm
    tiles_per_mod = tokens_per_mod // tm
    mod_map = (lambda i: (1 + i // tiles_per_mod, 0, 0)) if lat else (lambda i: (0, 0, 0))
    row = lambda i: (i, 0)
    const = lambda i: (0, 0)
    in_specs = [pl.BlockSpec((tm, D), row), pl.BlockSpec((None, 1, 6 * D), mod_map),
                pl.BlockSpec((tm, 512), row), pl.BlockSpec((tm, 768), row)]
    args = [x, mod3, attn, g3]
    if halo:
        r8 = tm // 8
        in_specs += [pl.BlockSpec((8, 768), lambda i: (jnp.maximum(i * r8 - 1, 0), 0)),
                     pl.BlockSpec((8, 768), lambda i: (jnp.minimum((i + 1) * r8, t // 8 - 1), 0))]
        args += [g3, g3]
    assert tm == SCAN_LEN
    in_specs += [pl.BlockSpec((SCAN_LEN, SSM_DIM), lambda i: (0, i))] * 3
    args += [su, yf, yb]
    in_specs += [pl.BlockSpec((3, SC_DIM), const), pl.BlockSpec((1, SSM_DIM), const),
                 pl.BlockSpec((SSM_DIM, SSM_DIM), const), pl.BlockSpec((ATTN_DIM, D), const),
                 pl.BlockSpec((SC_DIM, D), lambda i: (ATTN_DIM // SC_DIM, 0)),
                 pl.BlockSpec((SSM_DIM, D), lambda i: ((ATTN_DIM + SC_DIM) // SSM_DIM, 0)),
                 pl.BlockSpec((1, D), const)]
    args += [lp["sc_conv"], lp["ssm_d"], lp["w_glu"], lp["w_out"], lp["w_out"], lp["w_out"], lp["norm_ffn"]]
    return pl.pallas_call(
        functools.partial(_mix_kernel, halo=halo, tm=tm, seq_len=seq_len),
        grid=(t // tm,),
        in_specs=in_specs,
        out_specs=[pl.BlockSpec((tm, D), row), pl.BlockSpec((tm, D), row)],
        out_shape=[jax.ShapeDtypeStruct((t, D), F32), jax.ShapeDtypeStruct((t, D), BF16)],
        scratch_shapes=[pltpu.VMEM((tm + 16, SC_DIM), F32)],
        compiler_params=_cparams(("parallel",)),
        name="mix_lat" if lat else "mix_ctx",
    )(*args)


def _ffn_kernel(h_ref, *rest, halo, tm, seq_len, final):
    rest = list(rest)
    if halo:
        hp_ref, hn_ref = rest[:2]
        rest = rest[2:]
    x1_ref, mod_ref, wup_ref, cw_ref, wd_ref = rest[:5]
    rest = rest[5:]
    if final:
        gf_ref = rest[0]
        rest = rest[1:]
    o_ref, acc_scr, ua_scr, ug_scr = rest[:4]
    i = pl.program_id(0)

    if halo:
        hcat_scr = rest[4]
        hcat_scr[0:16, :] = hp_ref[...]
        hcat_scr[16:16 + tm, :] = h_ref[...]
        hcat_scr[16 + tm:32 + tm, :] = hn_ref[...]
        hh = hcat_scr[...]
        pad, seg_len, nseg = 16, tm, 1
        tiles_per_seq = seq_len // tm
        keep_prev = (i % tiles_per_seq != 0).astype(F32)
        keep_next = (i % tiles_per_seq != tiles_per_seq - 1).astype(F32)
    else:
        hh = h_ref[...]
        pad, seg_len, nseg = 8, seq_len, tm // seq_len
    seg_rows = seg_len + 2 * pad

    nchunk = D_FF // FFN_CHUNK

    def up(c, b):
        for u_scr, half in ((ua_scr, 0), (ug_scr, 1)):
            u = _dot(hh, wup_ref[half * nchunk + c])
            if halo:
                u_scr[b] = u
                u_scr[b, 8:16, :] = u_scr[b, 8:16, :] * keep_prev
                u_scr[b, 16 + tm:24 + tm, :] = u_scr[b, 16 + tm:24 + tm, :] * keep_next
            else:
                for sg in range(nseg):
                    r0 = sg * seg_rows
                    u_scr[b, r0:r0 + pad, :] = jnp.zeros((pad, FFN_CHUNK), F32)
                    u_scr[b, r0 + pad:r0 + pad + seg_len, :] = u[sg * seg_len:(sg + 1) * seg_len]
                    u_scr[b, r0 + pad + seg_len:r0 + seg_rows, :] = jnp.zeros((pad, FFN_CHUNK), F32)

    def conv(c, b, u_scr, half):
        w = cw_ref[half * nchunk + c]
        segs = []
        for sg in range(nseg):
            r0 = sg * seg_rows + pad
            segs.append(w[0:1, :] * u_scr[b, r0 - 1:r0 - 1 + seg_len, :]
                        + w[1:2, :] * u_scr[b, r0:r0 + seg_len, :]
                        + w[2:3, :] * u_scr[b, r0 + 1:r0 + 1 + seg_len, :])
        return segs[0] if nseg == 1 else jnp.concatenate(segs, axis=0)

    def down(c, b):
        a = conv(c, b, ua_scr, 0)
        g = conv(c, b, ug_scr, 1)
        act = (a * (g * _sigmoid(g))).astype(BF16)
        acc_scr[...] += _dot(act, wd_ref[c])

    assert nchunk % 2 == 1
    acc_scr[...] = jnp.zeros((tm, D), F32)
    up(0, 0)

    def round_(k, carry):
        c = 2 * k
        up(c + 1, 1)
        down(c, 0)
        up(c + 2, 0)
        down(c + 1, 1)
        return carry

    lax.fori_loop(0, nchunk // 2, round_, 0)
    down(nchunk - 1, 0)

    g2 = mod_ref[...][:, 5 * D:6 * D]
    x2 = x1_ref[...] + g2 * acc_scr[...]
    if final:
        x2 = x2 * lax.rsqrt(jnp.mean(x2 * x2, axis=-1, keepdims=True) + RMS_EPS) * gf_ref[...]
    o_ref[...] = x2


def _ffn(h2, x1, mod3, lp, lat, tokens_per_mod, seq_len, final_g, tm=512):
    t = h2.shape[0]
    halo = seq_len > tm
    final = final_g is not None
    tiles_per_mod = tokens_per_mod // tm
    mod_map = (lambda i: (1 + i // tiles_per_mod, 0, 0)) if lat else (lambda i: (0, 0, 0))
    row = lambda i: (i, 0)
    const = lambda i: (0, 0)
    const3 = lambda i: (0, 0, 0)
    nchunk = D_FF // FFN_CHUNK
    resident = dict(pipeline_mode=pl.Buffered(1))
    in_specs = [pl.BlockSpec((tm, D), row)]
    args = [h2]
    if halo:
        r16 = tm // 16
        in_specs += [pl.BlockSpec((16, D), lambda i: (jnp.maximum(i * r16 - 1, 0), 0)),
                     pl.BlockSpec((16, D), lambda i: (jnp.minimum((i + 1) * r16, t // 16 - 1), 0))]
        args += [h2, h2]
    in_specs += [pl.BlockSpec((tm, D), row), pl.BlockSpec((None, 1, 6 * D), mod_map),
                 pl.BlockSpec((2 * nchunk, D, FFN_CHUNK), const3, **resident),
                 pl.BlockSpec((2 * nchunk, 3, FFN_CHUNK), const3),
                 pl.BlockSpec((nchunk, FFN_CHUNK, D), const3, **resident)]
    args += [x1, mod3, lp["w_up"], lp["ffn_conv"], lp["w_down"]]
    if final:
        in_specs.append(pl.BlockSpec((1, D), const))
        args.append(final_g)
    urows = tm + 32 if halo else (tm // seq_len) * (seq_len + 16)
    scratch = [pltpu.VMEM((tm, D), F32), pltpu.VMEM((2, urows, FFN_CHUNK), F32),
               pltpu.VMEM((2, urows, FFN_CHUNK), F32)]
    if halo:
        scratch.append(pltpu.VMEM((tm + 32, D), BF16))
    return pl.pallas_call(
        functools.partial(_ffn_kernel, halo=halo, tm=tm, seq_len=seq_len, final=final),
        grid=(t // tm,),
        in_specs=in_specs,
        out_specs=pl.BlockSpec((tm, D), row),
        out_shape=jax.ShapeDtypeStruct((t, D), F32),
        scratch_shapes=scratch,
        compiler_params=_cparams(("parallel",)),
        name="ffn_lat" if lat else "ffn_ctx",
    )(*args)


def _rope_tables(seq_len):
    rows = seq_len // GRID_W
    row = jnp.repeat(jnp.arange(rows, dtype=F32), GRID_W)
    col = jnp.tile(jnp.arange(GRID_W, dtype=F32), rows)
    freqs = ROPE_BASE ** (-jnp.arange(ROPE_FREQS, dtype=F32) / ROPE_FREQS)
    ang = jnp.stack([row[:, None] * freqs, col[:, None] * freqs], axis=1)
    cos, sin = jnp.cos(ang), jnp.sin(ang)
    zero = jnp.zeros_like(sin)
    c64 = jnp.stack([cos, cos], axis=2).reshape(seq_len, HEAD_DIM)
    sneg64 = jnp.stack([-sin, zero], axis=2).reshape(seq_len, HEAD_DIM)
    spos64 = jnp.stack([zero, sin], axis=2).reshape(seq_len, HEAD_DIM)
    return tuple(jnp.tile(tb, (1, 2)) for tb in (c64, sneg64, spos64))


def _layer_params(l, w_in, w_out, norm_mix, norm_ffn, attn_sink, sc_conv, ssm_lam_re, ssm_lam_im,
                  ssm_log_dt, ssm_b_re, ssm_b_im, ssm_c_re, ssm_c_im, ssm_d, ssm_w_glu,
                  ffn_w_up, ffn_conv, ffn_w_down):
    nchunk = D_FF // FFN_CHUNK
    eye = jnp.eye(SSM_GROUPS, dtype=F32)
    bdiag = lambda b: jnp.einsum("dgnc,gh->dgchn", b, eye).reshape(2, SSM_DIM, SSM_N)
    cdiag = lambda c: jnp.einsum("dgcn,gh->dgnhc", c, eye).reshape(2, SSM_N, SSM_DIM)
    scan = {
        "lam_re": ssm_lam_re[l].reshape(2, 1, SSM_N),
        "lam_im": ssm_lam_im[l].reshape(2, 1, SSM_N),
        "log_dt": jnp.repeat(ssm_log_dt[l], SSM_STATE, axis=-1).reshape(2, 1, SSM_N),
        "b_re": bdiag(ssm_b_re[l]),
        "b_im": bdiag(ssm_b_im[l]),
        "cblk": jnp.concatenate([cdiag(ssm_c_re[l]), -cdiag(ssm_c_im[l])], axis=1).astype(BF16),
    }
    return {
        "w_in": w_in[l].astype(BF16),
        "norm_mix": norm_mix[l].reshape(1, D),
        "norm_ffn": norm_ffn[l].reshape(1, D),
        "sink": attn_sink[l],
        "sc_conv": sc_conv[l].T,
        "ssm_d": ssm_d[l].reshape(1, SSM_DIM),
        "w_glu": ssm_w_glu[l].astype(BF16),
        "w_out": w_out[l].astype(BF16),
        "w_up": jnp.transpose(ffn_w_up[l].astype(BF16).reshape(D, 2 * nchunk, FFN_CHUNK), (1, 0, 2)),
        "ffn_conv": jnp.transpose(ffn_conv[l].reshape(2 * nchunk, FFN_CHUNK, 3), (0, 2, 1)),
        "w_down": ffn_w_down[l].astype(BF16).reshape(nchunk, FFN_CHUNK, D),
        "scan": scan,
    }


def _dup_heads(c):
    b, s = c.shape[0], c.shape[1]
    c = c.astype(BF16)
    z = jnp.zeros_like(c)
    tiles = jnp.stack([jnp.concatenate([c, z], axis=-1), jnp.concatenate([z, c], axis=-1)], axis=3)
    return tiles.reshape(b, s, KV_HEADS * 2 * LANES)


def kernel(x_prompt, x_sample, cache_k, cache_v, state_ssm_re, state_ssm_im, c, c_ctx, norm_mix, norm_ffn, norm_final, w_ada, b_ada, w_in, w_out, attn_sink, sc_conv, ssm_lam_re, ssm_lam_im, ssm_log_dt, ssm_b_re, ssm_b_im, ssm_c_re, ssm_c_im, ssm_d, ssm_w_glu, ffn_w_up, ffn_conv, ffn_w_down):
    batch, seq = x_prompt.shape[0], x_prompt.shape[1]
    dec_batch, dec_seq = x_sample.shape[0], x_sample.shape[1]
    assert batch == SCAN_ROWS and seq == SCAN_LEN
    assert dec_batch * (dec_seq // SCAN_LEN) == SCAN_ROWS and dec_batch == 2

    cs = jnp.concatenate([c_ctx[None, :], c, jnp.zeros((8 - 1 - dec_batch, D), F32)], axis=0)
    mods = _adaln(cs, w_ada, b_ada)
    rope_tabs = _rope_tables(dec_seq)
    gfin = norm_final.reshape(1, D)

    xp = x_prompt.reshape(batch * seq, D)
    xs = x_sample.reshape(dec_batch * dec_seq, D)
    ks_out, vs_out, sre_out, sim_out = [], [], [], []
    for l in range(DEPTH):
        lp = _layer_params(l, w_in, w_out, norm_mix, norm_ffn, attn_sink, sc_conv, ssm_lam_re, ssm_lam_im,
                           ssm_log_dt, ssm_b_re, ssm_b_im, ssm_c_re, ssm_c_im, ssm_d, ssm_w_glu,
                           ffn_w_up, ffn_conv, ffn_w_down)
        mod3 = mods[l].reshape(8, 1, 6 * D)
        last = gfin if l == DEPTH - 1 else None

        q, kd, vd, kv, g3, su = _inproj(xp, mod3, lp["norm_mix"], lp["w_in"], None, batch * seq, seq)
        attn = _attn_ctx(lp["sink"], q, kd, vd, seq)
        yf, yb, fin = _scan(lp["scan"], su, True)
        x1, h2 = _mix(xp, mod3, attn, g3, su, yf, yb, lp, False, batch * seq, seq)
        xp = _ffn(h2, x1, mod3, lp, False, batch * seq, seq, last)
        ks_out.append(kv[:, 0:KV_DIM].reshape(batch, seq, KV_HEADS, HEAD_DIM))
        vs_out.append(kv[:, KV_DIM:].reshape(batch, seq, KV_HEADS, HEAD_DIM))
        fin = fin.reshape(2, batch, 2, SSM_GROUPS, SSM_STATE)
        sre_out.append(jnp.transpose(fin[:, :, 0], (1, 0, 2, 3)))
        sim_out.append(jnp.transpose(fin[:, :, 1], (1, 0, 2, 3)))

        q, kd, vd, g3, su = _inproj(xs, mod3, lp["norm_mix"], lp["w_in"], rope_tabs, dec_seq, dec_seq)
        attn = _attn_lat(lp["sink"], q, kd, vd, _dup_heads(cache_k[:, l]), _dup_heads(cache_v[:, l]),
                         dec_batch, dec_seq)
        (ez,) = _scan(lp["scan"], su, False)
        s0 = jnp.concatenate([state_ssm_re[:, l].reshape(dec_batch, 2, SSM_N),
                              state_ssm_im[:, l].reshape(dec_batch, 2, SSM_N)], axis=-1)
        yf, yb, _ = _scan(lp["scan"], su, True, s0=jnp.transpose(s0, (1, 0, 2)), ez=ez)
        x1, h2 = _mix(xs, mod3, attn, g3, su, yf, yb, lp, True, dec_seq, dec_seq)
        xs = _ffn(h2, x1, mod3, lp, True, dec_seq, dec_seq, last)

    return (xp.reshape(batch, seq, D), xs.reshape(dec_batch, dec_seq, D),
            jnp.stack(ks_out, axis=1), jnp.stack(vs_out, axis=1),
            jnp.stack(sre_out, axis=1), jnp.stack(sim_out, axis=1))
```

```python
import functools
import math

import jax
import jax.numpy as jnp
import numpy as np
from jax import lax
from jax.experimental import pallas as pl
from jax.experimental.pallas import tpu as pltpu

F32 = jnp.float32
BF16 = jnp.bfloat16

D = 1024
DEPTH = 2
GRID_W = 64
ATTN_DIM = 512
SC_DIM = 256
SSM_DIM = 256
HEAD_DIM = 64
N_HEADS = 8
KV_HEADS = 2
KV_DIM = 128
WINDOW = 128
Q_BLOCK = 128
ROPE_BASE = 10000.0
ROPE_FREQS = 16
SSM_CH = 16
SSM_GROUPS = 16
SSM_STATE = 64
SSM_N = SSM_GROUPS * SSM_STATE
IN_DIM = ATTN_DIM + 2 * KV_DIM + 3 * SC_DIM + SSM_DIM
D_FF = 2816
ATTN_SCALE = 1.0 / math.sqrt(HEAD_DIM)
RMS_EPS = 1e-6
NEG_BIG = -1e30

LANES = 128
SCAN_ROWS = 16
SCAN_LEN = 256
SCAN_TT = 32
SCAN_LW = 256
FFN_TF = 1408
FFN_SUB = (512, 512, 384)
assert D_FF % FFN_TF == 0 and sum(FFN_SUB) == FFN_TF
VMEM_LIMIT = 56 * 1024 * 1024


def _cparams(sem):
    return pltpu.CompilerParams(dimension_semantics=sem, vmem_limit_bytes=VMEM_LIMIT)


def _sigmoid(x):
    return 1.0 / (1.0 + jnp.exp(-x))


def _rms_mod(x, g, scale, shift):
    y = x * lax.rsqrt(jnp.mean(x * x, axis=-1, keepdims=True) + RMS_EPS) * g
    return y * (1.0 + scale) + shift


def _split_bf16(v):
    hi = v.astype(BF16)
    lo = (v - hi.astype(F32)).astype(BF16)
    return hi, lo


def _dot(a, b):
    return jnp.dot(a, b, preferred_element_type=F32)


def _dot_t(a, b):
    return lax.dot_general(a, b, (((1,), (1,)), ((), ())), preferred_element_type=F32)


def _layer_spec(shape, l):
    zeros = (0,) * len(shape)
    return pl.BlockSpec((None,) + tuple(shape), lambda *_: (l,) + zeros)


def _half_tiles(t):
    lo = lax.broadcasted_iota(jnp.int32, t.shape, 1) < HEAD_DIM
    tr = pltpu.roll(t, HEAD_DIM, 1)
    return [jnp.where(lo, t, 0.0).astype(BF16), jnp.where(lo, 0.0, tr).astype(BF16),
            jnp.where(lo, tr, 0.0).astype(BF16), jnp.where(lo, 0.0, t).astype(BF16)]


def _adaln_kernel(c_ref, w_ref, b_ref, o_ref):
    c = c_ref[...]
    s = c * _sigmoid(c)
    s_hi, s_lo = _split_bf16(s)
    w_hi, w_lo = _split_bf16(w_ref[...])
    o_ref[...] = _dot(s_hi, w_hi) + _dot(s_lo, w_hi) + _dot(s_hi, w_lo) + b_ref[...]


def _adaln(cs, w_ada, b_ada):
    tn = 1024
    return pl.pallas_call(
        _adaln_kernel,
        grid=(DEPTH, 6 * D // tn),
        in_specs=[
            pl.BlockSpec((8, D), lambda l, j: (0, 0)),
            pl.BlockSpec((None, D, tn), lambda l, j: (l, 0, j)),
            pl.BlockSpec((None, 1, tn), lambda l, j: (l, 0, j)),
        ],
        out_specs=pl.BlockSpec((None, 8, tn), lambda l, j: (l, 0, j)),
        out_shape=jax.ShapeDtypeStruct((DEPTH, 8, 6 * D), F32),
        compiler_params=_cparams(("parallel", "parallel")),
        name="adaln",
    )(cs, w_ada, b_ada.reshape(DEPTH, 1, 6 * D))


def _inproj_kernel(x_ref, mod_ref, g_ref, w_ref, *rest, rope):
    if rope:
        cos_ref, sneg_ref, spos_ref, q_ref, kd_ref, vd_ref, g3_ref, su_ref = rest
    else:
        q_ref, kd_ref, vd_ref, kv_ref, g3_ref, su_ref = rest
    m = mod_ref[...]
    h = _rms_mod(x_ref[...], g_ref[...], m[:, D:2 * D], m[:, 0:D]).astype(BF16)
    acc = _dot(h, w_ref[...])
    k = acc[:, 512:640]
    v = acc[:, 640:768]
    if rope:
        c, sn, sp = cos_ref[...], sneg_ref[...], spos_ref[...]

        def rot(t):
            return t * c + pltpu.roll(t, LANES - ROPE_FREQS, 1) * sn + pltpu.roll(t, ROPE_FREQS, 1) * sp

        for p in range(4):
            q_ref[:, LANES * p:LANES * (p + 1)] = (rot(acc[:, LANES * p:LANES * (p + 1)]) * ATTN_SCALE).astype(BF16)
        k = rot(k)
    else:
        q_ref[...] = (acc[:, 0:512] * ATTN_SCALE).astype(BF16)
        kv_ref[...] = acc[:, 512:768]
    for ref, t in ((kd_ref, k), (vd_ref, v)):
        for p, tile in enumerate(_half_tiles(t)):
            ref[:, LANES * p:LANES * (p + 1)] = tile
    g3_ref[...] = acc[:, 768:1536].astype(BF16)
    for r in range(acc.shape[0] // SCAN_LEN):
        su_ref[:, r * SSM_DIM:(r + 1) * SSM_DIM] = acc[r * SCAN_LEN:(r + 1) * SCAN_LEN, 1536:1792]


def _inproj(x, mods, prm, l, rope_tabs, tokens_per_mod, seq_len, tm=512):
    t = x.shape[0]
    rope = rope_tabs is not None
    tiles_per_mod = tokens_per_mod // tm
    tiles_per_seq = seq_len // tm
    if rope:
        mod_map = lambda i: (l, 1 + i // tiles_per_mod, 0, 0)
    else:
        mod_map = lambda i: (l, 0, 0, 0)
    row = lambda i: (i, 0)
    in_specs = [
        pl.BlockSpec((tm, D), row),
        pl.BlockSpec((None, None, 1, 6 * D), mod_map),
        _layer_spec((1, D), l),
        _layer_spec((D, IN_DIM), l),
    ]
    args = [x, mods, prm["norm_mix"], prm["w_in"]]
    if rope:
        in_specs += [pl.BlockSpec((tm, LANES), lambda i: (i % tiles_per_seq, 0))] * 3
        args += list(rope_tabs)
    out_shape = [jax.ShapeDtypeStruct((t, 512), BF16)] * 3
    out_specs = [pl.BlockSpec((tm, 512), row)] * 3
    if not rope:
        out_shape.append(jax.ShapeDtypeStruct((t, 256), F32))
        out_specs.append(pl.BlockSpec((tm, 256), row))
    out_shape += [jax.ShapeDtypeStruct((t, 768), BF16),
                  jax.ShapeDtypeStruct((SCAN_LEN, t // SCAN_LEN * SSM_DIM), F32)]
    out_specs += [pl.BlockSpec((tm, 768), row),
                  pl.BlockSpec((SCAN_LEN, tm // SCAN_LEN * SSM_DIM), lambda i: (0, i))]
    return pl.pallas_call(
        functools.partial(_inproj_kernel, rope=rope),
        grid=(t // tm,),
        in_specs=in_specs,
        out_specs=out_specs,
        out_shape=out_shape,
        compiler_params=_cparams(("parallel",)),
        name="inproj_lat" if rope else "inproj_ctx",
    )(*args)


def _attention(sink_ref, l, q_ref, o_ref, keys, vals, bias, nq):
    top = lax.broadcasted_iota(jnp.int32, (2 * nq, 1), 0) < nq
    scores = []
    for kvh in range(KV_HEADS):
        q2 = jnp.concatenate([q_ref[:, LANES * (2 * kvh + pp):LANES * (2 * kvh + pp + 1)] for pp in range(2)],
                             axis=0)
        scores.append([_dot_t(q2, keys[kvh][half]) for half in range(2)])
    for kvh in range(KV_HEADS):
        acc = None
        for half in range(2):
            sk = jnp.where(top, sink_ref[l, 4 * kvh + half], sink_ref[l, 4 * kvh + 2 + half])
            s = scores[kvh][half]
            if bias is not None:
                nb_ = bias.shape[1]
                s = jnp.concatenate([s[:, 0:nb_] + bias, s[:, nb_:]], axis=1)
            m = jnp.maximum(jnp.max(s, axis=-1, keepdims=True), sk)
            e = jnp.exp(s - m)
            den = jnp.sum(e, axis=-1, keepdims=True) + jnp.exp(sk - m)
            o = _dot(e.astype(BF16), vals[kvh][half]) / den
            acc = o if acc is None else acc + o
        o_ref[:, LANES * 2 * kvh:LANES * (2 * kvh + 1)] = acc[0:nq].astype(BF16)
        o_ref[:, LANES * (2 * kvh + 1):LANES * (2 * kvh + 2)] = acc[nq:2 * nq].astype(BF16)


def _kv_tiles(kvh):
    return [slice(LANES * (2 * kvh + h), LANES * (2 * kvh + h + 1)) for h in range(2)]


def _attn_ctx_kernel(sink_ref, q_ref, kd_ref, vd_ref, o_ref, *, l):
    keys = [[kd_ref[:, t] for t in _kv_tiles(kvh)] for kvh in range(KV_HEADS)]
    vals = [[vd_ref[:, t] for t in _kv_tiles(kvh)] for kvh in range(KV_HEADS)]
    _attention(sink_ref, l, q_ref, o_ref, keys, vals, None, q_ref.shape[0])


def _attn_ctx(sink, l, q, kd, vd, seq_len):
    t = q.shape[0]
    row = lambda b: (b, 0)
    return pl.pallas_call(
        functools.partial(_attn_ctx_kernel, l=l),
        grid=(t // seq_len,),
        in_specs=[
            pl.BlockSpec(memory_space=pltpu.SMEM),
            pl.BlockSpec((seq_len, 512), row),
            pl.BlockSpec((seq_len, 512), row),
            pl.BlockSpec((seq_len, 512), row),
        ],
        out_specs=pl.BlockSpec((seq_len, 512), row),
        out_shape=jax.ShapeDtypeStruct((t, 512), BF16),
        compiler_params=_cparams(("parallel",)),
        name="attn_ctx",
    )(sink, q, kd, vd)


def _attn_lat_kernel(sink_ref, q_ref, kp_ref, kc_ref, kn_ref, vp_ref, vc_ref, vn_ref, ck_ref, cv_ref, o_ref, *, l):
    i = pl.program_id(1)
    nb = pl.num_programs(1)
    r = lax.broadcasted_iota(jnp.int32, (Q_BLOCK, Q_BLOCK), 0)
    j = lax.broadcasted_iota(jnp.int32, (Q_BLOCK, Q_BLOCK), 1)
    bias = jnp.concatenate([
        jnp.where(jnp.logical_and(j >= r, i > 0), 0.0, NEG_BIG),
        jnp.where(jnp.logical_and(j <= r, i < nb - 1), 0.0, NEG_BIG)], axis=1)
    bias = jnp.concatenate([bias, bias], axis=0)
    ck_tiles = _half_tiles(ck_ref[...])
    cv_tiles = _half_tiles(cv_ref[...])

    def gather(refs, ctx_tiles, kvh, h):
        t = _kv_tiles(kvh)[h]
        return jnp.concatenate([ref[:, t] for ref in refs] + [ctx_tiles[2 * kvh + h]], axis=0)

    keys = [[gather((kp_ref, kn_ref, kc_ref), ck_tiles, kvh, h) for h in range(2)] for kvh in range(KV_HEADS)]
    vals = [[gather((vp_ref, vn_ref, vc_ref), cv_tiles, kvh, h) for h in range(2)] for kvh in range(KV_HEADS)]
    _attention(sink_ref, l, q_ref, o_ref, keys, vals, bias, Q_BLOCK)


def _attn_lat(sink, l, q, kd, vd, cache_k, cache_v, batch, seq_len):
    nb = seq_len // Q_BLOCK
    past = cache_k.shape[2]
    cur = lambda b, i: (b * nb + i, 0)
    prev = lambda b, i: (b * nb + jnp.maximum(i - 1, 0), 0)
    nxt = lambda b, i: (b * nb + jnp.minimum(i + 1, nb - 1), 0)
    kvspec = lambda f: pl.BlockSpec((Q_BLOCK, 512), f)
    cspec = pl.BlockSpec((None, None, past, KV_DIM), lambda b, i: (b, l, 0, 0))
    ck = cache_k.reshape(batch, DEPTH, past, KV_DIM)
    cv = cache_v.reshape(batch, DEPTH, past, KV_DIM)
    return pl.pallas_call(
        functools.partial(_attn_lat_kernel, l=l),
        grid=(batch, nb),
        in_specs=[
            pl.BlockSpec(memory_space=pltpu.SMEM),
            pl.BlockSpec((Q_BLOCK, 512), cur),
            kvspec(prev), kvspec(cur), kvspec(nxt),
            kvspec(prev), kvspec(cur), kvspec(nxt),
            cspec, cspec,
        ],
        out_specs=pl.BlockSpec((Q_BLOCK, 512), cur),
        out_shape=jax.ShapeDtypeStruct((batch * seq_len, 512), BF16),
        compiler_params=_cparams(("parallel", "parallel")),
        name="attn_lat",
    )(sink, q, kd, kd, kd, vd, vd, vd, ck, cv)


def _scan_kernel(lre_ref, lim_ref, ldt_ref, bre_ref, bim_ref, cblk_ref, *rest, emit_y, chain):
    rest = list(rest)
    if chain:
        s0_ref, ez_ref = rest[:2]
        rest = rest[2:]
    suf_ref, sub_ref = rest[:2]
    rest = rest[2:]
    if emit_y:
        y_refs = rest[:2]
        rest = rest[2:]
    fin_ref, a_scr, bb_scr, h_scr, bu_scr, u_scr = rest[:6]
    if emit_y:
        y_scr = rest[6]
    i = pl.program_id(0)
    tt = SCAN_TT
    n = SSM_N

    @pl.when(i == 0)
    def _prologue():
        for d in range(2):
            lr, li = lre_ref[d], lim_ref[d]
            dt = jnp.exp(ldt_ref[d])
            mag = jnp.exp(lr * dt)
            ar, ai = mag * jnp.cos(li * dt), mag * jnp.sin(li * dt)
            den = lr * lr + li * li
            fr = ((ar - 1.0) * lr + ai * li) / den
            fi = (ai * lr - (ar - 1.0) * li) / den
            bre, bim = bre_ref[d], bim_ref[d]
            bb_scr[d, :, 0:n] = (fr * bre - fi * bim).astype(BF16)
            bb_scr[d, :, n:2 * n] = (fr * bim + fi * bre).astype(BF16)
            a_scr[d, 0] = jnp.broadcast_to(ar, (8, n))
            a_scr[d, 1] = jnp.broadcast_to(ai, (8, n))
            if chain:
                pr, pi_ = ar, ai
                for _ in range(8):
                    pr, pi_ = pr * pr - pi_ * pi_, 2.0 * pr * pi_
                chunks = SCAN_ROWS // 2
                for b in range(2):
                    hr = s0_ref[d, b:b + 1, 0:n]
                    hi = s0_ref[d, b:b + 1, n:2 * n]
                    order = range(chunks) if d == 0 else range(chunks - 1, -1, -1)
                    for c in order:
                        rw = b * chunks + c
                        h_scr[d, rw:rw + 1, 0:n] = hr
                        h_scr[d, rw:rw + 1, n:2 * n] = hi
                        er = ez_ref[d, rw:rw + 1, 0:n]
                        ei = ez_ref[d, rw:rw + 1, n:2 * n]
                        hr, hi = pr * hr - pi_ * hi + er, pr * hi + pi_ * hr + ei
            else:
                h_scr[d] = jnp.zeros((SCAN_ROWS, 2 * n), F32)

    ntile = n // LANES
    group = SCAN_LW // LANES

    def project_in(d):
        su_ref = suf_ref if d == 0 else sub_ref
        for r in range(SCAN_ROWS):
            for sl in range(SSM_DIM // LANES):
                col = r * SSM_DIM + sl * LANES
                u_scr[d, sl, pl.ds(r, tt, stride=SCAN_ROWS), :] = su_ref[:, col:col + LANES]
        u = jnp.concatenate([u_scr[d, sl] for sl in range(SSM_DIM // LANES)], axis=1).astype(BF16)
        bu_scr[d] = _dot(u, bb_scr[d])

    def recur(d):
        for c0 in range(0, ntile, group):
            lre = slice(c0 * LANES, (c0 + group) * LANES)
            lim = slice(n + c0 * LANES, n + (c0 + group) * LANES)
            ar, ai = a_scr[d, 0, :, lre], a_scr[d, 1, :, lre]
            halves = [slice(0, 8), slice(8, 16)]
            hr = [h_scr[d, hs, lre] for hs in halves]
            hi = [h_scr[d, hs, lim] for hs in halves]
            steps = range(tt) if d == 0 else range(tt - 1, -1, -1)
            for s in steps:
                for k in range(2):
                    rows = slice(s * SCAN_ROWS + 8 * k, s * SCAN_ROWS + 8 * k + 8)
                    br, bi = bu_scr[d, rows, lre], bu_scr[d, rows, lim]
                    hr[k], hi[k] = ar * hr[k] - ai * hi[k] + br, ar * hi[k] + ai * hr[k] + bi
                    bu_scr[d, rows, lre] = hr[k]
                    bu_scr[d, rows, lim] = hi[k]
            for k, hs in enumerate(halves):
                h_scr[d, hs, lre] = hr[k]
                h_scr[d, hs, lim] = hi[k]

    def project_out(d):
        y = _dot(bu_scr[d].astype(BF16), cblk_ref[d])
        for sl in range(SSM_DIM // LANES):
            y_scr[d, sl] = y[:, sl * LANES:(sl + 1) * LANES]
        for r in range(SCAN_ROWS):
            for sl in range(SSM_DIM // LANES):
                col = r * SSM_DIM + sl * LANES
                y_refs[d][:, col:col + LANES] = y_scr[d, sl, pl.ds(r, tt, stride=SCAN_ROWS), :].astype(BF16)

    project_in(0)
    project_in(1)
    recur(0)
    if emit_y:
        project_out(0)
    recur(1)
    if emit_y:
        project_out(1)

    @pl.when(i == pl.num_programs(0) - 1)
    def _final():
        fin_ref[...] = h_scr[...]


def _scan(prm, l, su_tm, emit_y, s0=None, ez=None):
    chain = s0 is not None
    nt = SCAN_LEN // SCAN_TT
    n = SSM_N
    full3 = lambda shape: pl.BlockSpec(shape, lambda i: (0, 0, 0))
    in_specs = [_layer_spec((2, 1, n), l)] * 3 + [_layer_spec((2, SSM_DIM, n), l)] * 2 + [
        _layer_spec((2, 2 * n, SSM_DIM), l)]
    args = [prm["lam_re"], prm["lam_im"], prm["log_dt"], prm["b_re"], prm["b_im"], prm["cblk"]]
    if chain:
        in_specs += [full3((2, 2, 2 * n)), full3((2, SCAN_ROWS, 2 * n))]
        args += [s0, ez]
    tblk = (SCAN_TT, SCAN_ROWS * SSM_DIM)
    fwd = lambda i: (i, 0)
    bwd = lambda i: (nt - 1 - i, 0)
    in_specs += [pl.BlockSpec(tblk, fwd), pl.BlockSpec(tblk, bwd)]
    args += [su_tm, su_tm]
    out_shape, out_specs = [], []
    scratch = [
        pltpu.VMEM((2, 2, 8, n), F32),
        pltpu.VMEM((2, SSM_DIM, 2 * n), BF16),
        pltpu.VMEM((2, SCAN_ROWS, 2 * n), F32),
        pltpu.VMEM((2, SCAN_ROWS * SCAN_TT, 2 * n), F32),
        pltpu.VMEM((2, SSM_DIM // LANES, SCAN_ROWS * SCAN_TT, LANES), F32),
    ]
    if emit_y:
        yshape = jax.ShapeDtypeStruct((SCAN_LEN, SCAN_ROWS * SSM_DIM), BF16)
        out_shape += [yshape, yshape]
        out_specs += [pl.BlockSpec(tblk, fwd), pl.BlockSpec(tblk, bwd)]
        scratch.append(pltpu.VMEM((2, SSM_DIM // LANES, SCAN_ROWS * SCAN_TT, LANES), F32))
    out_shape.append(jax.ShapeDtypeStruct((2, SCAN_ROWS, 2 * n), F32))
    out_specs.append(full3((2, SCAN_ROWS, 2 * n)))
    return pl.pallas_call(
        functools.partial(_scan_kernel, emit_y=emit_y, chain=chain),
        grid=(nt,),
        in_specs=in_specs,
        out_specs=out_specs,
        out_shape=out_shape,
        scratch_shapes=scratch,
        compiler_params=_cparams(("arbitrary",)),
        name="scan_chain" if chain else ("scan_y" if emit_y else "scan_state"),
    )(*args)


def _shifted(scr_ref, off, rows, pos, seq_len, w):
    mid = scr_ref[off:off + rows, :]
    up = jnp.where(pos != 0, scr_ref[off - 1:off - 1 + rows, :], 0.0)
    dn = jnp.where(pos != seq_len - 1, scr_ref[off + 1:off + 1 + rows, :], 0.0)
    return w[0:1, :] * up + w[1:2, :] * mid + w[2:3, :] * dn


def _mix_kernel(x_ref, mod_ref, attn_ref, g3_ref, *rest, halo, tm, seq_len):
    rest = list(rest)
    if halo:
        gp_ref, gn_ref = rest[:2]
        rest = rest[2:]
    (su_ref, yf_ref, yb_ref, wc_ref, dsk_ref, wglu_ref, woa_ref, woc_ref, wos_ref, gn2_ref,
     x1_ref, h2_ref, z_scr) = rest
    i = pl.program_id(0)
    m = mod_ref[...]
    g1, sh2, sc2 = m[:, 2 * D:3 * D], m[:, 3 * D:4 * D], m[:, 4 * D:5 * D]
    pos = (i * tm + lax.broadcasted_iota(jnp.int32, (tm, 1), 0)) % seq_len

    def gate_prod(ref):
        return ref[:, SC_DIM:2 * SC_DIM].astype(F32) * ref[:, 2 * SC_DIM:3 * SC_DIM].astype(F32)

    gb = g3_ref[:, 0:SC_DIM].astype(F32)
    z_scr[16:16 + tm, :] = gate_prod(g3_ref)
    if halo:
        z_scr[0:16, :] = gate_prod(gp_ref)
        z_scr[16 + tm:32 + tm, :] = gate_prod(gn_ref)
    else:
        z_scr[0:16, :] = jnp.zeros((16, SC_DIM), F32)
        z_scr[16 + tm:32 + tm, :] = jnp.zeros((16, SC_DIM), F32)
    conv = gb * _shifted(z_scr, 16, tm, pos, seq_len, wc_ref[...])

    y = dsk_ref[...] * su_ref[...] + yf_ref[...].astype(F32) + yb_ref[...].astype(F32)
    zz = 0.5 * y * (1.0 + jnp.tanh(math.sqrt(2.0 / math.pi) * (y + 0.044715 * (y * y * y))))
    ssm = zz * _sigmoid(_dot(zz.astype(BF16), wglu_ref[...]))

    mix = (_dot(attn_ref[...], woa_ref[...]) + _dot(conv.astype(BF16), woc_ref[...])
           + _dot(ssm.astype(BF16), wos_ref[...]))
    x1 = x_ref[...] + g1 * mix
    x1_ref[...] = x1
    h2_ref[...] = _rms_mod(x1, gn2_ref[...], sc2, sh2).astype(BF16)


def _mix(x, mods, attn, g3, su, yf, yb, prm, l, lat, tokens_per_mod, seq_len, tm=256):
    t = x.shape[0]
    halo = seq_len > tm
    tiles_per_mod = tokens_per_mod // tm
    mod_map = (lambda i: (l, 1 + i // tiles_per_mod, 0, 0)) if lat else (lambda i: (l, 0, 0, 0))
    row = lambda i: (i, 0)
    in_specs = [pl.BlockSpec((tm, D), row), pl.BlockSpec((None, None, 1, 6 * D), mod_map),
                pl.BlockSpec((tm, 512), row), pl.BlockSpec((tm, 768), row)]
    args = [x, mods, attn, g3]
    if halo:
        r16 = tm // 16
        in_specs += [pl.BlockSpec((16, 768), lambda i: (jnp.maximum(i * r16 - 1, 0), 0)),
                     pl.BlockSpec((16, 768), lambda i: (jnp.minimum((i + 1) * r16, t // 16 - 1), 0))]
        args += [g3, g3]
    assert tm == SCAN_LEN
    in_specs += [pl.BlockSpec((SCAN_LEN, SSM_DIM), lambda i: (0, i))] * 3
    args += [su, yf, yb]
    in_specs += [_layer_spec((3, SC_DIM), l), _layer_spec((1, SSM_DIM), l), _layer_spec((SSM_DIM, SSM_DIM), l),
                 pl.BlockSpec((None, ATTN_DIM, D), lambda i: (l, 0, 0)),
                 pl.BlockSpec((None, SC_DIM, D), lambda i: (l, ATTN_DIM // SC_DIM, 0)),
                 pl.BlockSpec((None, SSM_DIM, D), lambda i: (l, (ATTN_DIM + SC_DIM) // SSM_DIM, 0)),
                 _layer_spec((1, D), l)]
    args += [prm["sc_conv"], prm["ssm_d"], prm["w_glu"], prm["w_out"], prm["w_out"], prm["w_out"],
             prm["norm_ffn"]]
    return pl.pallas_call(
        functools.partial(_mix_kernel, halo=halo, tm=tm, seq_len=seq_len),
        grid=(t // tm,),
        in_specs=in_specs,
        out_specs=[pl.BlockSpec((tm, D), row), pl.BlockSpec((tm, D), row)],
        out_shape=[jax.ShapeDtypeStruct((t, D), F32), jax.ShapeDtypeStruct((t, D), BF16)],
        scratch_shapes=[pltpu.VMEM((tm + 32, SC_DIM), F32)],
        compiler_params=_cparams(("parallel",)),
        name="mix_lat" if lat else "mix_ctx",
    )(*args)


def _ffn_kernel(h_ref, *rest, halo, tm, seq_len, final):
    rest = list(rest)
    if halo:
        hp_ref, hn_ref = rest[:2]
        rest = rest[2:]
    x1_ref, mod_ref, wa_ref, wg_ref, ca_ref, cg_ref, wd_ref = rest[:7]
    rest = rest[7:]
    if final:
        gf_ref = rest[0]
        rest = rest[1:]
    o_ref, acc_scr = rest[:2]
    rest = rest[2:]
    nsub = len(FFN_SUB)
    ua_scrs, ug_scrs = rest[:nsub], rest[nsub:2 * nsub]
    i = pl.program_id(0)
    j = pl.program_id(1)

    if halo:
        hcat_scr = rest[2 * nsub]

        @pl.when(j == 0)
        def _stage():
            hcat_scr[0:16, :] = hp_ref[...]
            hcat_scr[16:16 + tm, :] = h_ref[...]
            hcat_scr[16 + tm:32 + tm, :] = hn_ref[...]

        hh = hcat_scr[...]
        pad, seg_len, nseg = 16, tm, 1
        tiles_per_seq = seq_len // tm
        keep_prev = (i % tiles_per_seq != 0).astype(F32)
        keep_next = (i % tiles_per_seq != tiles_per_seq - 1).astype(F32)
    else:
        hh = h_ref[...]
        pad, seg_len, nseg = 8, seq_len, tm // seq_len
    seg_rows = seg_len + 2 * pad
    cols = [sum(FFN_SUB[:c]) for c in range(nsub)]

    def up(c):
        width = FFN_SUB[c]
        for u_scr, w_ref in ((ua_scrs[c], wa_ref), (ug_scrs[c], wg_ref)):
            u = _dot(hh, w_ref[:, cols[c]:cols[c] + width])
            if halo:
                u_scr[...] = u
                u_scr[8:16, :] = u_scr[8:16, :] * keep_prev
                u_scr[16 + tm:24 + tm, :] = u_scr[16 + tm:24 + tm, :] * keep_next
            else:
                for sg in range(nseg):
                    r0 = sg * seg_rows
                    u_scr[r0:r0 + pad, :] = jnp.zeros((pad, width), F32)
                    u_scr[r0 + pad:r0 + pad + seg_len, :] = u[sg * seg_len:(sg + 1) * seg_len]
                    u_scr[r0 + pad + seg_len:r0 + seg_rows, :] = jnp.zeros((pad, width), F32)

    def conv(c, u_scr, cw_ref):
        w = cw_ref[:, cols[c]:cols[c] + FFN_SUB[c]]
        segs = []
        for sg in range(nseg):
            r0 = sg * seg_rows + pad
            segs.append(w[0:1, :] * u_scr[r0 - 1:r0 - 1 + seg_len, :]
                        + w[1:2, :] * u_scr[r0:r0 + seg_len, :]
                        + w[2:3, :] * u_scr[r0 + 1:r0 + 1 + seg_len, :])
        return segs[0] if nseg == 1 else jnp.concatenate(segs, axis=0)

    def down(c):
        a = conv(c, ua_scrs[c], ca_ref)
        g = conv(c, ug_scrs[c], cg_ref)
        act = (a * (g * _sigmoid(g))).astype(BF16)
        acc_scr[...] += _dot(act, wd_ref[cols[c]:cols[c] + FFN_SUB[c], :])

    @pl.when(j == 0)
    def _init():
        acc_scr[...] = jnp.zeros((tm, D), F32)

    up(0)
    for c in range(nsub):
        if c + 1 < nsub:
            up(c + 1)
        down(c)

    @pl.when(j == pl.num_programs(1) - 1)
    def _finish():
        g2 = mod_ref[...][:, 5 * D:6 * D]
        x2 = x1_ref[...] + g2 * acc_scr[...]
        if final:
            x2 = x2 * lax.rsqrt(jnp.mean(x2 * x2, axis=-1, keepdims=True) + RMS_EPS) * gf_ref[...]
        o_ref[...] = x2


def _ffn(h2, x1, mods, prm, l, lat, tokens_per_mod, seq_len, final_g, tm=512):
    t = h2.shape[0]
    tf = FFN_TF
    halo = seq_len > tm
    final = final_g is not None
    nj = D_FF // tf
    tiles_per_mod = tokens_per_mod // tm
    mod_map = (lambda i, j: (l, 1 + i // tiles_per_mod, 0, 0)) if lat else (lambda i, j: (l, 0, 0, 0))
    row = lambda i, j: (i, 0)
    in_specs = [pl.BlockSpec((tm, D), row)]
    args = [h2]
    if halo:
        r16 = tm // 16
        in_specs += [pl.BlockSpec((16, D), lambda i, j: (jnp.maximum(i * r16 - 1, 0), 0)),
                     pl.BlockSpec((16, D), lambda i, j: (jnp.minimum((i + 1) * r16, t // 16 - 1), 0))]
        args += [h2, h2]
    in_specs += [pl.BlockSpec((tm, D), row), pl.BlockSpec((None, None, 1, 6 * D), mod_map),
                 pl.BlockSpec((None, D, tf), lambda i, j: (l, 0, j)),
                 pl.BlockSpec((None, D, tf), lambda i, j: (l, 0, j + nj)),
                 pl.BlockSpec((None, 3, tf), lambda i, j: (l, 0, j)),
                 pl.BlockSpec((None, 3, tf), lambda i, j: (l, 0, j + nj)),
                 pl.BlockSpec((None, tf, D), lambda i, j: (l, j, 0))]
    args += [x1, mods, prm["w_up"], prm["w_up"], prm["ffn_conv"], prm["ffn_conv"], prm["w_down"]]
    if final:
        in_specs.append(pl.BlockSpec((1, D), lambda i, j: (0, 0)))
        args.append(final_g)
    urows = tm + 32 if halo else (tm // seq_len) * (seq_len + 16)
    scratch = [pltpu.VMEM((tm, D), F32)]
    scratch += [pltpu.VMEM((urows, w), F32) for w in FFN_SUB] * 2
    if halo:
        scratch.append(pltpu.VMEM((tm + 32, D), BF16))
    return pl.pallas_call(
        functools.partial(_ffn_kernel, halo=halo, tm=tm, seq_len=seq_len, final=final),
        grid=(t // tm, nj),
        in_specs=in_specs,
        out_specs=pl.BlockSpec((tm, D), row),
        out_shape=jax.ShapeDtypeStruct((t, D), F32),
        scratch_shapes=scratch,
        compiler_params=_cparams(("parallel", "arbitrary")),
        name="ffn_lat" if lat else "ffn_ctx",
    )(*args)


def _rope_tables(seq_len):
    rows = seq_len // GRID_W
    row = np.repeat(np.arange(rows, dtype=np.float32), GRID_W)
    col = np.tile(np.arange(GRID_W, dtype=np.float32), rows)
    freqs = (np.float32(ROPE_BASE) ** (-np.arange(ROPE_FREQS, dtype=np.float32) / np.float32(ROPE_FREQS)))
    freqs = freqs.astype(np.float32)
    ang = np.stack([row[:, None] * freqs, col[:, None] * freqs], axis=1).astype(np.float32)
    cos, sin = np.cos(ang).astype(np.float32), np.sin(ang).astype(np.float32)
    zero = np.zeros_like(sin)
    c64 = np.stack([cos, cos], axis=2).reshape(seq_len, HEAD_DIM)
    sneg64 = np.stack([-sin, zero], axis=2).reshape(seq_len, HEAD_DIM)
    spos64 = np.stack([zero, sin], axis=2).reshape(seq_len, HEAD_DIM)
    return tuple(jnp.asarray(np.tile(tb, (1, 2))) for tb in (c64, sneg64, spos64))


def _prep_params(w_in, w_out, norm_mix, norm_ffn, sc_conv, ssm_lam_re, ssm_lam_im, ssm_log_dt, ssm_b_re,
                 ssm_b_im, ssm_c_re, ssm_c_im, ssm_d, ssm_w_glu, ffn_w_up, ffn_conv, ffn_w_down):
    eye = jnp.eye(SSM_GROUPS, dtype=F32)
    bdiag = lambda b: jnp.einsum("ldgnc,gh->ldgchn", b, eye).reshape(DEPTH, 2, SSM_DIM, SSM_N)
    cdiag = lambda c: jnp.einsum("ldgcn,gh->ldgnhc", c, eye).reshape(DEPTH, 2, SSM_N, SSM_DIM)
    return {
        "w_in": w_in.astype(BF16),
        "norm_mix": norm_mix.reshape(DEPTH, 1, D),
        "norm_ffn": norm_ffn.reshape(DEPTH, 1, D),
        "sc_conv": jnp.transpose(sc_conv, (0, 2, 1)),
        "ssm_d": ssm_d.reshape(DEPTH, 1, SSM_DIM),
        "w_glu": ssm_w_glu.astype(BF16),
        "w_out": w_out.astype(BF16),
        "w_up": ffn_w_up.astype(BF16),
        "ffn_conv": jnp.transpose(ffn_conv, (0, 2, 1)),
        "w_down": ffn_w_down.astype(BF16),
        "lam_re": ssm_lam_re.reshape(DEPTH, 2, 1, SSM_N),
        "lam_im": ssm_lam_im.reshape(DEPTH, 2, 1, SSM_N),
        "log_dt": jnp.repeat(ssm_log_dt, SSM_STATE, axis=-1).reshape(DEPTH, 2, 1, SSM_N),
        "b_re": bdiag(ssm_b_re),
        "b_im": bdiag(ssm_b_im),
        "cblk": jnp.concatenate([cdiag(ssm_c_re), -cdiag(ssm_c_im)], axis=2).astype(BF16),
    }


def kernel(x_prompt, x_sample, cache_k, cache_v, state_ssm_re, state_ssm_im, c, c_ctx, norm_mix, norm_ffn, norm_final, w_ada, b_ada, w_in, w_out, attn_sink, sc_conv, ssm_lam_re, ssm_lam_im, ssm_log_dt, ssm_b_re, ssm_b_im, ssm_c_re, ssm_c_im, ssm_d, ssm_w_glu, ffn_w_up, ffn_conv, ffn_w_down):
    batch, seq = x_prompt.shape[0], x_prompt.shape[1]
    dec_batch, dec_seq = x_sample.shape[0], x_sample.shape[1]
    assert batch == SCAN_ROWS and seq == SCAN_LEN
    assert dec_batch * (dec_seq // SCAN_LEN) == SCAN_ROWS and dec_batch == 2

    cs = jnp.concatenate([c_ctx[None, :], c, jnp.zeros((8 - 1 - dec_batch, D), F32)], axis=0)
    mods = _adaln(cs, w_ada, b_ada).reshape(DEPTH, 8, 1, 6 * D)
    rope_tabs = _rope_tables(dec_seq)
    gfin = norm_final.reshape(1, D)
    prm = _prep_params(w_in, w_out, norm_mix, norm_ffn, sc_conv, ssm_lam_re, ssm_lam_im, ssm_log_dt, ssm_b_re,
                       ssm_b_im, ssm_c_re, ssm_c_im, ssm_d, ssm_w_glu, ffn_w_up, ffn_conv, ffn_w_down)
    s0_all = jnp.transpose(jnp.concatenate([state_ssm_re.reshape(dec_batch, DEPTH, 2, SSM_N),
                                            state_ssm_im.reshape(dec_batch, DEPTH, 2, SSM_N)], axis=-1),
                           (1, 2, 0, 3))

    xp = x_prompt.reshape(batch * seq, D)
    xs = x_sample.reshape(dec_batch * dec_seq, D)
    kv_out, fin_out = [], []
    for l in range(DEPTH):
        last = gfin if l == DEPTH - 1 else None

        q, kd, vd, kv, g3, su = _inproj(xp, mods, prm, l, None, batch * seq, seq)
        attn = _attn_ctx(attn_sink, l, q, kd, vd, seq)
        yf, yb, fin = _scan(prm, l, su, True)
        x1, h2 = _mix(xp, mods, attn, g3, su, yf, yb, prm, l, False, batch * seq, seq)
        xp = _ffn(h2, x1, mods, prm, l, False, batch * seq, seq, last)
        kv_out.append(kv)
        fin_out.append(fin)

        q, kd, vd, g3, su = _inproj(xs, mods, prm, l, rope_tabs, dec_seq, dec_seq)
        attn = _attn_lat(attn_sink, l, q, kd, vd, cache_k, cache_v, dec_batch, dec_seq)
        (ez,) = _scan(prm, l, su, False)
        yf, yb, _ = _scan(prm, l, su, True, s0=s0_all[l], ez=ez)
        x1, h2 = _mix(xs, mods, attn, g3, su, yf, yb, prm, l, True, dec_seq, dec_seq)
        xs = _ffn(h2, x1, mods, prm, l, True, dec_seq, dec_seq, last)

    kv_all = jnp.stack(kv_out, axis=0).reshape(DEPTH, batch, seq, 2, KV_HEADS, HEAD_DIM)
    kv_all = jnp.transpose(kv_all, (3, 1, 0, 2, 4, 5))
    fin_all = jnp.stack(fin_out, axis=0).reshape(DEPTH, 2, batch, 2, SSM_GROUPS, SSM_STATE)
    fin_all = jnp.transpose(fin_all, (3, 2, 0, 1, 4, 5))
    return (xp.reshape(batch, seq, D), xs.reshape(dec_batch, dec_seq, D),
            kv_all[0], kv_all[1], fin_all[0], fin_all[1])
```

```python
import functools
import math

import jax
import jax.numpy as jnp
import numpy as np
from jax import lax
from jax.experimental import pallas as pl
from jax.experimental.pallas import tpu as pltpu

F32 = jnp.float32
BF16 = jnp.bfloat16

D = 1024
DEPTH = 2
GRID_W = 64
ATTN_DIM = 512
SC_DIM = 256
SSM_DIM = 256
HEAD_DIM = 64
N_HEADS = 8
KV_HEADS = 2
KV_DIM = 128
WINDOW = 128
Q_BLOCK = 128
ROPE_BASE = 10000.0
ROPE_FREQS = 16
SSM_CH = 16
SSM_GROUPS = 16
SSM_STATE = 64
SSM_N = SSM_GROUPS * SSM_STATE
IN_DIM = ATTN_DIM + 2 * KV_DIM + 3 * SC_DIM + SSM_DIM
D_FF = 2816
ATTN_SCALE = 1.0 / math.sqrt(HEAD_DIM)
RMS_EPS = 1e-6
NEG_BIG = -1e30

LANES = 128
SCAN_ROWS = 16
SCAN_LEN = 256
SCAN_TT = 32
SCAN_LW = 256
FFN_TF = 1408
FFN_SUB = (512, 512, 384)
assert D_FF % FFN_TF == 0 and sum(FFN_SUB) == FFN_TF
VMEM_LIMIT = 56 * 1024 * 1024


def _cparams(sem):
    return pltpu.CompilerParams(dimension_semantics=sem, vmem_limit_bytes=VMEM_LIMIT)


def _sigmoid(x):
    return 1.0 / (1.0 + jnp.exp(-x))


def _rms_mod(x, g, scale, shift):
    y = x * lax.rsqrt(jnp.mean(x * x, axis=-1, keepdims=True) + RMS_EPS) * g
    return y * (1.0 + scale) + shift


def _split_bf16(v):
    hi = v.astype(BF16)
    lo = (v - hi.astype(F32)).astype(BF16)
    return hi, lo


def _dot(a, b):
    return jnp.dot(a, b, preferred_element_type=F32)


def _dot_t(a, b):
    return lax.dot_general(a, b, (((1,), (1,)), ((), ())), preferred_element_type=F32)


def _layer_spec(shape, l):
    zeros = (0,) * len(shape)
    return pl.BlockSpec((None,) + tuple(shape), lambda *_: (l,) + zeros)


def _half_tiles(t):
    lo = lax.broadcasted_iota(jnp.int32, t.shape, 1) < HEAD_DIM
    tr = pltpu.roll(t, HEAD_DIM, 1)
    return [jnp.where(lo, t, 0.0).astype(BF16), jnp.where(lo, 0.0, tr).astype(BF16),
            jnp.where(lo, tr, 0.0).astype(BF16), jnp.where(lo, 0.0, t).astype(BF16)]


def _adaln_kernel(c_ref, w_ref, b_ref, o_ref):
    c = c_ref[...]
    s = c * _sigmoid(c)
    s_hi, s_lo = _split_bf16(s)
    w_hi, w_lo = _split_bf16(w_ref[...])
    o_ref[...] = _dot(s_hi, w_hi) + _dot(s_lo, w_hi) + _dot(s_hi, w_lo) + b_ref[...]


def _adaln(cs, w_ada, b_ada):
    tn = 1024
    return pl.pallas_call(
        _adaln_kernel,
        grid=(DEPTH, 6 * D // tn),
        in_specs=[
            pl.BlockSpec((8, D), lambda l, j: (0, 0)),
            pl.BlockSpec((None, D, tn), lambda l, j: (l, 0, j)),
            pl.BlockSpec((None, 1, tn), lambda l, j: (l, 0, j)),
        ],
        out_specs=pl.BlockSpec((None, 8, tn), lambda l, j: (l, 0, j)),
        out_shape=jax.ShapeDtypeStruct((DEPTH, 8, 6 * D), F32),
        compiler_params=_cparams(("parallel", "parallel")),
        name="adaln",
    )(cs, w_ada, b_ada.reshape(DEPTH, 1, 6 * D))


def _inproj_kernel(x_ref, mod_ref, g_ref, w_ref, *rest, rope):
    if rope:
        cos_ref, sneg_ref, spos_ref, q_ref, kd_ref, vd_ref, g3_ref, su_ref, wbf_scr = rest
    else:
        q_ref, kd_ref, vd_ref, kv_ref, g3_ref, su_ref, wbf_scr = rest

    @pl.when(pl.program_id(0) == 0)
    def _cast_weights():
        wbf_scr[...] = w_ref[...].astype(BF16)

    m = mod_ref[...]
    h = _rms_mod(x_ref[...], g_ref[...], m[:, D:2 * D], m[:, 0:D]).astype(BF16)
    acc = _dot(h, wbf_scr[...])
    k = acc[:, 512:640]
    v = acc[:, 640:768]
    if rope:
        c, sn, sp = cos_ref[...], sneg_ref[...], spos_ref[...]

        def rot(t):
            return t * c + pltpu.roll(t, LANES - ROPE_FREQS, 1) * sn + pltpu.roll(t, ROPE_FREQS, 1) * sp

        for p in range(4):
            q_ref[:, LANES * p:LANES * (p + 1)] = (rot(acc[:, LANES * p:LANES * (p + 1)]) * ATTN_SCALE).astype(BF16)
        k = rot(k)
    else:
        q_ref[...] = (acc[:, 0:512] * ATTN_SCALE).astype(BF16)
        kv_ref[...] = acc[:, 512:768]
    for ref, t in ((kd_ref, k), (vd_ref, v)):
        for p, tile in enumerate(_half_tiles(t)):
            ref[:, LANES * p:LANES * (p + 1)] = tile
    g3_ref[...] = acc[:, 768:1536].astype(BF16)
    for r in range(acc.shape[0] // SCAN_LEN):
        su_ref[:, r * SSM_DIM:(r + 1) * SSM_DIM] = acc[r * SCAN_LEN:(r + 1) * SCAN_LEN, 1536:1792]


def _inproj(x, mods, prm, l, rope_tabs, tokens_per_mod, seq_len, tm=512):
    t = x.shape[0]
    rope = rope_tabs is not None
    tiles_per_mod = tokens_per_mod // tm
    tiles_per_seq = seq_len // tm
    if rope:
        mod_map = lambda i: (l, 1 + i // tiles_per_mod, 0, 0)
    else:
        mod_map = lambda i: (l, 0, 0, 0)
    row = lambda i: (i, 0)
    in_specs = [
        pl.BlockSpec((tm, D), row),
        pl.BlockSpec((None, None, 1, 6 * D), mod_map),
        _layer_spec((1, D), l),
        _layer_spec((D, IN_DIM), l),
    ]
    args = [x, mods, prm["norm_mix"], prm["w_in"]]
    if rope:
        in_specs += [pl.BlockSpec((tm, LANES), lambda i: (i % tiles_per_seq, 0))] * 3
        args += list(rope_tabs)
    out_shape = [jax.ShapeDtypeStruct((t, 512), BF16)] * 3
    out_specs = [pl.BlockSpec((tm, 512), row)] * 3
    if not rope:
        out_shape.append(jax.ShapeDtypeStruct((t, 256), F32))
        out_specs.append(pl.BlockSpec((tm, 256), row))
    out_shape += [jax.ShapeDtypeStruct((t, 768), BF16),
                  jax.ShapeDtypeStruct((SCAN_LEN, t // SCAN_LEN * SSM_DIM), F32)]
    out_specs += [pl.BlockSpec((tm, 768), row),
                  pl.BlockSpec((SCAN_LEN, tm // SCAN_LEN * SSM_DIM), lambda i: (0, i))]
    return pl.pallas_call(
        functools.partial(_inproj_kernel, rope=rope),
        grid=(t // tm,),
        in_specs=in_specs,
        out_specs=out_specs,
        out_shape=out_shape,
        scratch_shapes=[pltpu.VMEM((D, IN_DIM), BF16)],
        compiler_params=_cparams(("arbitrary",)),
        name="inproj_lat" if rope else "inproj_ctx",
    )(*args)


def _attention(sink_ref, l, q_ref, o_ref, keys, vals, bias, nq):
    top = lax.broadcasted_iota(jnp.int32, (2 * nq, 1), 0) < nq
    scores = []
    for kvh in range(KV_HEADS):
        q2 = jnp.concatenate([q_ref[:, LANES * (2 * kvh + pp):LANES * (2 * kvh + pp + 1)] for pp in range(2)],
                             axis=0)
        scores.append([_dot_t(q2, keys[kvh][half]) for half in range(2)])
    for kvh in range(KV_HEADS):
        acc = None
        for half in range(2):
            sk = jnp.where(top, sink_ref[l, 4 * kvh + half], sink_ref[l, 4 * kvh + 2 + half])
            s = scores[kvh][half]
            if bias is not None:
                nb_ = bias.shape[1]
                s = jnp.concatenate([s[:, 0:nb_] + bias, s[:, nb_:]], axis=1)
            m = jnp.maximum(jnp.max(s, axis=-1, keepdims=True), sk)
            e = jnp.exp(s - m)
            den = jnp.sum(e, axis=-1, keepdims=True) + jnp.exp(sk - m)
            o = _dot(e.astype(BF16), vals[kvh][half]) / den
            acc = o if acc is None else acc + o
        o_ref[:, LANES * 2 * kvh:LANES * (2 * kvh + 1)] = acc[0:nq].astype(BF16)
        o_ref[:, LANES * (2 * kvh + 1):LANES * (2 * kvh + 2)] = acc[nq:2 * nq].astype(BF16)


def _kv_tiles(kvh):
    return [slice(LANES * (2 * kvh + h), LANES * (2 * kvh + h + 1)) for h in range(2)]


def _attn_ctx_kernel(sink_ref, q_ref, kd_ref, vd_ref, o_ref, *, l):
    keys = [[kd_ref[:, t] for t in _kv_tiles(kvh)] for kvh in range(KV_HEADS)]
    vals = [[vd_ref[:, t] for t in _kv_tiles(kvh)] for kvh in range(KV_HEADS)]
    _attention(sink_ref, l, q_ref, o_ref, keys, vals, None, q_ref.shape[0])


def _attn_ctx(sink, l, q, kd, vd, seq_len):
    t = q.shape[0]
    row = lambda b: (b, 0)
    return pl.pallas_call(
        functools.partial(_attn_ctx_kernel, l=l),
        grid=(t // seq_len,),
        in_specs=[
            pl.BlockSpec(memory_space=pltpu.SMEM),
            pl.BlockSpec((seq_len, 512), row),
            pl.BlockSpec((seq_len, 512), row),
            pl.BlockSpec((seq_len, 512), row),
        ],
        out_specs=pl.BlockSpec((seq_len, 512), row),
        out_shape=jax.ShapeDtypeStruct((t, 512), BF16),
        compiler_params=_cparams(("parallel",)),
        name="attn_ctx",
    )(sink, q, kd, vd)


def _attn_lat_kernel(sink_ref, q_ref, kp_ref, kc_ref, kn_ref, vp_ref, vc_ref, vn_ref, ck_ref, cv_ref, o_ref, *, l):
    i = pl.program_id(1)
    nb = pl.num_programs(1)
    r = lax.broadcasted_iota(jnp.int32, (Q_BLOCK, Q_BLOCK), 0)
    j = lax.broadcasted_iota(jnp.int32, (Q_BLOCK, Q_BLOCK), 1)
    bias = jnp.concatenate([
        jnp.where(jnp.logical_and(j >= r, i > 0), 0.0, NEG_BIG),
        jnp.where(jnp.logical_and(j <= r, i < nb - 1), 0.0, NEG_BIG)], axis=1)
    bias = jnp.concatenate([bias, bias], axis=0)
    ck_tiles = _half_tiles(ck_ref[...])
    cv_tiles = _half_tiles(cv_ref[...])

    def gather(refs, ctx_tiles, kvh, h):
        t = _kv_tiles(kvh)[h]
        return jnp.concatenate([ref[:, t] for ref in refs] + [ctx_tiles[2 * kvh + h]], axis=0)

    keys = [[gather((kp_ref, kn_ref, kc_ref), ck_tiles, kvh, h) for h in range(2)] for kvh in range(KV_HEADS)]
    vals = [[gather((vp_ref, vn_ref, vc_ref), cv_tiles, kvh, h) for h in range(2)] for kvh in range(KV_HEADS)]
    _attention(sink_ref, l, q_ref, o_ref, keys, vals, bias, Q_BLOCK)


def _attn_lat(sink, l, q, kd, vd, cache_k, cache_v, batch, seq_len):
    nb = seq_len // Q_BLOCK
    past = cache_k.shape[2]
    cur = lambda b, i: (b * nb + i, 0)
    prev = lambda b, i: (b * nb + jnp.maximum(i - 1, 0), 0)
    nxt = lambda b, i: (b * nb + jnp.minimum(i + 1, nb - 1), 0)
    kvspec = lambda f: pl.BlockSpec((Q_BLOCK, 512), f)
    cspec = pl.BlockSpec((None, None, past, KV_DIM), lambda b, i: (b, l, 0, 0))
    ck = cache_k.reshape(batch, DEPTH, past, KV_DIM)
    cv = cache_v.reshape(batch, DEPTH, past, KV_DIM)
    return pl.pallas_call(
        functools.partial(_attn_lat_kernel, l=l),
        grid=(batch, nb),
        in_specs=[
            pl.BlockSpec(memory_space=pltpu.SMEM),
            pl.BlockSpec((Q_BLOCK, 512), cur),
            kvspec(prev), kvspec(cur), kvspec(nxt),
            kvspec(prev), kvspec(cur), kvspec(nxt),
            cspec, cspec,
        ],
        out_specs=pl.BlockSpec((Q_BLOCK, 512), cur),
        out_shape=jax.ShapeDtypeStruct((batch * seq_len, 512), BF16),
        compiler_params=_cparams(("parallel", "parallel")),
        name="attn_lat",
    )(sink, q, kd, kd, kd, vd, vd, vd, ck, cv)


def _scan_kernel(lre_ref, lim_ref, ldt_ref, bre_ref, bim_ref, cre_ref, cim_ref, *rest, emit_y, chain):
    rest = list(rest)
    if chain:
        s0_ref, ez_ref = rest[:2]
        rest = rest[2:]
    suf_ref, sub_ref = rest[:2]
    rest = rest[2:]
    if emit_y:
        y_refs = rest[:2]
        rest = rest[2:]
    fin_ref, a_scr, bb_scr, h_scr, bu_scr, u_scr = rest[:6]
    if emit_y:
        y_scr, ct_scr = rest[6:8]
    i = pl.program_id(0)
    tt = SCAN_TT
    n = SSM_N

    @pl.when(i == 0)
    def _prologue():
        row_g = lax.shift_right_logical(lax.broadcasted_iota(jnp.int32, (SSM_DIM, n), 0), 4)
        col_g = lax.shift_right_logical(lax.broadcasted_iota(jnp.int32, (SSM_DIM, n), 1), 6)
        own = row_g == col_g

        def blockdiag(ref, d):
            return jnp.where(own, jnp.concatenate([ref[d]] * SSM_GROUPS, axis=0), 0.0)

        for d in range(2):
            lr, li = lre_ref[d], lim_ref[d]
            dt = jnp.exp(ldt_ref[d])
            mag = jnp.exp(lr * dt)
            ar, ai = mag * jnp.cos(li * dt), mag * jnp.sin(li * dt)
            den = lr * lr + li * li
            fr = ((ar - 1.0) * lr + ai * li) / den
            fi = (ai * lr - (ar - 1.0) * li) / den
            bre, bim = blockdiag(bre_ref, d), blockdiag(bim_ref, d)
            bb_scr[d, :, 0:n] = (fr * bre - fi * bim).astype(BF16)
            bb_scr[d, :, n:2 * n] = (fr * bim + fi * bre).astype(BF16)
            if emit_y:
                ct_scr[d, :, 0:n] = blockdiag(cre_ref, d).astype(BF16)
                ct_scr[d, :, n:2 * n] = (-blockdiag(cim_ref, d)).astype(BF16)
            a_scr[d, 0] = jnp.broadcast_to(ar, (8, n))
            a_scr[d, 1] = jnp.broadcast_to(ai, (8, n))
            if chain:
                pr, pi_ = ar, ai
                for _ in range(8):
                    pr, pi_ = pr * pr - pi_ * pi_, 2.0 * pr * pi_
                chunks = SCAN_ROWS // 2
                for b in range(2):
                    hr = s0_ref[d, b:b + 1, 0:n]
                    hi = s0_ref[d, b:b + 1, n:2 * n]
                    order = range(chunks) if d == 0 else range(chunks - 1, -1, -1)
                    for c in order:
                        rw = b * chunks + c
                        h_scr[d, rw:rw + 1, 0:n] = hr
                        h_scr[d, rw:rw + 1, n:2 * n] = hi
                        er = ez_ref[d, rw:rw + 1, 0:n]
                        ei = ez_ref[d, rw:rw + 1, n:2 * n]
                        hr, hi = pr * hr - pi_ * hi + er, pr * hi + pi_ * hr + ei
            else:
                h_scr[d] = jnp.zeros((SCAN_ROWS, 2 * n), F32)

    ntile = n // LANES
    group = SCAN_LW // LANES

    def project_in(d):
        su_ref = suf_ref if d == 0 else sub_ref
        for r in range(SCAN_ROWS):
            for sl in range(SSM_DIM // LANES):
                col = r * SSM_DIM + sl * LANES
                u_scr[d, sl, pl.ds(r, tt, stride=SCAN_ROWS), :] = su_ref[:, col:col + LANES]
        u = jnp.concatenate([u_scr[d, sl] for sl in range(SSM_DIM // LANES)], axis=1).astype(BF16)
        bu_scr[d] = _dot(u, bb_scr[d])

    def recur(d):
        for c0 in range(0, ntile, group):
            lre = slice(c0 * LANES, (c0 + group) * LANES)
            lim = slice(n + c0 * LANES, n + (c0 + group) * LANES)
            ar, ai = a_scr[d, 0, :, lre], a_scr[d, 1, :, lre]
            halves = [slice(0, 8), slice(8, 16)]
            hr = [h_scr[d, hs, lre] for hs in halves]
            hi = [h_scr[d, hs, lim] for hs in halves]
            steps = range(tt) if d == 0 else range(tt - 1, -1, -1)
            for s in steps:
                for k in range(2):
                    rows = slice(s * SCAN_ROWS + 8 * k, s * SCAN_ROWS + 8 * k + 8)
                    br, bi = bu_scr[d, rows, lre], bu_scr[d, rows, lim]
                    hr[k], hi[k] = ar * hr[k] - ai * hi[k] + br, ar * hi[k] + ai * hr[k] + bi
                    bu_scr[d, rows, lre] = hr[k]
                    bu_scr[d, rows, lim] = hi[k]
            for k, hs in enumerate(halves):
                h_scr[d, hs, lre] = hr[k]
                h_scr[d, hs, lim] = hi[k]

    def project_out(d):
        y = _dot_t(bu_scr[d].astype(BF16), ct_scr[d])
        for sl in range(SSM_DIM // LANES):
            y_scr[d, sl] = y[:, sl * LANES:(sl + 1) * LANES]
        for r in range(SCAN_ROWS):
            for sl in range(SSM_DIM // LANES):
                col = r * SSM_DIM + sl * LANES
                y_refs[d][:, col:col + LANES] = y_scr[d, sl, pl.ds(r, tt, stride=SCAN_ROWS), :].astype(BF16)

    project_in(0)
    project_in(1)
    recur(0)
    if emit_y:
        project_out(0)
    recur(1)
    if emit_y:
        project_out(1)

    @pl.when(i == pl.num_programs(0) - 1)
    def _final():
        fin_ref[...] = h_scr[...]


def _scan(prm, l, su_tm, emit_y, s0=None, ez=None):
    chain = s0 is not None
    nt = SCAN_LEN // SCAN_TT
    n = SSM_N
    full3 = lambda shape: pl.BlockSpec(shape, lambda i: (0, 0, 0))
    in_specs = [_layer_spec((2, 1, n), l)] * 3 + [_layer_spec((2, SSM_CH, n), l)] * 4
    args = [prm["lam_re"], prm["lam_im"], prm["log_dt"], prm["b_re"], prm["b_im"], prm["c_re"], prm["c_im"]]
    if chain:
        in_specs += [full3((2, 2, 2 * n)), full3((2, SCAN_ROWS, 2 * n))]
        args += [s0, ez]
    tblk = (SCAN_TT, SCAN_ROWS * SSM_DIM)
    fwd = lambda i: (i, 0)
    bwd = lambda i: (nt - 1 - i, 0)
    in_specs += [pl.BlockSpec(tblk, fwd), pl.BlockSpec(tblk, bwd)]
    args += [su_tm, su_tm]
    out_shape, out_specs = [], []
    scratch = [
        pltpu.VMEM((2, 2, 8, n), F32),
        pltpu.VMEM((2, SSM_DIM, 2 * n), BF16),
        pltpu.VMEM((2, SCAN_ROWS, 2 * n), F32),
        pltpu.VMEM((2, SCAN_ROWS * SCAN_TT, 2 * n), F32),
        pltpu.VMEM((2, SSM_DIM // LANES, SCAN_ROWS * SCAN_TT, LANES), F32),
    ]
    if emit_y:
        yshape = jax.ShapeDtypeStruct((SCAN_LEN, SCAN_ROWS * SSM_DIM), BF16)
        out_shape += [yshape, yshape]
        out_specs += [pl.BlockSpec(tblk, fwd), pl.BlockSpec(tblk, bwd)]
        scratch.append(pltpu.VMEM((2, SSM_DIM // LANES, SCAN_ROWS * SCAN_TT, LANES), F32))
        scratch.append(pltpu.VMEM((2, SSM_DIM, 2 * n), BF16))
    out_shape.append(jax.ShapeDtypeStruct((2, SCAN_ROWS, 2 * n), F32))
    out_specs.append(full3((2, SCAN_ROWS, 2 * n)))
    return pl.pallas_call(
        functools.partial(_scan_kernel, emit_y=emit_y, chain=chain),
        grid=(nt,),
        in_specs=in_specs,
        out_specs=out_specs,
        out_shape=out_shape,
        scratch_shapes=scratch,
        compiler_params=_cparams(("arbitrary",)),
        name="scan_chain" if chain else ("scan_y" if emit_y else "scan_state"),
    )(*args)


def _shifted(scr_ref, off, rows, pos, seq_len, w):
    mid = scr_ref[off:off + rows, :]
    up = jnp.where(pos != 0, scr_ref[off - 1:off - 1 + rows, :], 0.0)
    dn = jnp.where(pos != seq_len - 1, scr_ref[off + 1:off + 1 + rows, :], 0.0)
    return w[0:1, :] * up + w[1:2, :] * mid + w[2:3, :] * dn


def _mix_kernel(x_ref, mod_ref, attn_ref, g3_ref, *rest, halo, tm, seq_len):
    rest = list(rest)
    if halo:
        gp_ref, gn_ref = rest[:2]
        rest = rest[2:]
    (su_ref, yf_ref, yb_ref, wc_ref, dsk_ref, wglu_ref, wo_ref, gn2_ref,
     x1_ref, h2_ref, z_scr, wglu_scr, wo_scr) = rest
    i = pl.program_id(0)

    @pl.when(i == 0)
    def _cast_weights():
        wglu_scr[...] = wglu_ref[...].astype(BF16)
        wo_scr[...] = wo_ref[...].astype(BF16)

    def rows_of(ref):
        pieces = [ref[:, r * SSM_DIM:(r + 1) * SSM_DIM].astype(F32) for r in range(tm // SCAN_LEN)]
        return pieces[0] if len(pieces) == 1 else jnp.concatenate(pieces, axis=0)

    m = mod_ref[...]
    g1, sh2, sc2 = m[:, 2 * D:3 * D], m[:, 3 * D:4 * D], m[:, 4 * D:5 * D]
    pos = (i * tm + lax.broadcasted_iota(jnp.int32, (tm, 1), 0)) % seq_len

    def gate_prod(ref):
        return ref[:, SC_DIM:2 * SC_DIM].astype(F32) * ref[:, 2 * SC_DIM:3 * SC_DIM].astype(F32)

    gb = g3_ref[:, 0:SC_DIM].astype(F32)
    z_scr[16:16 + tm, :] = gate_prod(g3_ref)
    if halo:
        z_scr[0:16, :] = gate_prod(gp_ref)
        z_scr[16 + tm:32 + tm, :] = gate_prod(gn_ref)
    else:
        z_scr[0:16, :] = jnp.zeros((16, SC_DIM), F32)
        z_scr[16 + tm:32 + tm, :] = jnp.zeros((16, SC_DIM), F32)
    conv = gb * _shifted(z_scr, 16, tm, pos, seq_len, wc_ref[...])

    y = dsk_ref[...] * rows_of(su_ref) + rows_of(yf_ref) + rows_of(yb_ref)
    zz = 0.5 * y * (1.0 + jnp.tanh(math.sqrt(2.0 / math.pi) * (y + 0.044715 * (y * y * y))))
    ssm = zz * _sigmoid(_dot(zz.astype(BF16), wglu_scr[...]))

    mix = (_dot(attn_ref[...], wo_scr[0:ATTN_DIM, :])
           + _dot(conv.astype(BF16), wo_scr[ATTN_DIM:ATTN_DIM + SC_DIM, :])
           + _dot(ssm.astype(BF16), wo_scr[ATTN_DIM + SC_DIM:, :]))
    x1 = x_ref[...] + g1 * mix
    x1_ref[...] = x1
    h2_ref[...] = _rms_mod(x1, gn2_ref[...], sc2, sh2).astype(BF16)


def _mix(x, mods, attn, g3, su, yf, yb, prm, l, lat, tokens_per_mod, seq_len, tm=512):
    t = x.shape[0]
    halo = seq_len > tm
    tiles_per_mod = tokens_per_mod // tm
    mod_map = (lambda i: (l, 1 + i // tiles_per_mod, 0, 0)) if lat else (lambda i: (l, 0, 0, 0))
    row = lambda i: (i, 0)
    in_specs = [pl.BlockSpec((tm, D), row), pl.BlockSpec((None, None, 1, 6 * D), mod_map),
                pl.BlockSpec((tm, 512), row), pl.BlockSpec((tm, 768), row)]
    args = [x, mods, attn, g3]
    if halo:
        r16 = tm // 16
        in_specs += [pl.BlockSpec((16, 768), lambda i: (jnp.maximum(i * r16 - 1, 0), 0)),
                     pl.BlockSpec((16, 768), lambda i: (jnp.minimum((i + 1) * r16, t // 16 - 1), 0))]
        args += [g3, g3]
    assert tm % SCAN_LEN == 0
    in_specs += [pl.BlockSpec((SCAN_LEN, tm // SCAN_LEN * SSM_DIM), lambda i: (0, i))] * 3
    args += [su, yf, yb]
    in_specs += [_layer_spec((3, SC_DIM), l), _layer_spec((1, SSM_DIM), l), _layer_spec((SSM_DIM, SSM_DIM), l),
                 _layer_spec((D, D), l), _layer_spec((1, D), l)]
    args += [prm["sc_conv"], prm["ssm_d"], prm["w_glu"], prm["w_out"], prm["norm_ffn"]]
    return pl.pallas_call(
        functools.partial(_mix_kernel, halo=halo, tm=tm, seq_len=seq_len),
        grid=(t // tm,),
        in_specs=in_specs,
        out_specs=[pl.BlockSpec((tm, D), row), pl.BlockSpec((tm, D), row)],
        out_shape=[jax.ShapeDtypeStruct((t, D), F32), jax.ShapeDtypeStruct((t, D), BF16)],
        scratch_shapes=[pltpu.VMEM((tm + 32, SC_DIM), F32), pltpu.VMEM((SSM_DIM, SSM_DIM), BF16),
                        pltpu.VMEM((D, D), BF16)],
        compiler_params=_cparams(("arbitrary",)),
        name="mix_lat" if lat else "mix_ctx",
    )(*args)


def _ffn_kernel(h_ref, *rest, halo, tm, seq_len, final):
    rest = list(rest)
    if halo:
        hp_ref, hn_ref = rest[:2]
        rest = rest[2:]
    x1_ref, mod_ref, wa_ref, wg_ref, ca_ref, cg_ref, wd_ref = rest[:7]
    rest = rest[7:]
    if final:
        gf_ref = rest[0]
        rest = rest[1:]
    o_ref, acc_scr = rest[:2]
    rest = rest[2:]
    nsub = len(FFN_SUB)
    ua_scrs, ug_scrs = rest[:nsub], rest[nsub:2 * nsub]
    i = pl.program_id(0)
    j = pl.program_id(1)

    if halo:
        hcat_scr = rest[2 * nsub]

        @pl.when(j == 0)
        def _stage():
            hcat_scr[0:16, :] = hp_ref[...]
            hcat_scr[16:16 + tm, :] = h_ref[...]
            hcat_scr[16 + tm:32 + tm, :] = hn_ref[...]

        hh = hcat_scr[...]
        pad, seg_len, nseg = 16, tm, 1
        tiles_per_seq = seq_len // tm
        keep_prev = (i % tiles_per_seq != 0).astype(F32)
        keep_next = (i % tiles_per_seq != tiles_per_seq - 1).astype(F32)
    else:
        hh = h_ref[...]
        pad, seg_len, nseg = 8, seq_len, tm // seq_len
    seg_rows = seg_len + 2 * pad
    cols = [sum(FFN_SUB[:c]) for c in range(nsub)]

    def up(c):
        width = FFN_SUB[c]
        for u_scr, w_ref in ((ua_scrs[c], wa_ref), (ug_scrs[c], wg_ref)):
            u = _dot(hh, w_ref[:, cols[c]:cols[c] + width])
            if halo:
                u_scr[...] = u
                u_scr[8:16, :] = u_scr[8:16, :] * keep_prev
                u_scr[16 + tm:24 + tm, :] = u_scr[16 + tm:24 + tm, :] * keep_next
            else:
                for sg in range(nseg):
                    r0 = sg * seg_rows
                    u_scr[r0:r0 + pad, :] = jnp.zeros((pad, width), F32)
                    u_scr[r0 + pad:r0 + pad + seg_len, :] = u[sg * seg_len:(sg + 1) * seg_len]
                    u_scr[r0 + pad + seg_len:r0 + seg_rows, :] = jnp.zeros((pad, width), F32)

    def conv(c, u_scr, cw_ref):
        w = cw_ref[:, cols[c]:cols[c] + FFN_SUB[c]]
        segs = []
        for sg in range(nseg):
            r0 = sg * seg_rows + pad
            segs.append(w[0:1, :] * u_scr[r0 - 1:r0 - 1 + seg_len, :]
                        + w[1:2, :] * u_scr[r0:r0 + seg_len, :]
                        + w[2:3, :] * u_scr[r0 + 1:r0 + 1 + seg_len, :])
        return segs[0] if nseg == 1 else jnp.concatenate(segs, axis=0)

    def down(c):
        a = conv(c, ua_scrs[c], ca_ref)
        g = conv(c, ug_scrs[c], cg_ref)
        act = (a * (g * _sigmoid(g))).astype(BF16)
        acc_scr[...] += _dot(act, wd_ref[cols[c]:cols[c] + FFN_SUB[c], :])

    @pl.when(j == 0)
    def _init():
        acc_scr[...] = jnp.zeros((tm, D), F32)

    up(0)
    for c in range(nsub):
        if c + 1 < nsub:
            up(c + 1)
        down(c)

    @pl.when(j == pl.num_programs(1) - 1)
    def _finish():
        g2 = mod_ref[...][:, 5 * D:6 * D]
        x2 = x1_ref[...] + g2 * acc_scr[...]
        if final:
            x2 = x2 * lax.rsqrt(jnp.mean(x2 * x2, axis=-1, keepdims=True) + RMS_EPS) * gf_ref[...]
        o_ref[...] = x2


def _ffn(h2, x1, mods, prm, l, lat, tokens_per_mod, seq_len, final_g, tm=512):
    t = h2.shape[0]
    tf = FFN_TF
    halo = seq_len > tm
    final = final_g is not None
    nj = D_FF // tf
    tiles_per_mod = tokens_per_mod // tm
    mod_map = (lambda i, j: (l, 1 + i // tiles_per_mod, 0, 0)) if lat else (lambda i, j: (l, 0, 0, 0))
    row = lambda i, j: (i, 0)
    in_specs = [pl.BlockSpec((tm, D), row)]
    args = [h2]
    if halo:
        r16 = tm // 16
        in_specs += [pl.BlockSpec((16, D), lambda i, j: (jnp.maximum(i * r16 - 1, 0), 0)),
                     pl.BlockSpec((16, D), lambda i, j: (jnp.minimum((i + 1) * r16, t // 16 - 1), 0))]
        args += [h2, h2]
    in_specs += [pl.BlockSpec((tm, D), row), pl.BlockSpec((None, None, 1, 6 * D), mod_map),
                 pl.BlockSpec((None, D, tf), lambda i, j: (l, 0, j)),
                 pl.BlockSpec((None, D, tf), lambda i, j: (l, 0, j + nj)),
                 pl.BlockSpec((None, 3, tf), lambda i, j: (l, 0, j)),
                 pl.BlockSpec((None, 3, tf), lambda i, j: (l, 0, j + nj)),
                 pl.BlockSpec((None, tf, D), lambda i, j: (l, j, 0))]
    args += [x1, mods, prm["w_up"], prm["w_up"], prm["ffn_conv"], prm["ffn_conv"], prm["w_down"]]
    if final:
        in_specs.append(pl.BlockSpec((1, D), lambda i, j: (0, 0)))
        args.append(final_g)
    urows = tm + 32 if halo else (tm // seq_len) * (seq_len + 16)
    scratch = [pltpu.VMEM((tm, D), F32)]
    scratch += [pltpu.VMEM((urows, w), F32) for w in FFN_SUB] * 2
    if halo:
        scratch.append(pltpu.VMEM((tm + 32, D), BF16))
    return pl.pallas_call(
        functools.partial(_ffn_kernel, halo=halo, tm=tm, seq_len=seq_len, final=final),
        grid=(t // tm, nj),
        in_specs=in_specs,
        out_specs=pl.BlockSpec((tm, D), row),
        out_shape=jax.ShapeDtypeStruct((t, D), F32),
        scratch_shapes=scratch,
        compiler_params=_cparams(("parallel", "arbitrary")),
        name="ffn_lat" if lat else "ffn_ctx",
    )(*args)


def _rope_tables(seq_len):
    rows = seq_len // GRID_W
    row = np.repeat(np.arange(rows, dtype=np.float32), GRID_W)
    col = np.tile(np.arange(GRID_W, dtype=np.float32), rows)
    freqs = (np.float32(ROPE_BASE) ** (-np.arange(ROPE_FREQS, dtype=np.float32) / np.float32(ROPE_FREQS)))
    freqs = freqs.astype(np.float32)
    ang = np.stack([row[:, None] * freqs, col[:, None] * freqs], axis=1).astype(np.float32)
    cos, sin = np.cos(ang).astype(np.float32), np.sin(ang).astype(np.float32)
    zero = np.zeros_like(sin)
    c64 = np.stack([cos, cos], axis=2).reshape(seq_len, HEAD_DIM)
    sneg64 = np.stack([-sin, zero], axis=2).reshape(seq_len, HEAD_DIM)
    spos64 = np.stack([zero, sin], axis=2).reshape(seq_len, HEAD_DIM)
    return tuple(jnp.asarray(np.tile(tb, (1, 2))) for tb in (c64, sneg64, spos64))


def _prep_params(w_in, w_out, norm_mix, norm_ffn, sc_conv, ssm_lam_re, ssm_lam_im, ssm_log_dt, ssm_b_re,
                 ssm_b_im, ssm_c_re, ssm_c_im, ssm_d, ssm_w_glu, ffn_w_up, ffn_conv, ffn_w_down):
    b_slab = lambda b: jnp.transpose(b, (0, 1, 4, 2, 3)).reshape(DEPTH, 2, SSM_CH, SSM_N)
    c_slab = lambda c: jnp.transpose(c, (0, 1, 3, 2, 4)).reshape(DEPTH, 2, SSM_CH, SSM_N)
    return {
        "w_in": w_in,
        "norm_mix": norm_mix.reshape(DEPTH, 1, D),
        "norm_ffn": norm_ffn.reshape(DEPTH, 1, D),
        "sc_conv": jnp.transpose(sc_conv, (0, 2, 1)),
        "ssm_d": ssm_d.reshape(DEPTH, 1, SSM_DIM),
        "w_glu": ssm_w_glu,
        "w_out": w_out,
        "w_up": ffn_w_up.astype(BF16),
        "ffn_conv": jnp.transpose(ffn_conv, (0, 2, 1)),
        "w_down": ffn_w_down.astype(BF16),
        "lam_re": ssm_lam_re.reshape(DEPTH, 2, 1, SSM_N),
        "lam_im": ssm_lam_im.reshape(DEPTH, 2, 1, SSM_N),
        "log_dt": jnp.repeat(ssm_log_dt, SSM_STATE, axis=-1).reshape(DEPTH, 2, 1, SSM_N),
        "b_re": b_slab(ssm_b_re),
        "b_im": b_slab(ssm_b_im),
        "c_re": c_slab(ssm_c_re),
        "c_im": c_slab(ssm_c_im),
    }


def kernel(x_prompt, x_sample, cache_k, cache_v, state_ssm_re, state_ssm_im, c, c_ctx, norm_mix, norm_ffn, norm_final, w_ada, b_ada, w_in, w_out, attn_sink, sc_conv, ssm_lam_re, ssm_lam_im, ssm_log_dt, ssm_b_re, ssm_b_im, ssm_c_re, ssm_c_im, ssm_d, ssm_w_glu, ffn_w_up, ffn_conv, ffn_w_down):
    batch, seq = x_prompt.shape[0], x_prompt.shape[1]
    dec_batch, dec_seq = x_sample.shape[0], x_sample.shape[1]
    assert batch == SCAN_ROWS and seq == SCAN_LEN
    assert dec_batch * (dec_seq // SCAN_LEN) == SCAN_ROWS and dec_batch == 2

    cs = jnp.concatenate([c_ctx[None, :], c, jnp.zeros((8 - 1 - dec_batch, D), F32)], axis=0)
    mods = _adaln(cs, w_ada, b_ada).reshape(DEPTH, 8, 1, 6 * D)
    rope_tabs = _rope_tables(dec_seq)
    gfin = norm_final.reshape(1, D)
    prm = _prep_params(w_in, w_out, norm_mix, norm_ffn, sc_conv, ssm_lam_re, ssm_lam_im, ssm_log_dt, ssm_b_re,
                       ssm_b_im, ssm_c_re, ssm_c_im, ssm_d, ssm_w_glu, ffn_w_up, ffn_conv, ffn_w_down)
    s0_all = jnp.transpose(jnp.concatenate([state_ssm_re.reshape(dec_batch, DEPTH, 2, SSM_N),
                                            state_ssm_im.reshape(dec_batch, DEPTH, 2, SSM_N)], axis=-1),
                           (1, 2, 0, 3))

    xp = x_prompt.reshape(batch * seq, D)
    xs = x_sample.reshape(dec_batch * dec_seq, D)
    kv_out, fin_out = [], []
    for l in range(DEPTH):
        last = gfin if l == DEPTH - 1 else None

        q, kd, vd, kv, g3, su = _inproj(xp, mods, prm, l, None, batch * seq, seq)
        attn = _attn_ctx(attn_sink, l, q, kd, vd, seq)
        yf, yb, fin = _scan(prm, l, su, True)
        x1, h2 = _mix(xp, mods, attn, g3, su, yf, yb, prm, l, False, batch * seq, seq)
        xp = _ffn(h2, x1, mods, prm, l, False, batch * seq, seq, last)
        kv_out.append(kv)
        fin_out.append(fin)

        q, kd, vd, g3, su = _inproj(xs, mods, prm, l, rope_tabs, dec_seq, dec_seq)
        attn = _attn_lat(attn_sink, l, q, kd, vd, cache_k, cache_v, dec_batch, dec_seq)
        (ez,) = _scan(prm, l, su, False)
        yf, yb, _ = _scan(prm, l, su, True, s0=s0_all[l], ez=ez)
        x1, h2 = _mix(xs, mods, attn, g3, su, yf, yb, prm, l, True, dec_seq, dec_seq)
        xs = _ffn(h2, x1, mods, prm, l, True, dec_seq, dec_seq, last)

    kv_all = jnp.stack(kv_out, axis=0).reshape(DEPTH, batch, seq, 2, KV_HEADS, HEAD_DIM)
    kv_all = jnp.transpose(kv_all, (3, 1, 0, 2, 4, 5))
    fin_all = jnp.stack(fin_out, axis=0).reshape(DEPTH, 2, batch, 2, SSM_GROUPS, SSM_STATE)
    fin_all = jnp.transpose(fin_all, (3, 2, 0, 1, 4, 5))
    return (xp.reshape(batch, seq, D), xs.reshape(dec_batch, dec_seq, D),
            kv_all[0], kv_all[1], fin_all[0], fin_all[1])
```

```python
import functools
import math

import jax
import jax.numpy as jnp
import numpy as np
from jax import lax
from jax.experimental import pallas as pl
from jax.experimental.pallas import tpu as pltpu

F32 = jnp.float32
BF16 = jnp.bfloat16

D = 1024
DEPTH = 2
GRID_W = 64
ATTN_DIM = 512
SC_DIM = 256
SSM_DIM = 256
HEAD_DIM = 64
N_HEADS = 8
KV_HEADS = 2
KV_DIM = 128
WINDOW = 128
Q_BLOCK = 128
ROPE_BASE = 10000.0
ROPE_FREQS = 16
SSM_CH = 16
SSM_GROUPS = 16
SSM_STATE = 64
SSM_N = SSM_GROUPS * SSM_STATE
IN_DIM = ATTN_DIM + 2 * KV_DIM + 3 * SC_DIM + SSM_DIM
D_FF = 2816
ATTN_SCALE = 1.0 / math.sqrt(HEAD_DIM)
RMS_EPS = 1e-6
NEG_BIG = -1e30

LANES = 128
SCAN_ROWS = 16
SCAN_LEN = 256
SCAN_TT = 64
SCAN_LW = 256
FFN_TF = 1408
FFN_SUB = (512, 512, 384)
assert D_FF % FFN_TF == 0 and sum(FFN_SUB) == FFN_TF
VMEM_LIMIT = 56 * 1024 * 1024


def _cparams(sem):
    return pltpu.CompilerParams(dimension_semantics=sem, vmem_limit_bytes=VMEM_LIMIT)


def _sigmoid(x):
    return 1.0 / (1.0 + jnp.exp(-x))


def _rms_mod(x, g, scale, shift):
    y = x * lax.rsqrt(jnp.mean(x * x, axis=-1, keepdims=True) + RMS_EPS) * g
    return y * (1.0 + scale) + shift


def _split_bf16(v):
    hi = v.astype(BF16)
    lo = (v - hi.astype(F32)).astype(BF16)
    return hi, lo


def _dot(a, b):
    return jnp.dot(a, b, preferred_element_type=F32)


def _dot_t(a, b):
    return lax.dot_general(a, b, (((1,), (1,)), ((), ())), preferred_element_type=F32)


def _layer_spec(shape, l):
    zeros = (0,) * len(shape)
    return pl.BlockSpec((None,) + tuple(shape), lambda *_: (l,) + zeros)


def _half_tiles(t):
    lo = lax.broadcasted_iota(jnp.int32, t.shape, 1) < HEAD_DIM
    tr = pltpu.roll(t, HEAD_DIM, 1)
    return [jnp.where(lo, t, 0.0).astype(BF16), jnp.where(lo, 0.0, tr).astype(BF16),
            jnp.where(lo, tr, 0.0).astype(BF16), jnp.where(lo, 0.0, t).astype(BF16)]


def _adaln_kernel(c_ref, w_ref, b_ref, o_ref):
    c = c_ref[...]
    s = c * _sigmoid(c)
    s_hi, s_lo = _split_bf16(s)
    w_hi, w_lo = _split_bf16(w_ref[...])
    o_ref[...] = _dot(s_hi, w_hi) + _dot(s_lo, w_hi) + _dot(s_hi, w_lo) + b_ref[...]


def _adaln(cs, w_ada, b_ada):
    tn = 2048
    return pl.pallas_call(
        _adaln_kernel,
        grid=(DEPTH, 6 * D // tn),
        in_specs=[
            pl.BlockSpec((8, D), lambda l, j: (0, 0)),
            pl.BlockSpec((None, D, tn), lambda l, j: (l, 0, j)),
            pl.BlockSpec((None, 1, tn), lambda l, j: (l, 0, j)),
        ],
        out_specs=pl.BlockSpec((None, 8, tn), lambda l, j: (l, 0, j)),
        out_shape=jax.ShapeDtypeStruct((DEPTH, 8, 6 * D), F32),
        compiler_params=_cparams(("parallel", "parallel")),
        name="adaln",
    )(cs, w_ada, b_ada.reshape(DEPTH, 1, 6 * D))


def _inproj_kernel(x_ref, mod_ref, g_ref, w_ref, *rest, rope):
    if rope:
        cos_ref, sneg_ref, spos_ref, q_ref, kd_ref, vd_ref, g3_ref, su_ref, wbf_scr = rest
    else:
        q_ref, kd_ref, vd_ref, kv_ref, g3_ref, su_ref, wbf_scr = rest

    @pl.when(pl.program_id(0) == 0)
    def _cast_weights():
        wbf_scr[...] = w_ref[...].astype(BF16)

    m = mod_ref[...]
    h = _rms_mod(x_ref[...], g_ref[...], m[:, D:2 * D], m[:, 0:D]).astype(BF16)
    acc = _dot(h, wbf_scr[...])
    k = acc[:, 512:640]
    v = acc[:, 640:768]
    if rope:
        c, sn, sp = cos_ref[...], sneg_ref[...], spos_ref[...]

        def rot(t):
            return t * c + pltpu.roll(t, LANES - ROPE_FREQS, 1) * sn + pltpu.roll(t, ROPE_FREQS, 1) * sp

        for p in range(4):
            q_ref[:, LANES * p:LANES * (p + 1)] = (rot(acc[:, LANES * p:LANES * (p + 1)]) * ATTN_SCALE).astype(BF16)
        k = rot(k)
    else:
        q_ref[...] = (acc[:, 0:512] * ATTN_SCALE).astype(BF16)
        kv_ref[...] = acc[:, 512:768]
    for ref, t in ((kd_ref, k), (vd_ref, v)):
        for p, tile in enumerate(_half_tiles(t)):
            ref[:, LANES * p:LANES * (p + 1)] = tile
    g3_ref[...] = acc[:, 768:1536].astype(BF16)
    for r in range(acc.shape[0] // SCAN_LEN):
        su_ref[:, r * SSM_DIM:(r + 1) * SSM_DIM] = acc[r * SCAN_LEN:(r + 1) * SCAN_LEN, 1536:1792]


def _inproj(x, mods, prm, l, rope_tabs, tokens_per_mod, seq_len, tm=512):
    t = x.shape[0]
    rope = rope_tabs is not None
    tiles_per_mod = tokens_per_mod // tm
    tiles_per_seq = seq_len // tm
    if rope:
        mod_map = lambda i: (l, 1 + i // tiles_per_mod, 0, 0)
    else:
        mod_map = lambda i: (l, 0, 0, 0)
    row = lambda i: (i, 0)
    in_specs = [
        pl.BlockSpec((tm, D), row),
        pl.BlockSpec((None, None, 1, 6 * D), mod_map),
        _layer_spec((1, D), l),
        _layer_spec((D, IN_DIM), l),
    ]
    args = [x, mods, prm["norm_mix"], prm["w_in"]]
    if rope:
        in_specs += [pl.BlockSpec((tm, LANES), lambda i: (i % tiles_per_seq, 0))] * 3
        args += list(rope_tabs)
    out_shape = [jax.ShapeDtypeStruct((t, 512), BF16)] * 3
    out_specs = [pl.BlockSpec((tm, 512), row)] * 3
    if not rope:
        out_shape.append(jax.ShapeDtypeStruct((t, 256), F32))
        out_specs.append(pl.BlockSpec((tm, 256), row))
    out_shape += [jax.ShapeDtypeStruct((t, 768), BF16),
                  jax.ShapeDtypeStruct((SCAN_LEN, t // SCAN_LEN * SSM_DIM), F32)]
    out_specs += [pl.BlockSpec((tm, 768), row),
                  pl.BlockSpec((SCAN_LEN, tm // SCAN_LEN * SSM_DIM), lambda i: (0, i))]
    return pl.pallas_call(
        functools.partial(_inproj_kernel, rope=rope),
        grid=(t // tm,),
        in_specs=in_specs,
        out_specs=out_specs,
        out_shape=out_shape,
        scratch_shapes=[pltpu.VMEM((D, IN_DIM), BF16)],
        compiler_params=_cparams(("arbitrary",)),
        name="inproj_lat" if rope else "inproj_ctx",
    )(*args)


def _attention(sink_ref, l, q_ref, o_ref, keys, vals, bias, nq):
    top = lax.broadcasted_iota(jnp.int32, (2 * nq, 1), 0) < nq
    scores = []
    for kvh in range(KV_HEADS):
        q2 = jnp.concatenate([q_ref[:, LANES * (2 * kvh + pp):LANES * (2 * kvh + pp + 1)] for pp in range(2)],
                             axis=0)
        scores.append([_dot_t(q2, keys[kvh][half]) for half in range(2)])
    for kvh in range(KV_HEADS):
        acc = None
        for half in range(2):
            sk = jnp.where(top, sink_ref[l, 4 * kvh + half], sink_ref[l, 4 * kvh + 2 + half])
            s = scores[kvh][half]
            if bias is not None:
                nb_ = bias.shape[1]
                s = jnp.concatenate([s[:, 0:nb_] + bias, s[:, nb_:]], axis=1)
            m = jnp.maximum(jnp.max(s, axis=-1, keepdims=True), sk)
            e = jnp.exp(s - m)
            den = jnp.sum(e, axis=-1, keepdims=True) + jnp.exp(sk - m)
            o = _dot(e.astype(BF16), vals[kvh][half]) / den
            acc = o if acc is None else acc + o
        o_ref[:, LANES * 2 * kvh:LANES * (2 * kvh + 1)] = acc[0:nq].astype(BF16)
        o_ref[:, LANES * (2 * kvh + 1):LANES * (2 * kvh + 2)] = acc[nq:2 * nq].astype(BF16)


def _kv_tiles(kvh):
    return [slice(LANES * (2 * kvh + h), LANES * (2 * kvh + h + 1)) for h in range(2)]


def _attn_ctx_kernel(sink_ref, q_ref, kd_ref, vd_ref, o_ref, *, l):
    keys = [[kd_ref[:, t] for t in _kv_tiles(kvh)] for kvh in range(KV_HEADS)]
    vals = [[vd_ref[:, t] for t in _kv_tiles(kvh)] for kvh in range(KV_HEADS)]
    _attention(sink_ref, l, q_ref, o_ref, keys, vals, None, q_ref.shape[0])


def _attn_ctx(sink, l, q, kd, vd, seq_len):
    t = q.shape[0]
    row = lambda b: (b, 0)
    return pl.pallas_call(
        functools.partial(_attn_ctx_kernel, l=l),
        grid=(t // seq_len,),
        in_specs=[
            pl.BlockSpec(memory_space=pltpu.SMEM),
            pl.BlockSpec((seq_len, 512), row),
            pl.BlockSpec((seq_len, 512), row),
            pl.BlockSpec((seq_len, 512), row),
        ],
        out_specs=pl.BlockSpec((seq_len, 512), row),
        out_shape=jax.ShapeDtypeStruct((t, 512), BF16),
        compiler_params=_cparams(("parallel",)),
        name="attn_ctx",
    )(sink, q, kd, vd)


def _attn_lat_kernel(sink_ref, q_ref, kp_ref, kc_ref, kn_ref, vp_ref, vc_ref, vn_ref, ck_ref, cv_ref, o_ref,
                     ckt_scr, cvt_scr, *, l):
    i = pl.program_id(1)
    nb = pl.num_programs(1)

    @pl.when(i == 0)
    def _context_tiles():
        for p, (kt, vt) in enumerate(zip(_half_tiles(ck_ref[...]), _half_tiles(cv_ref[...]))):
            ckt_scr[p] = kt
            cvt_scr[p] = vt

    r = lax.broadcasted_iota(jnp.int32, (Q_BLOCK, Q_BLOCK), 0)
    j = lax.broadcasted_iota(jnp.int32, (Q_BLOCK, Q_BLOCK), 1)
    bias = jnp.concatenate([
        jnp.where(jnp.logical_and(j >= r, i > 0), 0.0, NEG_BIG),
        jnp.where(jnp.logical_and(j <= r, i < nb - 1), 0.0, NEG_BIG)], axis=1)
    bias = jnp.concatenate([bias, bias], axis=0)
    def gather(refs, ctx_scr, kvh, h):
        t = _kv_tiles(kvh)[h]
        return jnp.concatenate([ref[:, t] for ref in refs] + [ctx_scr[2 * kvh + h]], axis=0)

    keys = [[gather((kp_ref, kn_ref, kc_ref), ckt_scr, kvh, h) for h in range(2)] for kvh in range(KV_HEADS)]
    vals = [[gather((vp_ref, vn_ref, vc_ref), cvt_scr, kvh, h) for h in range(2)] for kvh in range(KV_HEADS)]
    _attention(sink_ref, l, q_ref, o_ref, keys, vals, bias, Q_BLOCK)


def _attn_lat(sink, l, q, kd, vd, cache_k, cache_v, batch, seq_len):
    nb = seq_len // Q_BLOCK
    past = cache_k.shape[2]
    cur = lambda b, i: (b * nb + i, 0)
    prev = lambda b, i: (b * nb + jnp.maximum(i - 1, 0), 0)
    nxt = lambda b, i: (b * nb + jnp.minimum(i + 1, nb - 1), 0)
    kvspec = lambda f: pl.BlockSpec((Q_BLOCK, 512), f)
    cspec = pl.BlockSpec((None, None, past, KV_DIM), lambda b, i: (b, l, 0, 0))
    ck = cache_k.reshape(batch, DEPTH, past, KV_DIM)
    cv = cache_v.reshape(batch, DEPTH, past, KV_DIM)
    return pl.pallas_call(
        functools.partial(_attn_lat_kernel, l=l),
        grid=(batch, nb),
        in_specs=[
            pl.BlockSpec(memory_space=pltpu.SMEM),
            pl.BlockSpec((Q_BLOCK, 512), cur),
            kvspec(prev), kvspec(cur), kvspec(nxt),
            kvspec(prev), kvspec(cur), kvspec(nxt),
            cspec, cspec,
        ],
        out_specs=pl.BlockSpec((Q_BLOCK, 512), cur),
        out_shape=jax.ShapeDtypeStruct((batch * seq_len, 512), BF16),
        scratch_shapes=[pltpu.VMEM((2 * KV_HEADS, past, LANES), BF16)] * 2,
        compiler_params=_cparams(("parallel", "arbitrary")),
        name="attn_lat",
    )(sink, q, kd, kd, kd, vd, vd, vd, ck, cv)


def _scan_kernel(lre_ref, lim_ref, ldt_ref, bre_ref, bim_ref, cre_ref, cim_ref, *rest, emit_y, chain):
    rest = list(rest)
    if chain:
        s0_ref, ez_ref = rest[:2]
        rest = rest[2:]
    suf_ref, sub_ref = rest[:2]
    rest = rest[2:]
    if emit_y:
        y_refs = rest[:2]
        rest = rest[2:]
    fin_ref, a_scr, bb_scr, h_scr, bu_scr, u_scr = rest[:6]
    if emit_y:
        y_scr, ct_scr = rest[6:8]
    i = pl.program_id(0)
    tt = SCAN_TT
    n = SSM_N

    @pl.when(i == 0)
    def _prologue():
        row_g = lax.shift_right_logical(lax.broadcasted_iota(jnp.int32, (SSM_DIM, n), 0), 4)
        col_g = lax.shift_right_logical(lax.broadcasted_iota(jnp.int32, (SSM_DIM, n), 1), 6)
        own = row_g == col_g

        def blockdiag(ref, d):
            return jnp.where(own, jnp.concatenate([ref[d]] * SSM_GROUPS, axis=0), 0.0)

        for d in range(2):
            lr, li = lre_ref[d], lim_ref[d]
            dt = jnp.exp(ldt_ref[d])
            mag = jnp.exp(lr * dt)
            ar, ai = mag * jnp.cos(li * dt), mag * jnp.sin(li * dt)
            den = lr * lr + li * li
            fr = ((ar - 1.0) * lr + ai * li) / den
            fi = (ai * lr - (ar - 1.0) * li) / den
            bre, bim = blockdiag(bre_ref, d), blockdiag(bim_ref, d)
            bb_scr[d, :, 0:n] = (fr * bre - fi * bim).astype(BF16)
            bb_scr[d, :, n:2 * n] = (fr * bim + fi * bre).astype(BF16)
            if emit_y:
                ct_scr[d, :, 0:n] = blockdiag(cre_ref, d).astype(BF16)
                ct_scr[d, :, n:2 * n] = (-blockdiag(cim_ref, d)).astype(BF16)
            a_scr[d, 0] = jnp.broadcast_to(ar, (8, n))
            a_scr[d, 1] = jnp.broadcast_to(ai, (8, n))
            if chain:
                pr, pi_ = ar, ai
                for _ in range(8):
                    pr, pi_ = pr * pr - pi_ * pi_, 2.0 * pr * pi_
                chunks = SCAN_ROWS // 2
                for b in range(2):
                    hr = s0_ref[d, b:b + 1, 0:n]
                    hi = s0_ref[d, b:b + 1, n:2 * n]
                    order = range(chunks) if d == 0 else range(chunks - 1, -1, -1)
                    for c in order:
                        rw = b * chunks + c
                        h_scr[d, rw:rw + 1, 0:n] = hr
                        h_scr[d, rw:rw + 1, n:2 * n] = hi
                        er = ez_ref[d, rw:rw + 1, 0:n]
                        ei = ez_ref[d, rw:rw + 1, n:2 * n]
                        hr, hi = pr * hr - pi_ * hi + er, pr * hi + pi_ * hr + ei
            else:
                h_scr[d] = jnp.zeros((SCAN_ROWS, 2 * n), F32)

    ntile = n // LANES
    group = SCAN_LW // LANES

    def project_in(d):
        su_ref = suf_ref if d == 0 else sub_ref
        for r in range(SCAN_ROWS):
            for sl in range(SSM_DIM // LANES):
                col = r * SSM_DIM + sl * LANES
                u_scr[d, sl, pl.ds(r, tt, stride=SCAN_ROWS), :] = su_ref[:, col:col + LANES]
        u = jnp.concatenate([u_scr[d, sl] for sl in range(SSM_DIM // LANES)], axis=1).astype(BF16)
        bu_scr[d] = _dot(u, bb_scr[d])

    def recur(d):
        for c0 in range(0, ntile, group):
            lre = slice(c0 * LANES, (c0 + group) * LANES)
            lim = slice(n + c0 * LANES, n + (c0 + group) * LANES)
            ar, ai = a_scr[d, 0, :, lre], a_scr[d, 1, :, lre]
            halves = [slice(0, 8), slice(8, 16)]
            hr = [h_scr[d, hs, lre] for hs in halves]
            hi = [h_scr[d, hs, lim] for hs in halves]
            steps = range(tt) if d == 0 else range(tt - 1, -1, -1)
            for s in steps:
                for k in range(2):
                    rows = slice(s * SCAN_ROWS + 8 * k, s * SCAN_ROWS + 8 * k + 8)
                    br, bi = bu_scr[d, rows, lre], bu_scr[d, rows, lim]
                    hr[k], hi[k] = ar * hr[k] - ai * hi[k] + br, ar * hi[k] + ai * hr[k] + bi
                    bu_scr[d, rows, lre] = hr[k]
                    bu_scr[d, rows, lim] = hi[k]
            for k, hs in enumerate(halves):
                h_scr[d, hs, lre] = hr[k]
                h_scr[d, hs, lim] = hi[k]

    def project_out(d):
        y = _dot_t(bu_scr[d].astype(BF16), ct_scr[d])
        for sl in range(SSM_DIM // LANES):
            y_scr[d, sl] = y[:, sl * LANES:(sl + 1) * LANES]
        for r in range(SCAN_ROWS):
            for sl in range(SSM_DIM // LANES):
                col = r * SSM_DIM + sl * LANES
                y_refs[d][:, col:col + LANES] = y_scr[d, sl, pl.ds(r, tt, stride=SCAN_ROWS), :].astype(BF16)

    project_in(0)
    project_in(1)
    recur(0)
    if emit_y:
        project_out(0)
    recur(1)
    if emit_y:
        project_out(1)

    @pl.when(i == pl.num_programs(0) - 1)
    def _final():
        fin_ref[...] = h_scr[...]


def _scan(prm, l, su_tm, emit_y, s0=None, ez=None):
    chain = s0 is not None
    nt = SCAN_LEN // SCAN_TT
    n = SSM_N
    full3 = lambda shape: pl.BlockSpec(shape, lambda i: (0, 0, 0))
    in_specs = [_layer_spec((2, 1, n), l)] * 3 + [_layer_spec((2, SSM_CH, n), l)] * 4
    args = [prm["lam_re"], prm["lam_im"], prm["log_dt"], prm["b_re"], prm["b_im"], prm["c_re"], prm["c_im"]]
    if chain:
        in_specs += [full3((2, 2, 2 * n)), full3((2, SCAN_ROWS, 2 * n))]
        args += [s0, ez]
    tblk = (SCAN_TT, SCAN_ROWS * SSM_DIM)
    fwd = lambda i: (i, 0)
    bwd = lambda i: (nt - 1 - i, 0)
    in_specs += [pl.BlockSpec(tblk, fwd), pl.BlockSpec(tblk, bwd)]
    args += [su_tm, su_tm]
    out_shape, out_specs = [], []
    scratch = [
        pltpu.VMEM((2, 2, 8, n), F32),
        pltpu.VMEM((2, SSM_DIM, 2 * n), BF16),
        pltpu.VMEM((2, SCAN_ROWS, 2 * n), F32),
        pltpu.VMEM((2, SCAN_ROWS * SCAN_TT, 2 * n), F32),
        pltpu.VMEM((2, SSM_DIM // LANES, SCAN_ROWS * SCAN_TT, LANES), F32),
    ]
    if emit_y:
        yshape = jax.ShapeDtypeStruct((SCAN_LEN, SCAN_ROWS * SSM_DIM), BF16)
        out_shape += [yshape, yshape]
        out_specs += [pl.BlockSpec(tblk, fwd), pl.BlockSpec(tblk, bwd)]
        scratch.append(pltpu.VMEM((2, SSM_DIM // LANES, SCAN_ROWS * SCAN_TT, LANES), F32))
        scratch.append(pltpu.VMEM((2, SSM_DIM, 2 * n), BF16))
    out_shape.append(jax.ShapeDtypeStruct((2, SCAN_ROWS, 2 * n), F32))
    out_specs.append(full3((2, SCAN_ROWS, 2 * n)))
    return pl.pallas_call(
        functools.partial(_scan_kernel, emit_y=emit_y, chain=chain),
        grid=(nt,),
        in_specs=in_specs,
        out_specs=out_specs,
        out_shape=out_shape,
        scratch_shapes=scratch,
        compiler_params=_cparams(("arbitrary",)),
        name="scan_chain" if chain else ("scan_y" if emit_y else "scan_state"),
    )(*args)


def _shifted(scr_ref, off, rows, pos, seq_len, w):
    mid = scr_ref[off:off + rows, :]
    up = jnp.where(pos != 0, scr_ref[off - 1:off - 1 + rows, :], 0.0)
    dn = jnp.where(pos != seq_len - 1, scr_ref[off + 1:off + 1 + rows, :], 0.0)
    return w[0:1, :] * up + w[1:2, :] * mid + w[2:3, :] * dn


def _mix_kernel(x_ref, mod_ref, attn_ref, g3_ref, *rest, halo, tm, seq_len):
    rest = list(rest)
    if halo:
        gp_ref, gn_ref = rest[:2]
        rest = rest[2:]
    (su_ref, yf_ref, yb_ref, wc_ref, dsk_ref, wglu_ref, wo_ref, gn2_ref,
     x1_ref, h2_ref, z_scr, wglu_scr, wo_scr) = rest
    i = pl.program_id(0)

    @pl.when(i == 0)
    def _cast_weights():
        wglu_scr[...] = wglu_ref[...].astype(BF16)
        wo_scr[...] = wo_ref[...].astype(BF16)

    def rows_of(ref):
        pieces = [ref[:, r * SSM_DIM:(r + 1) * SSM_DIM].astype(F32) for r in range(tm // SCAN_LEN)]
        return pieces[0] if len(pieces) == 1 else jnp.concatenate(pieces, axis=0)

    m = mod_ref[...]
    g1, sh2, sc2 = m[:, 2 * D:3 * D], m[:, 3 * D:4 * D], m[:, 4 * D:5 * D]
    pos = (i * tm + lax.broadcasted_iota(jnp.int32, (tm, 1), 0)) % seq_len

    def gate_prod(ref):
        return ref[:, SC_DIM:2 * SC_DIM].astype(F32) * ref[:, 2 * SC_DIM:3 * SC_DIM].astype(F32)

    gb = g3_ref[:, 0:SC_DIM].astype(F32)
    z_scr[16:16 + tm, :] = gate_prod(g3_ref)
    if halo:
        z_scr[0:16, :] = gate_prod(gp_ref)
        z_scr[16 + tm:32 + tm, :] = gate_prod(gn_ref)
    else:
        z_scr[0:16, :] = jnp.zeros((16, SC_DIM), F32)
        z_scr[16 + tm:32 + tm, :] = jnp.zeros((16, SC_DIM), F32)
    conv = gb * _shifted(z_scr, 16, tm, pos, seq_len, wc_ref[...])

    y = dsk_ref[...] * rows_of(su_ref) + rows_of(yf_ref) + rows_of(yb_ref)
    zz = 0.5 * y * (1.0 + jnp.tanh(math.sqrt(2.0 / math.pi) * (y + 0.044715 * (y * y * y))))
    ssm = zz * _sigmoid(_dot(zz.astype(BF16), wglu_scr[...]))

    mix = (_dot(attn_ref[...], wo_scr[0:ATTN_DIM, :])
           + _dot(conv.astype(BF16), wo_scr[ATTN_DIM:ATTN_DIM + SC_DIM, :])
           + _dot(ssm.astype(BF16), wo_scr[ATTN_DIM + SC_DIM:, :]))
    x1 = x_ref[...] + g1 * mix
    x1_ref[...] = x1
    h2_ref[...] = _rms_mod(x1, gn2_ref[...], sc2, sh2).astype(BF16)


def _mix(x, mods, attn, g3, su, yf, yb, prm, l, lat, tokens_per_mod, seq_len, tm=512):
    t = x.shape[0]
    halo = seq_len > tm
    tiles_per_mod = tokens_per_mod // tm
    mod_map = (lambda i: (l, 1 + i // tiles_per_mod, 0, 0)) if lat else (lambda i: (l, 0, 0, 0))
    row = lambda i: (i, 0)
    in_specs = [pl.BlockSpec((tm, D), row), pl.BlockSpec((None, None, 1, 6 * D), mod_map),
                pl.BlockSpec((tm, 512), row), pl.BlockSpec((tm, 768), row)]
    args = [x, mods, attn, g3]
    if halo:
        r16 = tm // 16
        in_specs += [pl.BlockSpec((16, 768), lambda i: (jnp.maximum(i * r16 - 1, 0), 0)),
                     pl.BlockSpec((16, 768), lambda i: (jnp.minimum((i + 1) * r16, t // 16 - 1), 0))]
        args += [g3, g3]
    assert tm % SCAN_LEN == 0
    in_specs += [pl.BlockSpec((SCAN_LEN, tm // SCAN_LEN * SSM_DIM), lambda i: (0, i))] * 3
    args += [su, yf, yb]
    in_specs += [_layer_spec((3, SC_DIM), l), _layer_spec((1, SSM_DIM), l), _layer_spec((SSM_DIM, SSM_DIM), l),
                 _layer_spec((D, D), l), _layer_spec((1, D), l)]
    args += [prm["sc_conv"], prm["ssm_d"], prm["w_glu"], prm["w_out"], prm["norm_ffn"]]
    return pl.pallas_call(
        functools.partial(_mix_kernel, halo=halo, tm=tm, seq_len=seq_len),
        grid=(t // tm,),
        in_specs=in_specs,
        out_specs=[pl.BlockSpec((tm, D), row), pl.BlockSpec((tm, D), row)],
        out_shape=[jax.ShapeDtypeStruct((t, D), F32), jax.ShapeDtypeStruct((t, D), BF16)],
        scratch_shapes=[pltpu.VMEM((tm + 32, SC_DIM), F32), pltpu.VMEM((SSM_DIM, SSM_DIM), BF16),
                        pltpu.VMEM((D, D), BF16)],
        compiler_params=_cparams(("arbitrary",)),
        name="mix_lat" if lat else "mix_ctx",
    )(*args)


def _ffn_kernel(h_ref, *rest, halo, tm, seq_len, final):
    rest = list(rest)
    if halo:
        hp_ref, hn_ref = rest[:2]
        rest = rest[2:]
    x1_ref, mod_ref, wa_ref, wg_ref, ca_ref, cg_ref, wd_ref = rest[:7]
    rest = rest[7:]
    if final:
        gf_ref = rest[0]
        rest = rest[1:]
    o_ref, acc_scr = rest[:2]
    rest = rest[2:]
    nsub = len(FFN_SUB)
    ua_scrs, ug_scrs = rest[:nsub], rest[nsub:2 * nsub]
    i = pl.program_id(0)
    j = pl.program_id(1)

    if halo:
        hcat_scr = rest[2 * nsub]

        @pl.when(j == 0)
        def _stage():
            hcat_scr[0:16, :] = hp_ref[...]
            hcat_scr[16:16 + tm, :] = h_ref[...]
            hcat_scr[16 + tm:32 + tm, :] = hn_ref[...]

        hh = hcat_scr[...]
        pad, seg_len, nseg = 16, tm, 1
        tiles_per_seq = seq_len // tm
        keep_prev = (i % tiles_per_seq != 0).astype(F32)
        keep_next = (i % tiles_per_seq != tiles_per_seq - 1).astype(F32)
    else:
        hh = h_ref[...]
        pad, seg_len, nseg = 8, seq_len, tm // seq_len
    seg_rows = seg_len + 2 * pad
    cols = [sum(FFN_SUB[:c]) for c in range(nsub)]

    def up(c):
        width = FFN_SUB[c]
        for u_scr, w_ref in ((ua_scrs[c], wa_ref), (ug_scrs[c], wg_ref)):
            u = _dot(hh, w_ref[:, cols[c]:cols[c] + width])
            if halo:
                u_scr[...] = u
                u_scr[8:16, :] = u_scr[8:16, :] * keep_prev
                u_scr[16 + tm:24 + tm, :] = u_scr[16 + tm:24 + tm, :] * keep_next
            else:
                for sg in range(nseg):
                    r0 = sg * seg_rows
                    u_scr[r0:r0 + pad, :] = jnp.zeros((pad, width), F32)
                    u_scr[r0 + pad:r0 + pad + seg_len, :] = u[sg * seg_len:(sg + 1) * seg_len]
                    u_scr[r0 + pad + seg_len:r0 + seg_rows, :] = jnp.zeros((pad, width), F32)

    def conv(c, u_scr, cw_ref):
        w = cw_ref[:, cols[c]:cols[c] + FFN_SUB[c]]
        segs = []
        for sg in range(nseg):
            r0 = sg * seg_rows + pad
            segs.append(w[0:1, :] * u_scr[r0 - 1:r0 - 1 + seg_len, :]
                        + w[1:2, :] * u_scr[r0:r0 + seg_len, :]
                        + w[2:3, :] * u_scr[r0 + 1:r0 + 1 + seg_len, :])
        return segs[0] if nseg == 1 else jnp.concatenate(segs, axis=0)

    def down(c):
        a = conv(c, ua_scrs[c], ca_ref)
        g = conv(c, ug_scrs[c], cg_ref)
        act = (a * (g * _sigmoid(g))).astype(BF16)
        acc_scr[...] += _dot(act, wd_ref[cols[c]:cols[c] + FFN_SUB[c], :])

    @pl.when(j == 0)
    def _init():
        acc_scr[...] = jnp.zeros((tm, D), F32)

    up(0)
    for c in range(nsub):
        if c + 1 < nsub:
            up(c + 1)
        down(c)

    @pl.when(j == pl.num_programs(1) - 1)
    def _finish():
        g2 = mod_ref[...][:, 5 * D:6 * D]
        x2 = x1_ref[...] + g2 * acc_scr[...]
        if final:
            x2 = x2 * lax.rsqrt(jnp.mean(x2 * x2, axis=-1, keepdims=True) + RMS_EPS) * gf_ref[...]
        o_ref[...] = x2


def _ffn(h2, x1, mods, prm, l, lat, tokens_per_mod, seq_len, final_g, tm=512):
    t = h2.shape[0]
    tf = FFN_TF
    halo = seq_len > tm
    final = final_g is not None
    nj = D_FF // tf
    tiles_per_mod = tokens_per_mod // tm
    mod_map = (lambda i, j: (l, 1 + i // tiles_per_mod, 0, 0)) if lat else (lambda i, j: (l, 0, 0, 0))
    row = lambda i, j: (i, 0)
    in_specs = [pl.BlockSpec((tm, D), row)]
    args = [h2]
    if halo:
        r16 = tm // 16
        in_specs += [pl.BlockSpec((16, D), lambda i, j: (jnp.maximum(i * r16 - 1, 0), 0)),
                     pl.BlockSpec((16, D), lambda i, j: (jnp.minimum((i + 1) * r16, t // 16 - 1), 0))]
        args += [h2, h2]
    in_specs += [pl.BlockSpec((tm, D), row), pl.BlockSpec((None, None, 1, 6 * D), mod_map),
                 pl.BlockSpec((None, D, tf), lambda i, j: (l, 0, j)),
                 pl.BlockSpec((None, D, tf), lambda i, j: (l, 0, j + nj)),
                 pl.BlockSpec((None, 3, tf), lambda i, j: (l, 0, j)),
                 pl.BlockSpec((None, 3, tf), lambda i, j: (l, 0, j + nj)),
                 pl.BlockSpec((None, tf, D), lambda i, j: (l, j, 0))]
    args += [x1, mods, prm["w_up"], prm["w_up"], prm["ffn_conv"], prm["ffn_conv"], prm["w_down"]]
    if final:
        in_specs.append(pl.BlockSpec((1, D), lambda i, j: (0, 0)))
        args.append(final_g)
    urows = tm + 32 if halo else (tm // seq_len) * (seq_len + 16)
    scratch = [pltpu.VMEM((tm, D), F32)]
    scratch += [pltpu.VMEM((urows, w), F32) for w in FFN_SUB] * 2
    if halo:
        scratch.append(pltpu.VMEM((tm + 32, D), BF16))
    return pl.pallas_call(
        functools.partial(_ffn_kernel, halo=halo, tm=tm, seq_len=seq_len, final=final),
        grid=(t // tm, nj),
        in_specs=in_specs,
        out_specs=pl.BlockSpec((tm, D), row),
        out_shape=jax.ShapeDtypeStruct((t, D), F32),
        scratch_shapes=scratch,
        compiler_params=_cparams(("parallel", "arbitrary")),
        name="ffn_lat" if lat else "ffn_ctx",
    )(*args)


def _rope_tables(seq_len):
    rows = seq_len // GRID_W
    row = np.repeat(np.arange(rows, dtype=np.float32), GRID_W)
    col = np.tile(np.arange(GRID_W, dtype=np.float32), rows)
    freqs = (np.float32(ROPE_BASE) ** (-np.arange(ROPE_FREQS, dtype=np.float32) / np.float32(ROPE_FREQS)))
    freqs = freqs.astype(np.float32)
    ang = np.stack([row[:, None] * freqs, col[:, None] * freqs], axis=1).astype(np.float32)
    cos, sin = np.cos(ang).astype(np.float32), np.sin(ang).astype(np.float32)
    zero = np.zeros_like(sin)
    c64 = np.stack([cos, cos], axis=2).reshape(seq_len, HEAD_DIM)
    sneg64 = np.stack([-sin, zero], axis=2).reshape(seq_len, HEAD_DIM)
    spos64 = np.stack([zero, sin], axis=2).reshape(seq_len, HEAD_DIM)
    return tuple(jnp.asarray(np.tile(tb, (1, 2))) for tb in (c64, sneg64, spos64))


def _prep_params(w_in, w_out, norm_mix, norm_ffn, sc_conv, ssm_lam_re, ssm_lam_im, ssm_log_dt, ssm_b_re,
                 ssm_b_im, ssm_c_re, ssm_c_im, ssm_d, ssm_w_glu, ffn_w_up, ffn_conv, ffn_w_down):
    b_slab = lambda b: jnp.transpose(b, (0, 1, 4, 2, 3)).reshape(DEPTH, 2, SSM_CH, SSM_N)
    c_slab = lambda c: jnp.transpose(c, (0, 1, 3, 2, 4)).reshape(DEPTH, 2, SSM_CH, SSM_N)
    return {
        "w_in": w_in,
        "norm_mix": norm_mix.reshape(DEPTH, 1, D),
        "norm_ffn": norm_ffn.reshape(DEPTH, 1, D),
        "sc_conv": jnp.transpose(sc_conv, (0, 2, 1)),
        "ssm_d": ssm_d.reshape(DEPTH, 1, SSM_DIM),
        "w_glu": ssm_w_glu,
        "w_out": w_out,
        "w_up": ffn_w_up.astype(BF16),
        "ffn_conv": jnp.transpose(ffn_conv, (0, 2, 1)),
        "w_down": ffn_w_down.astype(BF16),
        "lam_re": ssm_lam_re.reshape(DEPTH, 2, 1, SSM_N),
        "lam_im": ssm_lam_im.reshape(DEPTH, 2, 1, SSM_N),
        "log_dt": jnp.repeat(ssm_log_dt, SSM_STATE, axis=-1).reshape(DEPTH, 2, 1, SSM_N),
        "b_re": b_slab(ssm_b_re),
        "b_im": b_slab(ssm_b_im),
        "c_re": c_slab(ssm_c_re),
        "c_im": c_slab(ssm_c_im),
    }


def kernel(x_prompt, x_sample, cache_k, cache_v, state_ssm_re, state_ssm_im, c, c_ctx, norm_mix, norm_ffn, norm_final, w_ada, b_ada, w_in, w_out, attn_sink, sc_conv, ssm_lam_re, ssm_lam_im, ssm_log_dt, ssm_b_re, ssm_b_im, ssm_c_re, ssm_c_im, ssm_d, ssm_w_glu, ffn_w_up, ffn_conv, ffn_w_down):
    batch, seq = x_prompt.shape[0], x_prompt.shape[1]
    dec_batch, dec_seq = x_sample.shape[0], x_sample.shape[1]
    assert batch == SCAN_ROWS and seq == SCAN_LEN
    assert dec_batch * (dec_seq // SCAN_LEN) == SCAN_ROWS and dec_batch == 2

    cs = jnp.concatenate([c_ctx[None, :], c, jnp.zeros((8 - 1 - dec_batch, D), F32)], axis=0)
    mods = _adaln(cs, w_ada, b_ada).reshape(DEPTH, 8, 1, 6 * D)
    rope_tabs = _rope_tables(dec_seq)
    gfin = norm_final.reshape(1, D)
    prm = _prep_params(w_in, w_out, norm_mix, norm_ffn, sc_conv, ssm_lam_re, ssm_lam_im, ssm_log_dt, ssm_b_re,
                       ssm_b_im, ssm_c_re, ssm_c_im, ssm_d, ssm_w_glu, ffn_w_up, ffn_conv, ffn_w_down)
    s0_all = jnp.transpose(jnp.concatenate([state_ssm_re.reshape(dec_batch, DEPTH, 2, SSM_N),
                                            state_ssm_im.reshape(dec_batch, DEPTH, 2, SSM_N)], axis=-1),
                           (1, 2, 0, 3))

    xp = x_prompt.reshape(batch * seq, D)
    xs = x_sample.reshape(dec_batch * dec_seq, D)
    kv_out, fin_out = [], []
    for l in range(DEPTH):
        last = gfin if l == DEPTH - 1 else None

        q, kd, vd, kv, g3, su = _inproj(xp, mods, prm, l, None, batch * seq, seq)
        attn = _attn_ctx(attn_sink, l, q, kd, vd, seq)
        yf, yb, fin = _scan(prm, l, su, True)
        x1, h2 = _mix(xp, mods, attn, g3, su, yf, yb, prm, l, False, batch * seq, seq)
        xp = _ffn(h2, x1, mods, prm, l, False, batch * seq, seq, last)
        kv_out.append(kv)
        fin_out.append(fin)

        q, kd, vd, g3, su = _inproj(xs, mods, prm, l, rope_tabs, dec_seq, dec_seq)
        attn = _attn_lat(attn_sink, l, q, kd, vd, cache_k, cache_v, dec_batch, dec_seq)
        (ez,) = _scan(prm, l, su, False)
        yf, yb, _ = _scan(prm, l, su, True, s0=s0_all[l], ez=ez)
        x1, h2 = _mix(xs, mods, attn, g3, su, yf, yb, prm, l, True, dec_seq, dec_seq)
        xs = _ffn(h2, x1, mods, prm, l, True, dec_seq, dec_seq, last)

    kv_all = jnp.stack(kv_out, axis=0).reshape(DEPTH, batch, seq, 2, KV_HEADS, HEAD_DIM)
    kv_all = jnp.transpose(kv_all, (3, 1, 0, 2, 4, 5))
    fin_all = jnp.stack(fin_out, axis=0).reshape(DEPTH, 2, batch, 2, SSM_GROUPS, SSM_STATE)
    fin_all = jnp.transpose(fin_all, (3, 2, 0, 1, 4, 5))
    return (xp.reshape(batch, seq, D), xs.reshape(dec_batch, dec_seq, D),
            kv_all[0], kv_all[1], fin_all[0], fin_all[1])
```

```python
import functools
import math

import jax
import jax.numpy as jnp
import numpy as np
from jax import lax
from jax.experimental import pallas as pl
from jax.experimental.pallas import tpu as pltpu

F32 = jnp.float32
BF16 = jnp.bfloat16

D = 1024
DEPTH = 2
GRID_W = 64
ATTN_DIM = 512
SC_DIM = 256
SSM_DIM = 256
HEAD_DIM = 64
N_HEADS = 8
KV_HEADS = 2
KV_DIM = 128
WINDOW = 128
Q_BLOCK = 128
ROPE_BASE = 10000.0
ROPE_FREQS = 16
SSM_CH = 16
SSM_GROUPS = 16
SSM_STATE = 64
SSM_N = SSM_GROUPS * SSM_STATE
IN_DIM = ATTN_DIM + 2 * KV_DIM + 3 * SC_DIM + SSM_DIM
D_FF = 2816
LOG2E = math.log2(math.e)
Q_SCALE = LOG2E / math.sqrt(HEAD_DIM)
RMS_EPS = 1e-6
NEG_BIG = -1e30

LANES = 128
SCAN_ROWS = 16
SCAN_LEN = 256
SCAN_TT = 64
SCAN_LW = 256
SCAN_MM_ROWS = 256
FFN_TF = 1408
FFN_SUB = (512, 512, 384)
assert D_FF % FFN_TF == 0 and sum(FFN_SUB) == FFN_TF
VMEM_LIMIT = 56 * 1024 * 1024


def _cparams(sem):
    return pltpu.CompilerParams(dimension_semantics=sem, vmem_limit_bytes=VMEM_LIMIT)


def _sigmoid(x):
    return 1.0 / (1.0 + jnp.exp(-x))


def _rms_mod(x, g, scale, shift):
    y = x * lax.rsqrt(jnp.mean(x * x, axis=-1, keepdims=True) + RMS_EPS) * g
    return y * (1.0 + scale) + shift


def _split_bf16(v):
    hi = v.astype(BF16)
    lo = (v - hi.astype(F32)).astype(BF16)
    return hi, lo


def _dot(a, b):
    return jnp.dot(a, b, preferred_element_type=F32)


def _dot_t(a, b):
    return lax.dot_general(a, b, (((1,), (1,)), ((), ())), preferred_element_type=F32)


def _layer_spec(shape, l):
    zeros = (0,) * len(shape)
    return pl.BlockSpec((None,) + tuple(shape), lambda *_: (l,) + zeros)


def _half_tiles(t):
    lo = lax.broadcasted_iota(jnp.int32, t.shape, 1) < HEAD_DIM
    tr = pltpu.roll(t, HEAD_DIM, 1)
    return [jnp.where(lo, t, 0.0).astype(BF16), jnp.where(lo, 0.0, tr).astype(BF16),
            jnp.where(lo, tr, 0.0).astype(BF16), jnp.where(lo, 0.0, t).astype(BF16)]


def _adaln_kernel(c_ref, w_ref, b_ref, o_ref):
    c = c_ref[...]
    s = c * _sigmoid(c)
    s_hi, s_lo = _split_bf16(s)
    w_hi, w_lo = _split_bf16(w_ref[...])
    o_ref[...] = _dot(s_hi, w_hi) + _dot(s_lo, w_hi) + _dot(s_hi, w_lo) + b_ref[...]


def _adaln(cs, w_ada, b_ada):
    tn = 2048
    return pl.pallas_call(
        _adaln_kernel,
        grid=(DEPTH, 6 * D // tn),
        in_specs=[
            pl.BlockSpec((8, D), lambda l, j: (0, 0)),
            pl.BlockSpec((None, D, tn), lambda l, j: (l, 0, j)),
            pl.BlockSpec((None, 1, tn), lambda l, j: (l, 0, j)),
        ],
        out_specs=pl.BlockSpec((None, 8, tn), lambda l, j: (l, 0, j)),
        out_shape=jax.ShapeDtypeStruct((DEPTH, 8, 6 * D), F32),
        compiler_params=_cparams(("parallel", "parallel")),
        name="adaln",
    )(cs, w_ada, b_ada.reshape(DEPTH, 1, 6 * D))


def _inproj_kernel(x_ref, mod_ref, g_ref, w_ref, *rest, rope):
    if rope:
        cos_ref, sneg_ref, spos_ref, q_ref, kd_ref, vd_ref, g3_ref, su_ref, wbf_scr = rest
    else:
        q_ref, kd_ref, vd_ref, kv_ref, g3_ref, su_ref, wbf_scr = rest

    @pl.when(pl.program_id(0) == 0)
    def _cast_weights():
        wbf_scr[...] = w_ref[...].astype(BF16)

    m = mod_ref[...]
    h = _rms_mod(x_ref[...], g_ref[...], m[:, D:2 * D], m[:, 0:D]).astype(BF16)
    acc = _dot(h, wbf_scr[...])
    k = acc[:, 512:640]
    v = acc[:, 640:768]
    if rope:
        c, sn, sp = cos_ref[...], sneg_ref[...], spos_ref[...]

        def rot(t):
            return t * c + pltpu.roll(t, LANES - ROPE_FREQS, 1) * sn + pltpu.roll(t, ROPE_FREQS, 1) * sp

        for p in range(4):
            q_ref[:, LANES * p:LANES * (p + 1)] = (rot(acc[:, LANES * p:LANES * (p + 1)]) * Q_SCALE).astype(BF16)
        k = rot(k)
    else:
        q_ref[...] = (acc[:, 0:512] * Q_SCALE).astype(BF16)
        kv_ref[...] = acc[:, 512:768]
    for ref, t in ((kd_ref, k), (vd_ref, v)):
        for p, tile in enumerate(_half_tiles(t)):
            ref[:, LANES * p:LANES * (p + 1)] = tile
    g3_ref[...] = acc[:, 768:1536].astype(BF16)
    for r in range(acc.shape[0] // SCAN_LEN):
        su_ref[:, r * SSM_DIM:(r + 1) * SSM_DIM] = acc[r * SCAN_LEN:(r + 1) * SCAN_LEN, 1536:1792]


def _inproj(x, mods, prm, l, rope_tabs, tokens_per_mod, seq_len, tm=512):
    t = x.shape[0]
    rope = rope_tabs is not None
    tiles_per_mod = tokens_per_mod // tm
    tiles_per_seq = seq_len // tm
    if rope:
        mod_map = lambda i: (l, 1 + i // tiles_per_mod, 0, 0)
    else:
        mod_map = lambda i: (l, 0, 0, 0)
    row = lambda i: (i, 0)
    in_specs = [
        pl.BlockSpec((tm, D), row),
        pl.BlockSpec((None, None, 1, 6 * D), mod_map),
        _layer_spec((1, D), l),
        _layer_spec((D, IN_DIM), l),
    ]
    args = [x, mods, prm["norm_mix"], prm["w_in"]]
    if rope:
        in_specs += [pl.BlockSpec((tm, LANES), lambda i: (i % tiles_per_seq, 0))] * 3
        args += list(rope_tabs)
    out_shape = [jax.ShapeDtypeStruct((t, 512), BF16)] * 3
    out_specs = [pl.BlockSpec((tm, 512), row)] * 3
    if not rope:
        out_shape.append(jax.ShapeDtypeStruct((t, 256), F32))
        out_specs.append(pl.BlockSpec((tm, 256), row))
    out_shape += [jax.ShapeDtypeStruct((t, 768), BF16),
                  jax.ShapeDtypeStruct((SCAN_LEN, t // SCAN_LEN * SSM_DIM), F32)]
    out_specs += [pl.BlockSpec((tm, 768), row),
                  pl.BlockSpec((SCAN_LEN, tm // SCAN_LEN * SSM_DIM), lambda i: (0, i))]
    return pl.pallas_call(
        functools.partial(_inproj_kernel, rope=rope),
        grid=(t // tm,),
        in_specs=in_specs,
        out_specs=out_specs,
        out_shape=out_shape,
        scratch_shapes=[pltpu.VMEM((D, IN_DIM), BF16)],
        compiler_params=_cparams(("arbitrary",)),
        name="inproj_lat" if rope else "inproj_ctx",
    )(*args)


def _attention(sink_ref, l, q_ref, o_ref, keys, vals, bias, nq):
    top = lax.broadcasted_iota(jnp.int32, (2 * nq, 1), 0) < nq
    scores = []
    for kvh in range(KV_HEADS):
        q2 = jnp.concatenate([q_ref[:, LANES * (2 * kvh + pp):LANES * (2 * kvh + pp + 1)] for pp in range(2)],
                             axis=0)
        scores.append([_dot_t(q2, keys[kvh][half]) for half in range(2)])
    for kvh in range(KV_HEADS):
        acc = None
        for half in range(2):
            sk = jnp.where(top, sink_ref[l, 4 * kvh + half], sink_ref[l, 4 * kvh + 2 + half]) * LOG2E
            s = scores[kvh][half]
            if bias is not None:
                nb_ = bias.shape[1]
                s = jnp.concatenate([s[:, 0:nb_] + bias, s[:, nb_:]], axis=1)
            m = jnp.maximum(jnp.max(s, axis=-1, keepdims=True), sk)
            e = jnp.exp2(s - m)
            den = jnp.sum(e, axis=-1, keepdims=True) + jnp.exp2(sk - m)
            o = _dot(e.astype(BF16), vals[kvh][half]) / den
            acc = o if acc is None else acc + o
        o_ref[:, LANES * 2 * kvh:LANES * (2 * kvh + 1)] = acc[0:nq].astype(BF16)
        o_ref[:, LANES * (2 * kvh + 1):LANES * (2 * kvh + 2)] = acc[nq:2 * nq].astype(BF16)


def _kv_tiles(kvh):
    return [slice(LANES * (2 * kvh + h), LANES * (2 * kvh + h + 1)) for h in range(2)]


def _attn_ctx_kernel(sink_ref, q_ref, kd_ref, vd_ref, o_ref, *, l):
    keys = [[kd_ref[:, t] for t in _kv_tiles(kvh)] for kvh in range(KV_HEADS)]
    vals = [[vd_ref[:, t] for t in _kv_tiles(kvh)] for kvh in range(KV_HEADS)]
    _attention(sink_ref, l, q_ref, o_ref, keys, vals, None, q_ref.shape[0])


def _attn_ctx(sink, l, q, kd, vd, seq_len):
    t = q.shape[0]
    row = lambda b: (b, 0)
    return pl.pallas_call(
        functools.partial(_attn_ctx_kernel, l=l),
        grid=(t // seq_len,),
        in_specs=[
            pl.BlockSpec(memory_space=pltpu.SMEM),
            pl.BlockSpec((seq_len, 512), row),
            pl.BlockSpec((seq_len, 512), row),
            pl.BlockSpec((seq_len, 512), row),
        ],
        out_specs=pl.BlockSpec((seq_len, 512), row),
        out_shape=jax.ShapeDtypeStruct((t, 512), BF16),
        compiler_params=_cparams(("parallel",)),
        name="attn_ctx",
    )(sink, q, kd, vd)


def _attn_lat_kernel(sink_ref, q_ref, kp_ref, kc_ref, kn_ref, vp_ref, vc_ref, vn_ref, ck_ref, cv_ref, o_ref,
                     ckt_scr, cvt_scr, *, l):
    i = pl.program_id(1)
    nb = pl.num_programs(1)

    @pl.when(i == 0)
    def _context_tiles():
        for p, (kt, vt) in enumerate(zip(_half_tiles(ck_ref[...]), _half_tiles(cv_ref[...]))):
            ckt_scr[p] = kt
            cvt_scr[p] = vt

    r = lax.broadcasted_iota(jnp.int32, (Q_BLOCK, Q_BLOCK), 0)
    j = lax.broadcasted_iota(jnp.int32, (Q_BLOCK, Q_BLOCK), 1)
    bias = jnp.concatenate([
        jnp.where(jnp.logical_and(j >= r, i > 0), 0.0, NEG_BIG),
        jnp.where(jnp.logical_and(j <= r, i < nb - 1), 0.0, NEG_BIG)], axis=1)
    bias = jnp.concatenate([bias, bias], axis=0)
    def gather(refs, ctx_scr, kvh, h):
        t = _kv_tiles(kvh)[h]
        return jnp.concatenate([ref[:, t] for ref in refs] + [ctx_scr[2 * kvh + h]], axis=0)

    keys = [[gather((kp_ref, kn_ref, kc_ref), ckt_scr, kvh, h) for h in range(2)] for kvh in range(KV_HEADS)]
    vals = [[gather((vp_ref, vn_ref, vc_ref), cvt_scr, kvh, h) for h in range(2)] for kvh in range(KV_HEADS)]
    _attention(sink_ref, l, q_ref, o_ref, keys, vals, bias, Q_BLOCK)


def _attn_lat(sink, l, q, kd, vd, cache_k, cache_v, batch, seq_len):
    nb = seq_len // Q_BLOCK
    past = cache_k.shape[2]
    cur = lambda b, i: (b * nb + i, 0)
    prev = lambda b, i: (b * nb + jnp.maximum(i - 1, 0), 0)
    nxt = lambda b, i: (b * nb + jnp.minimum(i + 1, nb - 1), 0)
    kvspec = lambda f: pl.BlockSpec((Q_BLOCK, 512), f)
    cspec = pl.BlockSpec((None, None, past, KV_DIM), lambda b, i: (b, l, 0, 0))
    ck = cache_k.reshape(batch, DEPTH, past, KV_DIM)
    cv = cache_v.reshape(batch, DEPTH, past, KV_DIM)
    return pl.pallas_call(
        functools.partial(_attn_lat_kernel, l=l),
        grid=(batch, nb),
        in_specs=[
            pl.BlockSpec(memory_space=pltpu.SMEM),
            pl.BlockSpec((Q_BLOCK, 512), cur),
            kvspec(prev), kvspec(cur), kvspec(nxt),
            kvspec(prev), kvspec(cur), kvspec(nxt),
            cspec, cspec,
        ],
        out_specs=pl.BlockSpec((Q_BLOCK, 512), cur),
        out_shape=jax.ShapeDtypeStruct((batch * seq_len, 512), BF16),
        scratch_shapes=[pltpu.VMEM((2 * KV_HEADS, past, LANES), BF16)] * 2,
        compiler_params=_cparams(("parallel", "arbitrary")),
        name="attn_lat",
    )(sink, q, kd, kd, kd, vd, vd, vd, ck, cv)


def _scan_kernel(lre_ref, lim_ref, ldt_ref, bre_ref, bim_ref, cre_ref, cim_ref, *rest, emit_y, chain):
    rest = list(rest)
    if chain:
        s0_ref, ez_ref = rest[:2]
        rest = rest[2:]
    suf_ref, sub_ref = rest[:2]
    rest = rest[2:]
    if emit_y:
        y_refs = rest[:2]
        rest = rest[2:]
    fin_ref, a_scr, bb_scr, h_scr, bu_scr, u_scr = rest[:6]
    if emit_y:
        y_scr, ct_scr = rest[6:8]
    i = pl.program_id(0)
    tt = SCAN_TT
    n = SSM_N

    @pl.when(i == 0)
    def _prologue():
        row_g = lax.shift_right_logical(lax.broadcasted_iota(jnp.int32, (SSM_DIM, n), 0), 4)
        col_g = lax.shift_right_logical(lax.broadcasted_iota(jnp.int32, (SSM_DIM, n), 1), 6)
        own = row_g == col_g

        def blockdiag(ref, d):
            return jnp.where(own, jnp.concatenate([ref[d]] * SSM_GROUPS, axis=0), 0.0)

        for d in range(2):
            lr, li = lre_ref[d], lim_ref[d]
            dt = jnp.exp(ldt_ref[d])
            mag = jnp.exp(lr * dt)
            ar, ai = mag * jnp.cos(li * dt), mag * jnp.sin(li * dt)
            den = lr * lr + li * li
            fr = ((ar - 1.0) * lr + ai * li) / den
            fi = (ai * lr - (ar - 1.0) * li) / den
            bre, bim = blockdiag(bre_ref, d), blockdiag(bim_ref, d)
            bb_scr[d, :, 0:n] = (fr * bre - fi * bim).astype(BF16)
            bb_scr[d, :, n:2 * n] = (fr * bim + fi * bre).astype(BF16)
            if emit_y:
                ct_scr[d, :, 0:n] = blockdiag(cre_ref, d).astype(BF16)
                ct_scr[d, :, n:2 * n] = (-blockdiag(cim_ref, d)).astype(BF16)
            a_scr[d, 0] = jnp.broadcast_to(ar, (8, n))
            a_scr[d, 1] = jnp.broadcast_to(ai, (8, n))
            if chain:
                pr, pi_ = ar, ai
                for _ in range(8):
                    pr, pi_ = pr * pr - pi_ * pi_, 2.0 * pr * pi_
                chunks = SCAN_ROWS // 2
                for b in range(2):
                    hr = s0_ref[d, b:b + 1, 0:n]
                    hi = s0_ref[d, b:b + 1, n:2 * n]
                    order = range(chunks) if d == 0 else range(chunks - 1, -1, -1)
                    for c in order:
                        rw = b * chunks + c
                        h_scr[d, rw:rw + 1, 0:n] = hr
                        h_scr[d, rw:rw + 1, n:2 * n] = hi
                        er = ez_ref[d, rw:rw + 1, 0:n]
                        ei = ez_ref[d, rw:rw + 1, n:2 * n]
                        hr, hi = pr * hr - pi_ * hi + er, pr * hi + pi_ * hr + ei
            else:
                h_scr[d] = jnp.zeros((SCAN_ROWS, 2 * n), F32)

    ntile = n // LANES
    group = SCAN_LW // LANES

    def project_in(d):
        su_ref = suf_ref if d == 0 else sub_ref
        for r in range(SCAN_ROWS):
            for sl in range(SSM_DIM // LANES):
                col = r * SSM_DIM + sl * LANES
                u_scr[d, sl, pl.ds(r, tt, stride=SCAN_ROWS), :] = su_ref[:, col:col + LANES]
        for r0 in range(0, tt * SCAN_ROWS, SCAN_MM_ROWS):
            rows = slice(r0, r0 + SCAN_MM_ROWS)
            u = jnp.concatenate([u_scr[d, sl, rows, :] for sl in range(SSM_DIM // LANES)], axis=1).astype(BF16)
            bu_scr[d, rows, :] = _dot(u, bb_scr[d])

    def recur(d):
        for c0 in range(0, ntile, group):
            lre = slice(c0 * LANES, (c0 + group) * LANES)
            lim = slice(n + c0 * LANES, n + (c0 + group) * LANES)
            ar, ai = a_scr[d, 0, :, lre], a_scr[d, 1, :, lre]
            halves = [slice(0, 8), slice(8, 16)]
            hr = [h_scr[d, hs, lre] for hs in halves]
            hi = [h_scr[d, hs, lim] for hs in halves]
            steps = range(tt) if d == 0 else range(tt - 1, -1, -1)
            for s in steps:
                for k in range(2):
                    rows = slice(s * SCAN_ROWS + 8 * k, s * SCAN_ROWS + 8 * k + 8)
                    br, bi = bu_scr[d, rows, lre], bu_scr[d, rows, lim]
                    hr[k], hi[k] = ar * hr[k] - ai * hi[k] + br, ar * hi[k] + ai * hr[k] + bi
                    if emit_y:
                        bu_scr[d, rows, lre] = hr[k]
                        bu_scr[d, rows, lim] = hi[k]
            for k, hs in enumerate(halves):
                h_scr[d, hs, lre] = hr[k]
                h_scr[d, hs, lim] = hi[k]

    def project_out(d):
        for r0 in range(0, tt * SCAN_ROWS, SCAN_MM_ROWS):
            rows = slice(r0, r0 + SCAN_MM_ROWS)
            y = _dot_t(bu_scr[d, rows, :].astype(BF16), ct_scr[d])
            for sl in range(SSM_DIM // LANES):
                y_scr[d, sl, rows, :] = y[:, sl * LANES:(sl + 1) * LANES]
        for r in range(SCAN_ROWS):
            for sl in range(SSM_DIM // LANES):
                col = r * SSM_DIM + sl * LANES
                y_refs[d][:, col:col + LANES] = y_scr[d, sl, pl.ds(r, tt, stride=SCAN_ROWS), :].astype(BF16)

    project_in(0)
    project_in(1)
    recur(0)
    if emit_y:
        project_out(0)
    recur(1)
    if emit_y:
        project_out(1)

    @pl.when(i == pl.num_programs(0) - 1)
    def _final():
        fin_ref[...] = h_scr[...]


def _scan(prm, l, su_tm, emit_y, s0=None, ez=None):
    chain = s0 is not None
    nt = SCAN_LEN // SCAN_TT
    n = SSM_N
    full3 = lambda shape: pl.BlockSpec(shape, lambda i: (0, 0, 0))
    in_specs = [_layer_spec((2, 1, n), l)] * 3 + [_layer_spec((2, SSM_CH, n), l)] * 4
    args = [prm["lam_re"], prm["lam_im"], prm["log_dt"], prm["b_re"], prm["b_im"], prm["c_re"], prm["c_im"]]
    if chain:
        in_specs += [full3((2, 2, 2 * n)), full3((2, SCAN_ROWS, 2 * n))]
        args += [s0, ez]
    tblk = (SCAN_TT, SCAN_ROWS * SSM_DIM)
    fwd = lambda i: (i, 0)
    bwd = lambda i: (nt - 1 - i, 0)
    in_specs += [pl.BlockSpec(tblk, fwd), pl.BlockSpec(tblk, bwd)]
    args += [su_tm, su_tm]
    out_shape, out_specs = [], []
    scratch = [
        pltpu.VMEM((2, 2, 8, n), F32),
        pltpu.VMEM((2, SSM_DIM, 2 * n), BF16),
        pltpu.VMEM((2, SCAN_ROWS, 2 * n), F32),
        pltpu.VMEM((2, SCAN_ROWS * SCAN_TT, 2 * n), F32),
        pltpu.VMEM((2, SSM_DIM // LANES, SCAN_ROWS * SCAN_TT, LANES), F32),
    ]
    if emit_y:
        yshape = jax.ShapeDtypeStruct((SCAN_LEN, SCAN_ROWS * SSM_DIM), BF16)
        out_shape += [yshape, yshape]
        out_specs += [pl.BlockSpec(tblk, fwd), pl.BlockSpec(tblk, bwd)]
        scratch.append(pltpu.VMEM((2, SSM_DIM // LANES, SCAN_ROWS * SCAN_TT, LANES), F32))
        scratch.append(pltpu.VMEM((2, SSM_DIM, 2 * n), BF16))
    out_shape.append(jax.ShapeDtypeStruct((2, SCAN_ROWS, 2 * n), F32))
    out_specs.append(full3((2, SCAN_ROWS, 2 * n)))
    return pl.pallas_call(
        functools.partial(_scan_kernel, emit_y=emit_y, chain=chain),
        grid=(nt,),
        in_specs=in_specs,
        out_specs=out_specs,
        out_shape=out_shape,
        scratch_shapes=scratch,
        compiler_params=_cparams(("arbitrary",)),
        name="scan_chain" if chain else ("scan_y" if emit_y else "scan_state"),
    )(*args)


def _shifted(scr_ref, off, rows, pos, seq_len, w):
    mid = scr_ref[off:off + rows, :]
    up = jnp.where(pos != 0, scr_ref[off - 1:off - 1 + rows, :], 0.0)
    dn = jnp.where(pos != seq_len - 1, scr_ref[off + 1:off + 1 + rows, :], 0.0)
    return w[0:1, :] * up + w[1:2, :] * mid + w[2:3, :] * dn


def _mix_kernel(x_ref, mod_ref, attn_ref, g3_ref, *rest, halo, tm, seq_len):
    rest = list(rest)
    if halo:
        gp_ref, gn_ref = rest[:2]
        rest = rest[2:]
    (su_ref, yf_ref, yb_ref, wc_ref, dsk_ref, wglu_ref, wo_ref, gn2_ref,
     x1_ref, h2_ref, z_scr, wglu_scr, wo_scr) = rest
    i = pl.program_id(0)

    @pl.when(i == 0)
    def _cast_weights():
        wglu_scr[...] = wglu_ref[...].astype(BF16)
        wo_scr[...] = wo_ref[...].astype(BF16)

    def rows_of(ref):
        pieces = [ref[:, r * SSM_DIM:(r + 1) * SSM_DIM].astype(F32) for r in range(tm // SCAN_LEN)]
        return pieces[0] if len(pieces) == 1 else jnp.concatenate(pieces, axis=0)

    m = mod_ref[...]
    g1, sh2, sc2 = m[:, 2 * D:3 * D], m[:, 3 * D:4 * D], m[:, 4 * D:5 * D]
    pos = (i * tm + lax.broadcasted_iota(jnp.int32, (tm, 1), 0)) % seq_len

    def gate_prod(ref):
        return ref[:, SC_DIM:2 * SC_DIM].astype(F32) * ref[:, 2 * SC_DIM:3 * SC_DIM].astype(F32)

    gb = g3_ref[:, 0:SC_DIM].astype(F32)
    z_scr[16:16 + tm, :] = gate_prod(g3_ref)
    if halo:
        z_scr[0:16, :] = gate_prod(gp_ref)
        z_scr[16 + tm:32 + tm, :] = gate_prod(gn_ref)
    else:
        z_scr[0:16, :] = jnp.zeros((16, SC_DIM), F32)
        z_scr[16 + tm:32 + tm, :] = jnp.zeros((16, SC_DIM), F32)
    conv = gb * _shifted(z_scr, 16, tm, pos, seq_len, wc_ref[...])

    y = dsk_ref[...] * rows_of(su_ref) + rows_of(yf_ref) + rows_of(yb_ref)
    zz = 0.5 * y * (1.0 + jnp.tanh(math.sqrt(2.0 / math.pi) * (y + 0.044715 * (y * y * y))))
    ssm = zz * _sigmoid(_dot(zz.astype(BF16), wglu_scr[...]))

    mix = (_dot(attn_ref[...], wo_scr[0:ATTN_DIM, :])
           + _dot(conv.astype(BF16), wo_scr[ATTN_DIM:ATTN_DIM + SC_DIM, :])
           + _dot(ssm.astype(BF16), wo_scr[ATTN_DIM + SC_DIM:, :]))
    x1 = x_ref[...] + g1 * mix
    x1_ref[...] = x1
    h2_ref[...] = _rms_mod(x1, gn2_ref[...], sc2, sh2).astype(BF16)


def _mix(x, mods, attn, g3, su, yf, yb, prm, l, lat, tokens_per_mod, seq_len, tm=512):
    t = x.shape[0]
    halo = seq_len > tm
    tiles_per_mod = tokens_per_mod // tm
    mod_map = (lambda i: (l, 1 + i // tiles_per_mod, 0, 0)) if lat else (lambda i: (l, 0, 0, 0))
    row = lambda i: (i, 0)
    in_specs = [pl.BlockSpec((tm, D), row), pl.BlockSpec((None, None, 1, 6 * D), mod_map),
                pl.BlockSpec((tm, 512), row), pl.BlockSpec((tm, 768), row)]
    args = [x, mods, attn, g3]
    if halo:
        r16 = tm // 16
        in_specs += [pl.BlockSpec((16, 768), lambda i: (jnp.maximum(i * r16 - 1, 0), 0)),
                     pl.BlockSpec((16, 768), lambda i: (jnp.minimum((i + 1) * r16, t // 16 - 1), 0))]
        args += [g3, g3]
    assert tm % SCAN_LEN == 0
    in_specs += [pl.BlockSpec((SCAN_LEN, tm // SCAN_LEN * SSM_DIM), lambda i: (0, i))] * 3
    args += [su, yf, yb]
    in_specs += [_layer_spec((3, SC_DIM), l), _layer_spec((1, SSM_DIM), l), _layer_spec((SSM_DIM, SSM_DIM), l),
                 _layer_spec((D, D), l), _layer_spec((1, D), l)]
    args += [prm["sc_conv"], prm["ssm_d"], prm["w_glu"], prm["w_out"], prm["norm_ffn"]]
    return pl.pallas_call(
        functools.partial(_mix_kernel, halo=halo, tm=tm, seq_len=seq_len),
        grid=(t // tm,),
        in_specs=in_specs,
        out_specs=[pl.BlockSpec((tm, D), row), pl.BlockSpec((tm, D), row)],
        out_shape=[jax.ShapeDtypeStruct((t, D), F32), jax.ShapeDtypeStruct((t, D), BF16)],
        scratch_shapes=[pltpu.VMEM((tm + 32, SC_DIM), F32), pltpu.VMEM((SSM_DIM, SSM_DIM), BF16),
                        pltpu.VMEM((D, D), BF16)],
        compiler_params=_cparams(("arbitrary",)),
        name="mix_lat" if lat else "mix_ctx",
    )(*args)


def _ffn_kernel(h_ref, *rest, halo, tm, seq_len, final):
    rest = list(rest)
    if halo:
        hp_ref, hn_ref = rest[:2]
        rest = rest[2:]
    x1_ref, mod_ref, wa_ref, wg_ref, ca_ref, cg_ref, wd_ref = rest[:7]
    rest = rest[7:]
    if final:
        gf_ref = rest[0]
        rest = rest[1:]
    o_ref, acc_scr = rest[:2]
    rest = rest[2:]
    nsub = len(FFN_SUB)
    ua_scrs, ug_scrs = rest[:nsub], rest[nsub:2 * nsub]
    i = pl.program_id(0)
    j = pl.program_id(1)

    if halo:
        hcat_scr = rest[2 * nsub]

        @pl.when(j == 0)
        def _stage():
            hcat_scr[0:16, :] = hp_ref[...]
            hcat_scr[16:16 + tm, :] = h_ref[...]
            hcat_scr[16 + tm:32 + tm, :] = hn_ref[...]

        hh = hcat_scr[...]
        pad, seg_len, nseg = 16, tm, 1
        tiles_per_seq = seq_len // tm
        keep_prev = (i % tiles_per_seq != 0).astype(F32)
        keep_next = (i % tiles_per_seq != tiles_per_seq - 1).astype(F32)
    else:
        hh = h_ref[...]
        pad, seg_len, nseg = 8, seq_len, tm // seq_len
    seg_rows = seg_len + 2 * pad
    cols = [sum(FFN_SUB[:c]) for c in range(nsub)]

    def up(c):
        width = FFN_SUB[c]
        for u_scr, w_ref in ((ua_scrs[c], wa_ref), (ug_scrs[c], wg_ref)):
            u = _dot(hh, w_ref[:, cols[c]:cols[c] + width])
            if halo:
                u_scr[...] = u
                u_scr[8:16, :] = u_scr[8:16, :] * keep_prev
                u_scr[16 + tm:24 + tm, :] = u_scr[16 + tm:24 + tm, :] * keep_next
            else:
                for sg in range(nseg):
                    r0 = sg * seg_rows
                    u_scr[r0:r0 + pad, :] = jnp.zeros((pad, width), F32)
                    u_scr[r0 + pad:r0 + pad + seg_len, :] = u[sg * seg_len:(sg + 1) * seg_len]
                    u_scr[r0 + pad + seg_len:r0 + seg_rows, :] = jnp.zeros((pad, width), F32)

    def conv(c, u_scr, cw_ref):
        w = cw_ref[:, cols[c]:cols[c] + FFN_SUB[c]]
        segs = []
        for sg in range(nseg):
            r0 = sg * seg_rows + pad
            segs.append(w[0:1, :] * u_scr[r0 - 1:r0 - 1 + seg_len, :]
                        + w[1:2, :] * u_scr[r0:r0 + seg_len, :]
                        + w[2:3, :] * u_scr[r0 + 1:r0 + 1 + seg_len, :])
        return segs[0] if nseg == 1 else jnp.concatenate(segs, axis=0)

    def down(c):
        a = conv(c, ua_scrs[c], ca_ref)
        g = conv(c, ug_scrs[c], cg_ref)
        act = (a * (g * _sigmoid(g))).astype(BF16)
        acc_scr[...] += _dot(act, wd_ref[cols[c]:cols[c] + FFN_SUB[c], :])

    @pl.when(j == 0)
    def _init():
        acc_scr[...] = jnp.zeros((tm, D), F32)

    up(0)
    for c in range(nsub):
        if c + 1 < nsub:
            up(c + 1)
        down(c)

    @pl.when(j == pl.num_programs(1) - 1)
    def _finish():
        g2 = mod_ref[...][:, 5 * D:6 * D]
        x2 = x1_ref[...] + g2 * acc_scr[...]
        if final:
            x2 = x2 * lax.rsqrt(jnp.mean(x2 * x2, axis=-1, keepdims=True) + RMS_EPS) * gf_ref[...]
        o_ref[...] = x2


def _ffn(h2, x1, mods, prm, l, lat, tokens_per_mod, seq_len, final_g, tm=512):
    t = h2.shape[0]
    tf = FFN_TF
    halo = seq_len > tm
    final = final_g is not None
    nj = D_FF // tf
    tiles_per_mod = tokens_per_mod // tm
    mod_map = (lambda i, j: (l, 1 + i // tiles_per_mod, 0, 0)) if lat else (lambda i, j: (l, 0, 0, 0))
    row = lambda i, j: (i, 0)
    in_specs = [pl.BlockSpec((tm, D), row)]
    args = [h2]
    if halo:
        r16 = tm // 16
        in_specs += [pl.BlockSpec((16, D), lambda i, j: (jnp.maximum(i * r16 - 1, 0), 0)),
                     pl.BlockSpec((16, D), lambda i, j: (jnp.minimum((i + 1) * r16, t // 16 - 1), 0))]
        args += [h2, h2]
    in_specs += [pl.BlockSpec((tm, D), row), pl.BlockSpec((None, None, 1, 6 * D), mod_map),
                 pl.BlockSpec((None, D, tf), lambda i, j: (l, 0, j)),
                 pl.BlockSpec((None, D, tf), lambda i, j: (l, 0, j + nj)),
                 pl.BlockSpec((None, 3, tf), lambda i, j: (l, 0, j)),
                 pl.BlockSpec((None, 3, tf), lambda i, j: (l, 0, j + nj)),
                 pl.BlockSpec((None, tf, D), lambda i, j: (l, j, 0))]
    args += [x1, mods, prm["w_up"], prm["w_up"], prm["ffn_conv"], prm["ffn_conv"], prm["w_down"]]
    if final:
        in_specs.append(pl.BlockSpec((1, D), lambda i, j: (0, 0)))
        args.append(final_g)
    urows = tm + 32 if halo else (tm // seq_len) * (seq_len + 16)
    scratch = [pltpu.VMEM((tm, D), F32)]
    scratch += [pltpu.VMEM((urows, w), F32) for w in FFN_SUB] * 2
    if halo:
        scratch.append(pltpu.VMEM((tm + 32, D), BF16))
    return pl.pallas_call(
        functools.partial(_ffn_kernel, halo=halo, tm=tm, seq_len=seq_len, final=final),
        grid=(t // tm, nj),
        in_specs=in_specs,
        out_specs=pl.BlockSpec((tm, D), row),
        out_shape=jax.ShapeDtypeStruct((t, D), F32),
        scratch_shapes=scratch,
        compiler_params=_cparams(("parallel", "arbitrary")),
        name="ffn_lat" if lat else "ffn_ctx",
    )(*args)


def _rope_tables(seq_len):
    rows = seq_len // GRID_W
    row = np.repeat(np.arange(rows, dtype=np.float32), GRID_W)
    col = np.tile(np.arange(GRID_W, dtype=np.float32), rows)
    freqs = (np.float32(ROPE_BASE) ** (-np.arange(ROPE_FREQS, dtype=np.float32) / np.float32(ROPE_FREQS)))
    freqs = freqs.astype(np.float32)
    ang = np.stack([row[:, None] * freqs, col[:, None] * freqs], axis=1).astype(np.float32)
    cos, sin = np.cos(ang).astype(np.float32), np.sin(ang).astype(np.float32)
    zero = np.zeros_like(sin)
    c64 = np.stack([cos, cos], axis=2).reshape(seq_len, HEAD_DIM)
    sneg64 = np.stack([-sin, zero], axis=2).reshape(seq_len, HEAD_DIM)
    spos64 = np.stack([zero, sin], axis=2).reshape(seq_len, HEAD_DIM)
    return tuple(jnp.asarray(np.tile(tb, (1, 2))) for tb in (c64, sneg64, spos64))


def _prep_params(w_in, w_out, norm_mix, norm_ffn, sc_conv, ssm_lam_re, ssm_lam_im, ssm_log_dt, ssm_b_re,
                 ssm_b_im, ssm_c_re, ssm_c_im, ssm_d, ssm_w_glu, ffn_w_up, ffn_conv, ffn_w_down):
    b_slab = lambda b: jnp.transpose(b, (0, 1, 4, 2, 3)).reshape(DEPTH, 2, SSM_CH, SSM_N)
    c_slab = lambda c: jnp.transpose(c, (0, 1, 3, 2, 4)).reshape(DEPTH, 2, SSM_CH, SSM_N)
    return {
        "w_in": w_in,
        "norm_mix": norm_mix.reshape(DEPTH, 1, D),
        "norm_ffn": norm_ffn.reshape(DEPTH, 1, D),
        "sc_conv": jnp.transpose(sc_conv, (0, 2, 1)),
        "ssm_d": ssm_d.reshape(DEPTH, 1, SSM_DIM),
        "w_glu": ssm_w_glu,
        "w_out": w_out,
        "w_up": ffn_w_up.astype(BF16),
        "ffn_conv": jnp.transpose(ffn_conv, (0, 2, 1)),
        "w_down": ffn_w_down.astype(BF16),
        "lam_re": ssm_lam_re.reshape(DEPTH, 2, 1, SSM_N),
        "lam_im": ssm_lam_im.reshape(DEPTH, 2, 1, SSM_N),
        "log_dt": jnp.repeat(ssm_log_dt, SSM_STATE, axis=-1).reshape(DEPTH, 2, 1, SSM_N),
        "b_re": b_slab(ssm_b_re),
        "b_im": b_slab(ssm_b_im),
        "c_re": c_slab(ssm_c_re),
        "c_im": c_slab(ssm_c_im),
    }


def kernel(x_prompt, x_sample, cache_k, cache_v, state_ssm_re, state_ssm_im, c, c_ctx, norm_mix, norm_ffn, norm_final, w_ada, b_ada, w_in, w_out, attn_sink, sc_conv, ssm_lam_re, ssm_lam_im, ssm_log_dt, ssm_b_re, ssm_b_im, ssm_c_re, ssm_c_im, ssm_d, ssm_w_glu, ffn_w_up, ffn_conv, ffn_w_down):
    batch, seq = x_prompt.shape[0], x_prompt.shape[1]
    dec_batch, dec_seq = x_sample.shape[0], x_sample.shape[1]
    assert batch == SCAN_ROWS and seq == SCAN_LEN
    assert dec_batch * (dec_seq // SCAN_LEN) == SCAN_ROWS and dec_batch == 2

    cs = jnp.concatenate([c_ctx[None, :], c, jnp.zeros((8 - 1 - dec_batch, D), F32)], axis=0)
    mods = _adaln(cs, w_ada, b_ada).reshape(DEPTH, 8, 1, 6 * D)
    rope_tabs = _rope_tables(dec_seq)
    gfin = norm_final.reshape(1, D)
    prm = _prep_params(w_in, w_out, norm_mix, norm_ffn, sc_conv, ssm_lam_re, ssm_lam_im, ssm_log_dt, ssm_b_re,
                       ssm_b_im, ssm_c_re, ssm_c_im, ssm_d, ssm_w_glu, ffn_w_up, ffn_conv, ffn_w_down)
    s0_all = jnp.transpose(jnp.concatenate([state_ssm_re.reshape(dec_batch, DEPTH, 2, SSM_N),
                                            state_ssm_im.reshape(dec_batch, DEPTH, 2, SSM_N)], axis=-1),
                           (1, 2, 0, 3))

    xp = x_prompt.reshape(batch * seq, D)
    xs = x_sample.reshape(dec_batch * dec_seq, D)
    kv_out, fin_out = [], []
    for l in range(DEPTH):
        last = gfin if l == DEPTH - 1 else None

        q, kd, vd, kv, g3, su = _inproj(xp, mods, prm, l, None, batch * seq, seq)
        attn = _attn_ctx(attn_sink, l, q, kd, vd, seq)
        yf, yb, fin = _scan(prm, l, su, True)
        x1, h2 = _mix(xp, mods, attn, g3, su, yf, yb, prm, l, False, batch * seq, seq)
        xp = _ffn(h2, x1, mods, prm, l, False, batch * seq, seq, last)
        kv_out.append(kv)
        fin_out.append(fin)

        q, kd, vd, g3, su = _inproj(xs, mods, prm, l, rope_tabs, dec_seq, dec_seq)
        attn = _attn_lat(attn_sink, l, q, kd, vd, cache_k, cache_v, dec_batch, dec_seq)
        (ez,) = _scan(prm, l, su, False)
        yf, yb, _ = _scan(prm, l, su, True, s0=s0_all[l], ez=ez)
        x1, h2 = _mix(xs, mods, attn, g3, su, yf, yb, prm, l, True, dec_seq, dec_seq)
        xs = _ffn(h2, x1, mods, prm, l, True, dec_seq, dec_seq, last)

    kv_all = jnp.stack(kv_out, axis=0).reshape(DEPTH, batch, seq, 2, KV_HEADS, HEAD_DIM)
    kv_all = jnp.transpose(kv_all, (3, 1, 0, 2, 4, 5))
    fin_all = jnp.stack(fin_out, axis=0).reshape(DEPTH, 2, batch, 2, SSM_GROUPS, SSM_STATE)
    fin_all = jnp.transpose(fin_all, (3, 2, 0, 1, 4, 5))
    return (xp.reshape(batch, seq, D), xs.reshape(dec_batch, dec_seq, D),
            kv_all[0], kv_all[1], fin_all[0], fin_all[1])
```

```python
import functools
import math

import jax
import jax.numpy as jnp
import numpy as np
from jax import lax
from jax.experimental import pallas as pl
from jax.experimental.pallas import tpu as pltpu

F32 = jnp.float32
BF16 = jnp.bfloat16

D = 1024
DEPTH = 2
GRID_W = 64
ATTN_DIM = 512
SC_DIM = 256
SSM_DIM = 256
HEAD_DIM = 64
N_HEADS = 8
KV_HEADS = 2
KV_DIM = 128
WINDOW = 128
Q_BLOCK = 128
ROPE_BASE = 10000.0
ROPE_FREQS = 16
SSM_CH = 16
SSM_GROUPS = 16
SSM_STATE = 64
SSM_N = SSM_GROUPS * SSM_STATE
IN_DIM = ATTN_DIM + 2 * KV_DIM + 3 * SC_DIM + SSM_DIM
D_FF = 2816
LOG2E = math.log2(math.e)
Q_SCALE = LOG2E / math.sqrt(HEAD_DIM)
RMS_EPS = 1e-6
NEG_BIG = -1e30

LANES = 128
SCAN_ROWS = 16
SCAN_LEN = 256
SCAN_TT = 64
SCAN_LW = 256
SCAN_MM_ROWS = 256
FFN_TF = 1408
FFN_SUB = (768, 512, 128)
assert D_FF % FFN_TF == 0 and sum(FFN_SUB) == FFN_TF
VMEM_LIMIT = 56 * 1024 * 1024


def _cparams(sem):
    return pltpu.CompilerParams(dimension_semantics=sem, vmem_limit_bytes=VMEM_LIMIT)


def _sigmoid(x):
    return 1.0 / (1.0 + jnp.exp(-x))


def _rms_mod(x, g, scale, shift):
    y = x * lax.rsqrt(jnp.mean(x * x, axis=-1, keepdims=True) + RMS_EPS) * g
    return y * (1.0 + scale) + shift


def _split_bf16(v):
    hi = v.astype(BF16)
    lo = (v - hi.astype(F32)).astype(BF16)
    return hi, lo


def _dot(a, b):
    return jnp.dot(a, b, preferred_element_type=F32)


def _dot_t(a, b):
    return lax.dot_general(a, b, (((1,), (1,)), ((), ())), preferred_element_type=F32)


def _layer_spec(shape, l):
    zeros = (0,) * len(shape)
    return pl.BlockSpec((None,) + tuple(shape), lambda *_: (l,) + zeros)


def _half_tiles(t):
    lo = lax.broadcasted_iota(jnp.int32, t.shape, 1) < HEAD_DIM
    tr = pltpu.roll(t, HEAD_DIM, 1)
    return [jnp.where(lo, t, 0.0).astype(BF16), jnp.where(lo, 0.0, tr).astype(BF16),
            jnp.where(lo, tr, 0.0).astype(BF16), jnp.where(lo, 0.0, t).astype(BF16)]


def _adaln_kernel(c_ref, w_ref, b_ref, o_ref):
    c = c_ref[...]
    s = c * _sigmoid(c)
    s_hi, s_lo = _split_bf16(s)
    w_hi, w_lo = _split_bf16(w_ref[...])
    o_ref[...] = _dot(s_hi, w_hi) + _dot(s_lo, w_hi) + _dot(s_hi, w_lo) + b_ref[...]


def _adaln(cs, w_ada, b_ada):
    tn = 2048
    return pl.pallas_call(
        _adaln_kernel,
        grid=(DEPTH, 6 * D // tn),
        in_specs=[
            pl.BlockSpec((8, D), lambda l, j: (0, 0)),
            pl.BlockSpec((None, D, tn), lambda l, j: (l, 0, j)),
            pl.BlockSpec((None, 1, tn), lambda l, j: (l, 0, j)),
        ],
        out_specs=pl.BlockSpec((None, 8, tn), lambda l, j: (l, 0, j)),
        out_shape=jax.ShapeDtypeStruct((DEPTH, 8, 6 * D), F32),
        compiler_params=_cparams(("parallel", "parallel")),
        name="adaln",
    )(cs, w_ada, b_ada.reshape(DEPTH, 1, 6 * D))


def _inproj_kernel(x_ref, mod_ref, g_ref, w_ref, *rest, rope):
    if rope:
        cos_ref, sneg_ref, spos_ref, q_ref, kd_ref, vd_ref, g3_ref, su_ref, wbf_scr = rest
    else:
        q_ref, kd_ref, vd_ref, kv_ref, g3_ref, su_ref, wbf_scr = rest

    @pl.when(pl.program_id(0) == 0)
    def _cast_weights():
        wbf_scr[...] = w_ref[...].astype(BF16)

    m = mod_ref[...]
    h = _rms_mod(x_ref[...], g_ref[...], m[:, D:2 * D], m[:, 0:D]).astype(BF16)
    acc = _dot(h, wbf_scr[...])
    k = acc[:, 512:640]
    v = acc[:, 640:768]
    if rope:
        c, sn, sp = cos_ref[...], sneg_ref[...], spos_ref[...]

        def rot(t):
            return t * c + pltpu.roll(t, LANES - ROPE_FREQS, 1) * sn + pltpu.roll(t, ROPE_FREQS, 1) * sp

        for p in range(4):
            q_ref[:, LANES * p:LANES * (p + 1)] = (rot(acc[:, LANES * p:LANES * (p + 1)]) * Q_SCALE).astype(BF16)
        k = rot(k)
    else:
        q_ref[...] = (acc[:, 0:512] * Q_SCALE).astype(BF16)
        kv_ref[...] = acc[:, 512:768]
    for ref, t in ((kd_ref, k), (vd_ref, v)):
        for p, tile in enumerate(_half_tiles(t)):
            ref[:, LANES * p:LANES * (p + 1)] = tile
    g3_ref[...] = acc[:, 768:1536].astype(BF16)
    for r in range(acc.shape[0] // SCAN_LEN):
        su_ref[:, r * SSM_DIM:(r + 1) * SSM_DIM] = acc[r * SCAN_LEN:(r + 1) * SCAN_LEN, 1536:1792]


def _inproj(x, mods, prm, l, rope_tabs, tokens_per_mod, seq_len, tm=512):
    t = x.shape[0]
    rope = rope_tabs is not None
    tiles_per_mod = tokens_per_mod // tm
    tiles_per_seq = seq_len // tm
    if rope:
        mod_map = lambda i: (l, 1 + i // tiles_per_mod, 0, 0)
    else:
        mod_map = lambda i: (l, 0, 0, 0)
    row = lambda i: (i, 0)
    in_specs = [
        pl.BlockSpec((tm, D), row),
        pl.BlockSpec((None, None, 1, 6 * D), mod_map),
        _layer_spec((1, D), l),
        _layer_spec((D, IN_DIM), l),
    ]
    args = [x, mods, prm["norm_mix"], prm["w_in"]]
    if rope:
        in_specs += [pl.BlockSpec((tm, LANES), lambda i: (i % tiles_per_seq, 0))] * 3
        args += list(rope_tabs)
    out_shape = [jax.ShapeDtypeStruct((t, 512), BF16)] * 3
    out_specs = [pl.BlockSpec((tm, 512), row)] * 3
    if not rope:
        out_shape.append(jax.ShapeDtypeStruct((t, 256), F32))
        out_specs.append(pl.BlockSpec((tm, 256), row))
    out_shape += [jax.ShapeDtypeStruct((t, 768), BF16),
                  jax.ShapeDtypeStruct((SCAN_LEN, t // SCAN_LEN * SSM_DIM), F32)]
    out_specs += [pl.BlockSpec((tm, 768), row),
                  pl.BlockSpec((SCAN_LEN, tm // SCAN_LEN * SSM_DIM), lambda i: (0, i))]
    return pl.pallas_call(
        functools.partial(_inproj_kernel, rope=rope),
        grid=(t // tm,),
        in_specs=in_specs,
        out_specs=out_specs,
        out_shape=out_shape,
        scratch_shapes=[pltpu.VMEM((D, IN_DIM), BF16)],
        compiler_params=_cparams(("arbitrary",)),
        name="inproj_lat" if rope else "inproj_ctx",
    )(*args)


def _attention(sink_ref, l, q_ref, o_ref, keys, vals, bias, nq):
    top = lax.broadcasted_iota(jnp.int32, (2 * nq, 1), 0) < nq
    scores = []
    for kvh in range(KV_HEADS):
        q2 = jnp.concatenate([q_ref[:, LANES * (2 * kvh + pp):LANES * (2 * kvh + pp + 1)] for pp in range(2)],
                             axis=0)
        scores.append([_dot_t(q2, keys[kvh][half]) for half in range(2)])
    for kvh in range(KV_HEADS):
        acc = None
        for half in range(2):
            sk = jnp.where(top, sink_ref[l, 4 * kvh + half], sink_ref[l, 4 * kvh + 2 + half]) * LOG2E
            s = scores[kvh][half]
            if bias is not None:
                nb_ = bias.shape[1]
                s = jnp.concatenate([s[:, 0:nb_] + bias, s[:, nb_:]], axis=1)
            m = jnp.maximum(jnp.max(s, axis=-1, keepdims=True), sk)
            e = jnp.exp2(s - m)
            den = jnp.sum(e, axis=-1, keepdims=True) + jnp.exp2(sk - m)
            o = _dot(e.astype(BF16), vals[kvh][half]) / den
            acc = o if acc is None else acc + o
        o_ref[:, LANES * 2 * kvh:LANES * (2 * kvh + 1)] = acc[0:nq].astype(BF16)
        o_ref[:, LANES * (2 * kvh + 1):LANES * (2 * kvh + 2)] = acc[nq:2 * nq].astype(BF16)


def _kv_tiles(kvh):
    return [slice(LANES * (2 * kvh + h), LANES * (2 * kvh + h + 1)) for h in range(2)]


def _attn_ctx_kernel(sink_ref, q_ref, kd_ref, vd_ref, o_ref, *, l):
    keys = [[kd_ref[:, t] for t in _kv_tiles(kvh)] for kvh in range(KV_HEADS)]
    vals = [[vd_ref[:, t] for t in _kv_tiles(kvh)] for kvh in range(KV_HEADS)]
    _attention(sink_ref, l, q_ref, o_ref, keys, vals, None, q_ref.shape[0])


def _attn_ctx(sink, l, q, kd, vd, seq_len):
    t = q.shape[0]
    row = lambda b: (b, 0)
    return pl.pallas_call(
        functools.partial(_attn_ctx_kernel, l=l),
        grid=(t // seq_len,),
        in_specs=[
            pl.BlockSpec(memory_space=pltpu.SMEM),
            pl.BlockSpec((seq_len, 512), row),
            pl.BlockSpec((seq_len, 512), row),
            pl.BlockSpec((seq_len, 512), row),
        ],
        out_specs=pl.BlockSpec((seq_len, 512), row),
        out_shape=jax.ShapeDtypeStruct((t, 512), BF16),
        compiler_params=_cparams(("parallel",)),
        name="attn_ctx",
    )(sink, q, kd, vd)


def _attn_lat_kernel(sink_ref, q_ref, kp_ref, kc_ref, kn_ref, vp_ref, vc_ref, vn_ref, ck_ref, cv_ref, o_ref,
                     ckt_scr, cvt_scr, *, l):
    i = pl.program_id(1)
    nb = pl.num_programs(1)

    @pl.when(i == 0)
    def _context_tiles():
        for p, (kt, vt) in enumerate(zip(_half_tiles(ck_ref[...]), _half_tiles(cv_ref[...]))):
            ckt_scr[p] = kt
            cvt_scr[p] = vt

    r = lax.broadcasted_iota(jnp.int32, (Q_BLOCK, Q_BLOCK), 0)
    j = lax.broadcasted_iota(jnp.int32, (Q_BLOCK, Q_BLOCK), 1)
    bias = jnp.concatenate([
        jnp.where(jnp.logical_and(j >= r, i > 0), 0.0, NEG_BIG),
        jnp.where(jnp.logical_and(j <= r, i < nb - 1), 0.0, NEG_BIG)], axis=1)
    bias = jnp.concatenate([bias, bias], axis=0)
    def gather(refs, ctx_scr, kvh, h):
        t = _kv_tiles(kvh)[h]
        return jnp.concatenate([ref[:, t] for ref in refs] + [ctx_scr[2 * kvh + h]], axis=0)

    keys = [[gather((kp_ref, kn_ref, kc_ref), ckt_scr, kvh, h) for h in range(2)] for kvh in range(KV_HEADS)]
    vals = [[gather((vp_ref, vn_ref, vc_ref), cvt_scr, kvh, h) for h in range(2)] for kvh in range(KV_HEADS)]
    _attention(sink_ref, l, q_ref, o_ref, keys, vals, bias, Q_BLOCK)


def _attn_lat(sink, l, q, kd, vd, cache_k, cache_v, batch, seq_len):
    nb = seq_len // Q_BLOCK
    past = cache_k.shape[2]
    cur = lambda b, i: (b * nb + i, 0)
    prev = lambda b, i: (b * nb + jnp.maximum(i - 1, 0), 0)
    nxt = lambda b, i: (b * nb + jnp.minimum(i + 1, nb - 1), 0)
    kvspec = lambda f: pl.BlockSpec((Q_BLOCK, 512), f)
    cspec = pl.BlockSpec((None, None, past, KV_DIM), lambda b, i: (b, l, 0, 0))
    ck = cache_k.reshape(batch, DEPTH, past, KV_DIM)
    cv = cache_v.reshape(batch, DEPTH, past, KV_DIM)
    return pl.pallas_call(
        functools.partial(_attn_lat_kernel, l=l),
        grid=(batch, nb),
        in_specs=[
            pl.BlockSpec(memory_space=pltpu.SMEM),
            pl.BlockSpec((Q_BLOCK, 512), cur),
            kvspec(prev), kvspec(cur), kvspec(nxt),
            kvspec(prev), kvspec(cur), kvspec(nxt),
            cspec, cspec,
        ],
        out_specs=pl.BlockSpec((Q_BLOCK, 512), cur),
        out_shape=jax.ShapeDtypeStruct((batch * seq_len, 512), BF16),
        scratch_shapes=[pltpu.VMEM((2 * KV_HEADS, past, LANES), BF16)] * 2,
        compiler_params=_cparams(("parallel", "arbitrary")),
        name="attn_lat",
    )(sink, q, kd, kd, kd, vd, vd, vd, ck, cv)


def _scan_kernel(lre_ref, lim_ref, ldt_ref, bre_ref, bim_ref, cre_ref, cim_ref, *rest, emit_y, chain):
    rest = list(rest)
    if chain:
        s0_ref, ez_ref = rest[:2]
        rest = rest[2:]
    suf_ref, sub_ref = rest[:2]
    rest = rest[2:]
    if emit_y:
        y_refs = rest[:2]
        rest = rest[2:]
    fin_ref, a_scr, bb_scr, h_scr, bu_scr, u_scr = rest[:6]
    if emit_y:
        y_scr, ct_scr = rest[6:8]
    i = pl.program_id(0)
    tt = SCAN_TT
    n = SSM_N

    @pl.when(i == 0)
    def _prologue():
        row_g = lax.shift_right_logical(lax.broadcasted_iota(jnp.int32, (SSM_DIM, n), 0), 4)
        col_g = lax.shift_right_logical(lax.broadcasted_iota(jnp.int32, (SSM_DIM, n), 1), 6)
        own = row_g == col_g

        def blockdiag(ref, d):
            return jnp.where(own, jnp.concatenate([ref[d]] * SSM_GROUPS, axis=0), 0.0)

        for d in range(2):
            lr, li = lre_ref[d], lim_ref[d]
            dt = jnp.exp(ldt_ref[d])
            mag = jnp.exp(lr * dt)
            ar, ai = mag * jnp.cos(li * dt), mag * jnp.sin(li * dt)
            den = lr * lr + li * li
            fr = ((ar - 1.0) * lr + ai * li) / den
            fi = (ai * lr - (ar - 1.0) * li) / den
            bre, bim = blockdiag(bre_ref, d), blockdiag(bim_ref, d)
            bb_scr[d, :, 0:n] = (fr * bre - fi * bim).astype(BF16)
            bb_scr[d, :, n:2 * n] = (fr * bim + fi * bre).astype(BF16)
            if emit_y:
                ct_scr[d, :, 0:n] = blockdiag(cre_ref, d).astype(BF16)
                ct_scr[d, :, n:2 * n] = (-blockdiag(cim_ref, d)).astype(BF16)
            a_scr[d, 0] = jnp.broadcast_to(ar, (8, n))
            a_scr[d, 1] = jnp.broadcast_to(ai, (8, n))
            if chain:
                pr, pi_ = ar, ai
                for _ in range(8):
                    pr, pi_ = pr * pr - pi_ * pi_, 2.0 * pr * pi_
                chunks = SCAN_ROWS // 2
                for b in range(2):
                    hr = s0_ref[d, b:b + 1, 0:n]
                    hi = s0_ref[d, b:b + 1, n:2 * n]
                    order = range(chunks) if d == 0 else range(chunks - 1, -1, -1)
                    for c in order:
                        rw = b * chunks + c
                        h_scr[d, rw:rw + 1, 0:n] = hr
                        h_scr[d, rw:rw + 1, n:2 * n] = hi
                        er = ez_ref[d, rw:rw + 1, 0:n]
                        ei = ez_ref[d, rw:rw + 1, n:2 * n]
                        hr, hi = pr * hr - pi_ * hi + er, pr * hi + pi_ * hr + ei
            else:
                h_scr[d] = jnp.zeros((SCAN_ROWS, 2 * n), F32)

    ntile = n // LANES
    group = SCAN_LW // LANES

    def project_in(d):
        su_ref = suf_ref if d == 0 else sub_ref
        for r in range(SCAN_ROWS):
            for sl in range(SSM_DIM // LANES):
                col = r * SSM_DIM + sl * LANES
                u_scr[d, sl, pl.ds(r, tt, stride=SCAN_ROWS), :] = su_ref[:, col:col + LANES]
        for r0 in range(0, tt * SCAN_ROWS, SCAN_MM_ROWS):
            rows = slice(r0, r0 + SCAN_MM_ROWS)
            u = jnp.concatenate([u_scr[d, sl, rows, :] for sl in range(SSM_DIM // LANES)], axis=1).astype(BF16)
            bu_scr[d, rows, :] = _dot(u, bb_scr[d])

    def recur(d):
        for c0 in range(0, ntile, group):
            lre = slice(c0 * LANES, (c0 + group) * LANES)
            lim = slice(n + c0 * LANES, n + (c0 + group) * LANES)
            ar, ai = a_scr[d, 0, :, lre], a_scr[d, 1, :, lre]
            halves = [slice(0, 8), slice(8, 16)]
            hr = [h_scr[d, hs, lre] for hs in halves]
            hi = [h_scr[d, hs, lim] for hs in halves]
            steps = range(tt) if d == 0 else range(tt - 1, -1, -1)
            for s in steps:
                for k in range(2):
                    rows = slice(s * SCAN_ROWS + 8 * k, s * SCAN_ROWS + 8 * k + 8)
                    br, bi = bu_scr[d, rows, lre], bu_scr[d, rows, lim]
                    hr[k], hi[k] = ar * hr[k] - ai * hi[k] + br, ar * hi[k] + ai * hr[k] + bi
                    if emit_y:
                        bu_scr[d, rows, lre] = hr[k]
                        bu_scr[d, rows, lim] = hi[k]
            for k, hs in enumerate(halves):
                h_scr[d, hs, lre] = hr[k]
                h_scr[d, hs, lim] = hi[k]

    def project_out(d):
        for r0 in range(0, tt * SCAN_ROWS, SCAN_MM_ROWS):
            rows = slice(r0, r0 + SCAN_MM_ROWS)
            y = _dot_t(bu_scr[d, rows, :].astype(BF16), ct_scr[d])
            for sl in range(SSM_DIM // LANES):
                y_scr[d, sl, rows, :] = y[:, sl * LANES:(sl + 1) * LANES]
        for r in range(SCAN_ROWS):
            for sl in range(SSM_DIM // LANES):
                col = r * SSM_DIM + sl * LANES
                y_refs[d][:, col:col + LANES] = y_scr[d, sl, pl.ds(r, tt, stride=SCAN_ROWS), :].astype(BF16)

    project_in(0)
    project_in(1)
    recur(0)
    if emit_y:
        project_out(0)
    recur(1)
    if emit_y:
        project_out(1)

    @pl.when(i == pl.num_programs(0) - 1)
    def _final():
        fin_ref[...] = h_scr[...]


def _scan(prm, l, su_tm, emit_y, s0=None, ez=None):
    chain = s0 is not None
    nt = SCAN_LEN // SCAN_TT
    n = SSM_N
    full3 = lambda shape: pl.BlockSpec(shape, lambda i: (0, 0, 0))
    in_specs = [_layer_spec((2, 1, n), l)] * 3 + [_layer_spec((2, SSM_CH, n), l)] * 4
    args = [prm["lam_re"], prm["lam_im"], prm["log_dt"], prm["b_re"], prm["b_im"], prm["c_re"], prm["c_im"]]
    if chain:
        in_specs += [full3((2, 2, 2 * n)), full3((2, SCAN_ROWS, 2 * n))]
        args += [s0, ez]
    tblk = (SCAN_TT, SCAN_ROWS * SSM_DIM)
    fwd = lambda i: (i, 0)
    bwd = lambda i: (nt - 1 - i, 0)
    in_specs += [pl.BlockSpec(tblk, fwd), pl.BlockSpec(tblk, bwd)]
    args += [su_tm, su_tm]
    out_shape, out_specs = [], []
    scratch = [
        pltpu.VMEM((2, 2, 8, n), F32),
        pltpu.VMEM((2, SSM_DIM, 2 * n), BF16),
        pltpu.VMEM((2, SCAN_ROWS, 2 * n), F32),
        pltpu.VMEM((2, SCAN_ROWS * SCAN_TT, 2 * n), F32),
        pltpu.VMEM((2, SSM_DIM // LANES, SCAN_ROWS * SCAN_TT, LANES), F32),
    ]
    if emit_y:
        yshape = jax.ShapeDtypeStruct((SCAN_LEN, SCAN_ROWS * SSM_DIM), BF16)
        out_shape += [yshape, yshape]
        out_specs += [pl.BlockSpec(tblk, fwd), pl.BlockSpec(tblk, bwd)]
        scratch.append(pltpu.VMEM((2, SSM_DIM // LANES, SCAN_ROWS * SCAN_TT, LANES), F32))
        scratch.append(pltpu.VMEM((2, SSM_DIM, 2 * n), BF16))
    out_shape.append(jax.ShapeDtypeStruct((2, SCAN_ROWS, 2 * n), F32))
    out_specs.append(full3((2, SCAN_ROWS, 2 * n)))
    return pl.pallas_call(
        functools.partial(_scan_kernel, emit_y=emit_y, chain=chain),
        grid=(nt,),
        in_specs=in_specs,
        out_specs=out_specs,
        out_shape=out_shape,
        scratch_shapes=scratch,
        compiler_params=_cparams(("arbitrary",)),
        name="scan_chain" if chain else ("scan_y" if emit_y else "scan_state"),
    )(*args)


def _shifted(scr_ref, off, rows, pos, seq_len, w):
    mid = scr_ref[off:off + rows, :]
    up = jnp.where(pos != 0, scr_ref[off - 1:off - 1 + rows, :], 0.0)
    dn = jnp.where(pos != seq_len - 1, scr_ref[off + 1:off + 1 + rows, :], 0.0)
    return w[0:1, :] * up + w[1:2, :] * mid + w[2:3, :] * dn


def _mix_kernel(x_ref, mod_ref, attn_ref, g3_ref, *rest, halo, tm, seq_len):
    rest = list(rest)
    if halo:
        gp_ref, gn_ref = rest[:2]
        rest = rest[2:]
    (su_ref, yf_ref, yb_ref, wc_ref, dsk_ref, wglu_ref, wo_ref, gn2_ref,
     x1_ref, h2_ref, z_scr, wglu_scr, wo_scr) = rest
    i = pl.program_id(0)

    @pl.when(i == 0)
    def _cast_weights():
        wglu_scr[...] = wglu_ref[...].astype(BF16)
        wo_scr[...] = wo_ref[...].astype(BF16)

    def rows_of(ref):
        pieces = [ref[:, r * SSM_DIM:(r + 1) * SSM_DIM].astype(F32) for r in range(tm // SCAN_LEN)]
        return pieces[0] if len(pieces) == 1 else jnp.concatenate(pieces, axis=0)

    m = mod_ref[...]
    g1, sh2, sc2 = m[:, 2 * D:3 * D], m[:, 3 * D:4 * D], m[:, 4 * D:5 * D]
    pos = (i * tm + lax.broadcasted_iota(jnp.int32, (tm, 1), 0)) % seq_len

    def gate_prod(ref):
        return ref[:, SC_DIM:2 * SC_DIM].astype(F32) * ref[:, 2 * SC_DIM:3 * SC_DIM].astype(F32)

    gb = g3_ref[:, 0:SC_DIM].astype(F32)
    z_scr[16:16 + tm, :] = gate_prod(g3_ref)
    if halo:
        z_scr[0:16, :] = gate_prod(gp_ref)
        z_scr[16 + tm:32 + tm, :] = gate_prod(gn_ref)
    else:
        z_scr[0:16, :] = jnp.zeros((16, SC_DIM), F32)
        z_scr[16 + tm:32 + tm, :] = jnp.zeros((16, SC_DIM), F32)
    conv = gb * _shifted(z_scr, 16, tm, pos, seq_len, wc_ref[...])

    y = dsk_ref[...] * rows_of(su_ref) + rows_of(yf_ref) + rows_of(yb_ref)
    zz = 0.5 * y * (1.0 + jnp.tanh(math.sqrt(2.0 / math.pi) * (y + 0.044715 * (y * y * y))))
    ssm = zz * _sigmoid(_dot(zz.astype(BF16), wglu_scr[...]))

    mix = (_dot(attn_ref[...], wo_scr[0:ATTN_DIM, :])
           + _dot(conv.astype(BF16), wo_scr[ATTN_DIM:ATTN_DIM + SC_DIM, :])
           + _dot(ssm.astype(BF16), wo_scr[ATTN_DIM + SC_DIM:, :]))
    x1 = x_ref[...] + g1 * mix
    x1_ref[...] = x1
    h2_ref[...] = _rms_mod(x1, gn2_ref[...], sc2, sh2).astype(BF16)


def _mix(x, mods, attn, g3, su, yf, yb, prm, l, lat, tokens_per_mod, seq_len, tm=512):
    t = x.shape[0]
    halo = seq_len > tm
    tiles_per_mod = tokens_per_mod // tm
    mod_map = (lambda i: (l, 1 + i // tiles_per_mod, 0, 0)) if lat else (lambda i: (l, 0, 0, 0))
    row = lambda i: (i, 0)
    in_specs = [pl.BlockSpec((tm, D), row), pl.BlockSpec((None, None, 1, 6 * D), mod_map),
                pl.BlockSpec((tm, 512), row), pl.BlockSpec((tm, 768), row)]
    args = [x, mods, attn, g3]
    if halo:
        r16 = tm // 16
        in_specs += [pl.BlockSpec((16, 768), lambda i: (jnp.maximum(i * r16 - 1, 0), 0)),
                     pl.BlockSpec((16, 768), lambda i: (jnp.minimum((i + 1) * r16, t // 16 - 1), 0))]
        args += [g3, g3]
    assert tm % SCAN_LEN == 0
    in_specs += [pl.BlockSpec((SCAN_LEN, tm // SCAN_LEN * SSM_DIM), lambda i: (0, i))] * 3
    args += [su, yf, yb]
    in_specs += [_layer_spec((3, SC_DIM), l), _layer_spec((1, SSM_DIM), l), _layer_spec((SSM_DIM, SSM_DIM), l),
                 _layer_spec((D, D), l), _layer_spec((1, D), l)]
    args += [prm["sc_conv"], prm["ssm_d"], prm["w_glu"], prm["w_out"], prm["norm_ffn"]]
    return pl.pallas_call(
        functools.partial(_mix_kernel, halo=halo, tm=tm, seq_len=seq_len),
        grid=(t // tm,),
        in_specs=in_specs,
        out_specs=[pl.BlockSpec((tm, D), row), pl.BlockSpec((tm, D), row)],
        out_shape=[jax.ShapeDtypeStruct((t, D), F32), jax.ShapeDtypeStruct((t, D), BF16)],
        scratch_shapes=[pltpu.VMEM((tm + 32, SC_DIM), F32), pltpu.VMEM((SSM_DIM, SSM_DIM), BF16),
                        pltpu.VMEM((D, D), BF16)],
        compiler_params=_cparams(("arbitrary",)),
        name="mix_lat" if lat else "mix_ctx",
    )(*args)


def _ffn_kernel(h_ref, *rest, halo, tm, seq_len, final):
    rest = list(rest)
    if halo:
        hp_ref, hn_ref = rest[:2]
        rest = rest[2:]
    x1_ref, mod_ref, wa_ref, wg_ref, ca_ref, cg_ref, wd_ref = rest[:7]
    rest = rest[7:]
    if final:
        gf_ref = rest[0]
        rest = rest[1:]
    o_ref, acc_scr = rest[:2]
    rest = rest[2:]
    nsub = len(FFN_SUB)
    ua_scrs, ug_scrs = rest[:nsub], rest[nsub:2 * nsub]
    i = pl.program_id(0)
    j = pl.program_id(1)

    if halo:
        hcat_scr = rest[2 * nsub]

        @pl.when(j == 0)
        def _stage():
            hcat_scr[0:16, :] = hp_ref[...]
            hcat_scr[16:16 + tm, :] = h_ref[...]
            hcat_scr[16 + tm:32 + tm, :] = hn_ref[...]

        hh = hcat_scr[...]
        pad, seg_len, nseg = 16, tm, 1
        tiles_per_seq = seq_len // tm
        keep_prev = (i % tiles_per_seq != 0).astype(F32)
        keep_next = (i % tiles_per_seq != tiles_per_seq - 1).astype(F32)
    else:
        hh = h_ref[...]
        pad, seg_len, nseg = 8, seq_len, tm // seq_len
    seg_rows = seg_len + 2 * pad
    cols = [sum(FFN_SUB[:c]) for c in range(nsub)]

    def up(c):
        width = FFN_SUB[c]
        for u_scr, w_ref in ((ua_scrs[c], wa_ref), (ug_scrs[c], wg_ref)):
            u = _dot(hh, w_ref[:, cols[c]:cols[c] + width])
            if halo:
                u_scr[...] = u
                u_scr[8:16, :] = u_scr[8:16, :] * keep_prev
                u_scr[16 + tm:24 + tm, :] = u_scr[16 + tm:24 + tm, :] * keep_next
            else:
                for sg in range(nseg):
                    r0 = sg * seg_rows
                    u_scr[r0:r0 + pad, :] = jnp.zeros((pad, width), F32)
                    u_scr[r0 + pad:r0 + pad + seg_len, :] = u[sg * seg_len:(sg + 1) * seg_len]
                    u_scr[r0 + pad + seg_len:r0 + seg_rows, :] = jnp.zeros((pad, width), F32)

    def conv(c, u_scr, cw_ref):
        w = cw_ref[:, cols[c]:cols[c] + FFN_SUB[c]]
        segs = []
        for sg in range(nseg):
            r0 = sg * seg_rows + pad
            segs.append(w[0:1, :] * u_scr[r0 - 1:r0 - 1 + seg_len, :]
                        + w[1:2, :] * u_scr[r0:r0 + seg_len, :]
                        + w[2:3, :] * u_scr[r0 + 1:r0 + 1 + seg_len, :])
        return segs[0] if nseg == 1 else jnp.concatenate(segs, axis=0)

    def down(c):
        a = conv(c, ua_scrs[c], ca_ref)
        g = conv(c, ug_scrs[c], cg_ref)
        act = (a * (g * _sigmoid(g))).astype(BF16)
        acc_scr[...] += _dot(act, wd_ref[cols[c]:cols[c] + FFN_SUB[c], :])

    @pl.when(j == 0)
    def _init():
        acc_scr[...] = jnp.zeros((tm, D), F32)

    up(0)
    for c in range(nsub):
        if c + 1 < nsub:
            up(c + 1)
        down(c)

    @pl.when(j == pl.num_programs(1) - 1)
    def _finish():
        g2 = mod_ref[...][:, 5 * D:6 * D]
        x2 = x1_ref[...] + g2 * acc_scr[...]
        if final:
            x2 = x2 * lax.rsqrt(jnp.mean(x2 * x2, axis=-1, keepdims=True) + RMS_EPS) * gf_ref[...]
        o_ref[...] = x2


def _ffn(h2, x1, mods, prm, l, lat, tokens_per_mod, seq_len, final_g, tm=512):
    t = h2.shape[0]
    tf = FFN_TF
    halo = seq_len > tm
    final = final_g is not None
    nj = D_FF // tf
    tiles_per_mod = tokens_per_mod // tm
    mod_map = (lambda i, j: (l, 1 + i // tiles_per_mod, 0, 0)) if lat else (lambda i, j: (l, 0, 0, 0))
    row = lambda i, j: (i, 0)
    in_specs = [pl.BlockSpec((tm, D), row)]
    args = [h2]
    if halo:
        r16 = tm // 16
        in_specs += [pl.BlockSpec((16, D), lambda i, j: (jnp.maximum(i * r16 - 1, 0), 0)),
                     pl.BlockSpec((16, D), lambda i, j: (jnp.minimum((i + 1) * r16, t // 16 - 1), 0))]
        args += [h2, h2]
    in_specs += [pl.BlockSpec((tm, D), row), pl.BlockSpec((None, None, 1, 6 * D), mod_map),
                 pl.BlockSpec((None, D, tf), lambda i, j: (l, 0, j)),
                 pl.BlockSpec((None, D, tf), lambda i, j: (l, 0, j + nj)),
                 pl.BlockSpec((None, 3, tf), lambda i, j: (l, 0, j)),
                 pl.BlockSpec((None, 3, tf), lambda i, j: (l, 0, j + nj)),
                 pl.BlockSpec((None, tf, D), lambda i, j: (l, j, 0))]
    args += [x1, mods, prm["w_up"], prm["w_up"], prm["ffn_conv"], prm["ffn_conv"], prm["w_down"]]
    if final:
        in_specs.append(pl.BlockSpec((1, D), lambda i, j: (0, 0)))
        args.append(final_g)
    urows = tm + 32 if halo else (tm // seq_len) * (seq_len + 16)
    scratch = [pltpu.VMEM((tm, D), F32)]
    scratch += [pltpu.VMEM((urows, w), F32) for w in FFN_SUB] * 2
    if halo:
        scratch.append(pltpu.VMEM((tm + 32, D), BF16))
    return pl.pallas_call(
        functools.partial(_ffn_kernel, halo=halo, tm=tm, seq_len=seq_len, final=final),
        grid=(t // tm, nj),
        in_specs=in_specs,
        out_specs=pl.BlockSpec((tm, D), row),
        out_shape=jax.ShapeDtypeStruct((t, D), F32),
        scratch_shapes=scratch,
        compiler_params=_cparams(("parallel", "arbitrary")),
        name="ffn_lat" if lat else "ffn_ctx",
    )(*args)


def _rope_tables(seq_len):
    rows = seq_len // GRID_W
    row = np.repeat(np.arange(rows, dtype=np.float32), GRID_W)
    col = np.tile(np.arange(GRID_W, dtype=np.float32), rows)
    freqs = (np.float32(ROPE_BASE) ** (-np.arange(ROPE_FREQS, dtype=np.float32) / np.float32(ROPE_FREQS)))
    freqs = freqs.astype(np.float32)
    ang = np.stack([row[:, None] * freqs, col[:, None] * freqs], axis=1).astype(np.float32)
    cos, sin = np.cos(ang).astype(np.float32), np.sin(ang).astype(np.float32)
    zero = np.zeros_like(sin)
    c64 = np.stack([cos, cos], axis=2).reshape(seq_len, HEAD_DIM)
    sneg64 = np.stack([-sin, zero], axis=2).reshape(seq_len, HEAD_DIM)
    spos64 = np.stack([zero, sin], axis=2).reshape(seq_len, HEAD_DIM)
    return tuple(jnp.asarray(np.tile(tb, (1, 2))) for tb in (c64, sneg64, spos64))


def _prep_params(w_in, w_out, norm_mix, norm_ffn, sc_conv, ssm_lam_re, ssm_lam_im, ssm_log_dt, ssm_b_re,
                 ssm_b_im, ssm_c_re, ssm_c_im, ssm_d, ssm_w_glu, ffn_w_up, ffn_conv, ffn_w_down):
    b_slab = lambda b: jnp.transpose(b, (0, 1, 4, 2, 3)).reshape(DEPTH, 2, SSM_CH, SSM_N)
    c_slab = lambda c: jnp.transpose(c, (0, 1, 3, 2, 4)).reshape(DEPTH, 2, SSM_CH, SSM_N)
    return {
        "w_in": w_in,
        "norm_mix": norm_mix.reshape(DEPTH, 1, D),
        "norm_ffn": norm_ffn.reshape(DEPTH, 1, D),
        "sc_conv": jnp.transpose(sc_conv, (0, 2, 1)),
        "ssm_d": ssm_d.reshape(DEPTH, 1, SSM_DIM),
        "w_glu": ssm_w_glu,
        "w_out": w_out,
        "w_up": ffn_w_up.astype(BF16),
        "ffn_conv": jnp.transpose(ffn_conv, (0, 2, 1)),
        "w_down": ffn_w_down.astype(BF16),
        "lam_re": ssm_lam_re.reshape(DEPTH, 2, 1, SSM_N),
        "lam_im": ssm_lam_im.reshape(DEPTH, 2, 1, SSM_N),
        "log_dt": jnp.repeat(ssm_log_dt, SSM_STATE, axis=-1).reshape(DEPTH, 2, 1, SSM_N),
        "b_re": b_slab(ssm_b_re),
        "b_im": b_slab(ssm_b_im),
        "c_re": c_slab(ssm_c_re),
        "c_im": c_slab(ssm_c_im),
    }


def kernel(x_prompt, x_sample, cache_k, cache_v, state_ssm_re, state_ssm_im, c, c_ctx, norm_mix, norm_ffn, norm_final, w_ada, b_ada, w_in, w_out, attn_sink, sc_conv, ssm_lam_re, ssm_lam_im, ssm_log_dt, ssm_b_re, ssm_b_im, ssm_c_re, ssm_c_im, ssm_d, ssm_w_glu, ffn_w_up, ffn_conv, ffn_w_down):
    batch, seq = x_prompt.shape[0], x_prompt.shape[1]
    dec_batch, dec_seq = x_sample.shape[0], x_sample.shape[1]
    assert batch == SCAN_ROWS and seq == SCAN_LEN
    assert dec_batch * (dec_seq // SCAN_LEN) == SCAN_ROWS and dec_batch == 2

    cs = jnp.concatenate([c_ctx[None, :], c, jnp.zeros((8 - 1 - dec_batch, D), F32)], axis=0)
    mods = _adaln(cs, w_ada, b_ada).reshape(DEPTH, 8, 1, 6 * D)
    rope_tabs = _rope_tables(dec_seq)
    gfin = norm_final.reshape(1, D)
    prm = _prep_params(w_in, w_out, norm_mix, norm_ffn, sc_conv, ssm_lam_re, ssm_lam_im, ssm_log_dt, ssm_b_re,
                       ssm_b_im, ssm_c_re, ssm_c_im, ssm_d, ssm_w_glu, ffn_w_up, ffn_conv, ffn_w_down)
    s0_all = jnp.transpose(jnp.concatenate([state_ssm_re.reshape(dec_batch, DEPTH, 2, SSM_N),
                                            state_ssm_im.reshape(dec_batch, DEPTH, 2, SSM_N)], axis=-1),
                           (1, 2, 0, 3))

    xp = x_prompt.reshape(batch * seq, D)
    xs = x_sample.reshape(dec_batch * dec_seq, D)
    kv_out, fin_out = [], []
    for l in range(DEPTH):
        last = gfin if l == DEPTH - 1 else None

        q, kd, vd, kv, g3, su = _inproj(xp, mods, prm, l, None, batch * seq, seq)
        attn = _attn_ctx(attn_sink, l, q, kd, vd, seq)
        yf, yb, fin = _scan(prm, l, su, True)
        x1, h2 = _mix(xp, mods, attn, g3, su, yf, yb, prm, l, False, batch * seq, seq)
        xp = _ffn(h2, x1, mods, prm, l, False, batch * seq, seq, last)
        kv_out.append(kv)
        fin_out.append(fin)

        q, kd, vd, g3, su = _inproj(xs, mods, prm, l, rope_tabs, dec_seq, dec_seq)
        attn = _attn_lat(attn_sink, l, q, kd, vd, cache_k, cache_v, dec_batch, dec_seq)
        (ez,) = _scan(prm, l, su, False)
        yf, yb, _ = _scan(prm, l, su, True, s0=s0_all[l], ez=ez)
        x1, h2 = _mix(xs, mods, attn, g3, su, yf, yb, prm, l, True, dec_seq, dec_seq)
        xs = _ffn(h2, x1, mods, prm, l, True, dec_seq, dec_seq, last)

    kv_all = jnp.stack(kv_out, axis=0).reshape(DEPTH, batch, seq, 2, KV_HEADS, HEAD_DIM)
    kv_all = jnp.transpose(kv_all, (3, 1, 0, 2, 4, 5))
    fin_all = jnp.stack(fin_out, axis=0).reshape(DEPTH, 2, batch, 2, SSM_GROUPS, SSM_STATE)
    fin_all = jnp.transpose(fin_all, (3, 2, 0, 1, 4, 5))
    return (xp.reshape(batch, seq, D), xs.reshape(dec_batch, dec_seq, D),
            kv_all[0], kv_all[1], fin_all[0], fin_all[1])
```

```python
import functools
import math

import jax
import jax.numpy as jnp
import numpy as np
from jax import lax
from jax.experimental import pallas as pl
from jax.experimental.pallas import tpu as pltpu

F32 = jnp.float32
BF16 = jnp.bfloat16

D = 1024
DEPTH = 2
GRID_W = 64
ATTN_DIM = 512
SC_DIM = 256
SSM_DIM = 256
HEAD_DIM = 64
N_HEADS = 8
KV_HEADS = 2
KV_DIM = 128
WINDOW = 128
Q_BLOCK = 128
ROPE_BASE = 10000.0
ROPE_FREQS = 16
SSM_CH = 16
SSM_GROUPS = 16
SSM_STATE = 64
SSM_N = SSM_GROUPS * SSM_STATE
IN_DIM = ATTN_DIM + 2 * KV_DIM + 3 * SC_DIM + SSM_DIM
D_FF = 2816
LOG2E = math.log2(math.e)
Q_SCALE = LOG2E / math.sqrt(HEAD_DIM)
RMS_EPS = 1e-6
NEG_BIG = -1e30

LANES = 128
SCAN_ROWS = 16
SCAN_LEN = 256
SCAN_TT = 64
SCAN_LW = 256
SCAN_MM_ROWS = 1024
FFN_TF = 1408
FFN_SUB = (768, 512, 128)
assert D_FF % FFN_TF == 0 and sum(FFN_SUB) == FFN_TF
VMEM_LIMIT = 56 * 1024 * 1024


def _cparams(sem):
    return pltpu.CompilerParams(dimension_semantics=sem, vmem_limit_bytes=VMEM_LIMIT)


def _sigmoid(x):
    return 1.0 / (1.0 + jnp.exp(-x))


def _rms_mod(x, g, scale, shift):
    y = x * lax.rsqrt(jnp.mean(x * x, axis=-1, keepdims=True) + RMS_EPS) * g
    return y * (1.0 + scale) + shift


def _split_bf16(v):
    hi = v.astype(BF16)
    lo = (v - hi.astype(F32)).astype(BF16)
    return hi, lo


def _dot(a, b):
    return jnp.dot(a, b, preferred_element_type=F32)


def _dot_t(a, b):
    return lax.dot_general(a, b, (((1,), (1,)), ((), ())), preferred_element_type=F32)


def _layer_spec(shape, l):
    zeros = (0,) * len(shape)
    return pl.BlockSpec((None,) + tuple(shape), lambda *_: (l,) + zeros)


def _half_tiles(t):
    lo = lax.broadcasted_iota(jnp.int32, t.shape, 1) < HEAD_DIM
    tr = pltpu.roll(t, HEAD_DIM, 1)
    return [jnp.where(lo, t, 0.0).astype(BF16), jnp.where(lo, 0.0, tr).astype(BF16),
            jnp.where(lo, tr, 0.0).astype(BF16), jnp.where(lo, 0.0, t).astype(BF16)]


def _adaln_kernel(c_ref, w_ref, b_ref, o_ref):
    c = c_ref[...]
    s = c * _sigmoid(c)
    s_hi, s_lo = _split_bf16(s)
    w_hi, w_lo = _split_bf16(w_ref[...])
    o_ref[...] = _dot(s_hi, w_hi) + _dot(s_lo, w_hi) + _dot(s_hi, w_lo) + b_ref[...]


def _adaln(cs, w_ada, b_ada):
    tn = 2048
    return pl.pallas_call(
        _adaln_kernel,
        grid=(DEPTH, 6 * D // tn),
        in_specs=[
            pl.BlockSpec((8, D), lambda l, j: (0, 0)),
            pl.BlockSpec((None, D, tn), lambda l, j: (l, 0, j)),
            pl.BlockSpec((None, 1, tn), lambda l, j: (l, 0, j)),
        ],
        out_specs=pl.BlockSpec((None, 8, tn), lambda l, j: (l, 0, j)),
        out_shape=jax.ShapeDtypeStruct((DEPTH, 8, 6 * D), F32),
        compiler_params=_cparams(("parallel", "parallel")),
        name="adaln",
    )(cs, w_ada, b_ada.reshape(DEPTH, 1, 6 * D))


def _inproj_kernel(x_ref, mod_ref, g_ref, w_ref, *rest, rope):
    if rope:
        cos_ref, sneg_ref, spos_ref, q_ref, kd_ref, vd_ref, g3_ref, su_ref, wbf_scr = rest
    else:
        q_ref, kd_ref, vd_ref, kv_ref, g3_ref, su_ref, wbf_scr = rest

    @pl.when(pl.program_id(0) == 0)
    def _cast_weights():
        wbf_scr[...] = w_ref[...].astype(BF16)

    m = mod_ref[...]
    h = _rms_mod(x_ref[...], g_ref[...], m[:, D:2 * D], m[:, 0:D]).astype(BF16)
    acc = _dot(h, wbf_scr[...])
    k = acc[:, 512:640]
    v = acc[:, 640:768]
    if rope:
        c, sn, sp = cos_ref[...], sneg_ref[...], spos_ref[...]

        def rot(t):
            return t * c + pltpu.roll(t, LANES - ROPE_FREQS, 1) * sn + pltpu.roll(t, ROPE_FREQS, 1) * sp

        for p in range(4):
            q_ref[:, LANES * p:LANES * (p + 1)] = (rot(acc[:, LANES * p:LANES * (p + 1)]) * Q_SCALE).astype(BF16)
        k = rot(k)
    else:
        q_ref[...] = (acc[:, 0:512] * Q_SCALE).astype(BF16)
        kv_ref[...] = acc[:, 512:768]
    for ref, t in ((kd_ref, k), (vd_ref, v)):
        for p, tile in enumerate(_half_tiles(t)):
            ref[:, LANES * p:LANES * (p + 1)] = tile
    g3_ref[...] = acc[:, 768:1536].astype(BF16)
    for r in range(acc.shape[0] // SCAN_LEN):
        su_ref[:, r * SSM_DIM:(r + 1) * SSM_DIM] = acc[r * SCAN_LEN:(r + 1) * SCAN_LEN, 1536:1792]


def _inproj(x, mods, prm, l, rope_tabs, tokens_per_mod, seq_len, tm=1024):
    t = x.shape[0]
    rope = rope_tabs is not None
    tiles_per_mod = tokens_per_mod // tm
    tiles_per_seq = seq_len // tm
    if rope:
        mod_map = lambda i: (l, 1 + i // tiles_per_mod, 0, 0)
    else:
        mod_map = lambda i: (l, 0, 0, 0)
    row = lambda i: (i, 0)
    in_specs = [
        pl.BlockSpec((tm, D), row),
        pl.BlockSpec((None, None, 1, 6 * D), mod_map),
        _layer_spec((1, D), l),
        _layer_spec((D, IN_DIM), l),
    ]
    args = [x, mods, prm["norm_mix"], prm["w_in"]]
    if rope:
        in_specs += [pl.BlockSpec((tm, LANES), lambda i: (i % tiles_per_seq, 0))] * 3
        args += list(rope_tabs)
    out_shape = [jax.ShapeDtypeStruct((t, 512), BF16)] * 3
    out_specs = [pl.BlockSpec((tm, 512), row)] * 3
    if not rope:
        out_shape.append(jax.ShapeDtypeStruct((t, 256), F32))
        out_specs.append(pl.BlockSpec((tm, 256), row))
    out_shape += [jax.ShapeDtypeStruct((t, 768), BF16),
                  jax.ShapeDtypeStruct((SCAN_LEN, t // SCAN_LEN * SSM_DIM), F32)]
    out_specs += [pl.BlockSpec((tm, 768), row),
                  pl.BlockSpec((SCAN_LEN, tm // SCAN_LEN * SSM_DIM), lambda i: (0, i))]
    return pl.pallas_call(
        functools.partial(_inproj_kernel, rope=rope),
        grid=(t // tm,),
        in_specs=in_specs,
        out_specs=out_specs,
        out_shape=out_shape,
        scratch_shapes=[pltpu.VMEM((D, IN_DIM), BF16)],
        compiler_params=_cparams(("arbitrary",)),
        name="inproj_lat" if rope else "inproj_ctx",
    )(*args)


def _attention(sink_ref, l, q_ref, o_ref, keys, vals, bias, nq):
    top = lax.broadcasted_iota(jnp.int32, (2 * nq, 1), 0) < nq
    scores = []
    for kvh in range(KV_HEADS):
        q2 = jnp.concatenate([q_ref[:, LANES * (2 * kvh + pp):LANES * (2 * kvh + pp + 1)] for pp in range(2)],
                             axis=0)
        scores.append([_dot_t(q2, keys[kvh][half]) for half in range(2)])
    for kvh in range(KV_HEADS):
        acc = None
        for half in range(2):
            sk = jnp.where(top, sink_ref[l, 4 * kvh + half], sink_ref[l, 4 * kvh + 2 + half]) * LOG2E
            s = scores[kvh][half]
            if bias is not None:
                nb_ = bias.shape[1]
                s = jnp.concatenate([s[:, 0:nb_] + bias, s[:, nb_:]], axis=1)
            m = jnp.maximum(jnp.max(s, axis=-1, keepdims=True), sk)
            e = jnp.exp2(s - m)
            den = jnp.sum(e, axis=-1, keepdims=True) + jnp.exp2(sk - m)
            o = _dot(e.astype(BF16), vals[kvh][half]) / den
            acc = o if acc is None else acc + o
        o_ref[:, LANES * 2 * kvh:LANES * (2 * kvh + 1)] = acc[0:nq].astype(BF16)
        o_ref[:, LANES * (2 * kvh + 1):LANES * (2 * kvh + 2)] = acc[nq:2 * nq].astype(BF16)


def _kv_tiles(kvh):
    return [slice(LANES * (2 * kvh + h), LANES * (2 * kvh + h + 1)) for h in range(2)]


def _attn_ctx_kernel(sink_ref, q_ref, kd_ref, vd_ref, o_ref, *, l):
    keys = [[kd_ref[:, t] for t in _kv_tiles(kvh)] for kvh in range(KV_HEADS)]
    vals = [[vd_ref[:, t] for t in _kv_tiles(kvh)] for kvh in range(KV_HEADS)]
    _attention(sink_ref, l, q_ref, o_ref, keys, vals, None, q_ref.shape[0])


def _attn_ctx(sink, l, q, kd, vd, seq_len):
    t = q.shape[0]
    row = lambda b: (b, 0)
    return pl.pallas_call(
        functools.partial(_attn_ctx_kernel, l=l),
        grid=(t // seq_len,),
        in_specs=[
            pl.BlockSpec(memory_space=pltpu.SMEM),
            pl.BlockSpec((seq_len, 512), row),
            pl.BlockSpec((seq_len, 512), row),
            pl.BlockSpec((seq_len, 512), row),
        ],
        out_specs=pl.BlockSpec((seq_len, 512), row),
        out_shape=jax.ShapeDtypeStruct((t, 512), BF16),
        compiler_params=_cparams(("parallel",)),
        name="attn_ctx",
    )(sink, q, kd, vd)


def _attn_lat_kernel(sink_ref, q_ref, kp_ref, kc_ref, kn_ref, vp_ref, vc_ref, vn_ref, ck_ref, cv_ref, o_ref,
                     ckt_scr, cvt_scr, *, l):
    i = pl.program_id(1)
    nb = pl.num_programs(1)

    @pl.when(i == 0)
    def _context_tiles():
        for p, (kt, vt) in enumerate(zip(_half_tiles(ck_ref[...]), _half_tiles(cv_ref[...]))):
            ckt_scr[p] = kt
            cvt_scr[p] = vt

    r = lax.broadcasted_iota(jnp.int32, (Q_BLOCK, Q_BLOCK), 0)
    j = lax.broadcasted_iota(jnp.int32, (Q_BLOCK, Q_BLOCK), 1)
    bias = jnp.concatenate([
        jnp.where(jnp.logical_and(j >= r, i > 0), 0.0, NEG_BIG),
        jnp.where(jnp.logical_and(j <= r, i < nb - 1), 0.0, NEG_BIG)], axis=1)
    bias = jnp.concatenate([bias, bias], axis=0)
    def gather(refs, ctx_scr, kvh, h):
        t = _kv_tiles(kvh)[h]
        return jnp.concatenate([ref[:, t] for ref in refs] + [ctx_scr[2 * kvh + h]], axis=0)

    keys = [[gather((kp_ref, kn_ref, kc_ref), ckt_scr, kvh, h) for h in range(2)] for kvh in range(KV_HEADS)]
    vals = [[gather((vp_ref, vn_ref, vc_ref), cvt_scr, kvh, h) for h in range(2)] for kvh in range(KV_HEADS)]
    _attention(sink_ref, l, q_ref, o_ref, keys, vals, bias, Q_BLOCK)


def _attn_lat(sink, l, q, kd, vd, cache_k, cache_v, batch, seq_len):
    nb = seq_len // Q_BLOCK
    past = cache_k.shape[2]
    cur = lambda b, i: (b * nb + i, 0)
    prev = lambda b, i: (b * nb + jnp.maximum(i - 1, 0), 0)
    nxt = lambda b, i: (b * nb + jnp.minimum(i + 1, nb - 1), 0)
    kvspec = lambda f: pl.BlockSpec((Q_BLOCK, 512), f)
    cspec = pl.BlockSpec((None, None, past, KV_DIM), lambda b, i: (b, l, 0, 0))
    ck = cache_k.reshape(batch, DEPTH, past, KV_DIM)
    cv = cache_v.reshape(batch, DEPTH, past, KV_DIM)
    return pl.pallas_call(
        functools.partial(_attn_lat_kernel, l=l),
        grid=(batch, nb),
        in_specs=[
            pl.BlockSpec(memory_space=pltpu.SMEM),
            pl.BlockSpec((Q_BLOCK, 512), cur),
            kvspec(prev), kvspec(cur), kvspec(nxt),
            kvspec(prev), kvspec(cur), kvspec(nxt),
            cspec, cspec,
        ],
        out_specs=pl.BlockSpec((Q_BLOCK, 512), cur),
        out_shape=jax.ShapeDtypeStruct((batch * seq_len, 512), BF16),
        scratch_shapes=[pltpu.VMEM((2 * KV_HEADS, past, LANES), BF16)] * 2,
        compiler_params=_cparams(("parallel", "arbitrary")),
        name="attn_lat",
    )(sink, q, kd, kd, kd, vd, vd, vd, ck, cv)


def _scan_kernel(lre_ref, lim_ref, ldt_ref, bre_ref, bim_ref, cre_ref, cim_ref, *rest, emit_y, chain):
    rest = list(rest)
    if chain:
        s0_ref, ez_ref = rest[:2]
        rest = rest[2:]
    suf_ref, sub_ref = rest[:2]
    rest = rest[2:]
    if emit_y:
        y_refs = rest[:2]
        rest = rest[2:]
    fin_ref, a_scr, bb_scr, h_scr, bu_scr, u_scr = rest[:6]
    if emit_y:
        y_scr, ct_scr = rest[6:8]
    i = pl.program_id(0)
    tt = SCAN_TT
    n = SSM_N

    @pl.when(i == 0)
    def _prologue():
        row_g = lax.shift_right_logical(lax.broadcasted_iota(jnp.int32, (SSM_DIM, n), 0), 4)
        col_g = lax.shift_right_logical(lax.broadcasted_iota(jnp.int32, (SSM_DIM, n), 1), 6)
        own = row_g == col_g

        def blockdiag(ref, d):
            return jnp.where(own, jnp.concatenate([ref[d]] * SSM_GROUPS, axis=0), 0.0)

        for d in range(2):
            lr, li = lre_ref[d], lim_ref[d]
            dt = jnp.exp(ldt_ref[d])
            mag = jnp.exp(lr * dt)
            ar, ai = mag * jnp.cos(li * dt), mag * jnp.sin(li * dt)
            den = lr * lr + li * li
            fr = ((ar - 1.0) * lr + ai * li) / den
            fi = (ai * lr - (ar - 1.0) * li) / den
            bre, bim = blockdiag(bre_ref, d), blockdiag(bim_ref, d)
            bb_scr[d, :, 0:n] = (fr * bre - fi * bim).astype(BF16)
            bb_scr[d, :, n:2 * n] = (fr * bim + fi * bre).astype(BF16)
            if emit_y:
                ct_scr[d, :, 0:n] = blockdiag(cre_ref, d).astype(BF16)
                ct_scr[d, :, n:2 * n] = (-blockdiag(cim_ref, d)).astype(BF16)
            a_scr[d, 0] = jnp.broadcast_to(ar, (8, n))
            a_scr[d, 1] = jnp.broadcast_to(ai, (8, n))
            if chain:
                pr, pi_ = ar, ai
                for _ in range(8):
                    pr, pi_ = pr * pr - pi_ * pi_, 2.0 * pr * pi_
                chunks = SCAN_ROWS // 2
                for b in range(2):
                    hr = s0_ref[d, b:b + 1, 0:n]
                    hi = s0_ref[d, b:b + 1, n:2 * n]
                    order = range(chunks) if d == 0 else range(chunks - 1, -1, -1)
                    for c in order:
                        rw = b * chunks + c
                        h_scr[d, rw:rw + 1, 0:n] = hr
                        h_scr[d, rw:rw + 1, n:2 * n] = hi
                        er = ez_ref[d, rw:rw + 1, 0:n]
                        ei = ez_ref[d, rw:rw + 1, n:2 * n]
                        hr, hi = pr * hr - pi_ * hi + er, pr * hi + pi_ * hr + ei
            else:
                h_scr[d] = jnp.zeros((SCAN_ROWS, 2 * n), F32)

    ntile = n // LANES
    group = SCAN_LW // LANES

    def project_in(d):
        su_ref = suf_ref if d == 0 else sub_ref
        for r in range(SCAN_ROWS):
            for sl in range(SSM_DIM // LANES):
                col = r * SSM_DIM + sl * LANES
                u_scr[d, sl, pl.ds(r, tt, stride=SCAN_ROWS), :] = su_ref[:, col:col + LANES]
        for r0 in range(0, tt * SCAN_ROWS, SCAN_MM_ROWS):
            rows = slice(r0, r0 + SCAN_MM_ROWS)
            u = jnp.concatenate([u_scr[d, sl, rows, :] for sl in range(SSM_DIM // LANES)], axis=1).astype(BF16)
            bu_scr[d, rows, :] = _dot(u, bb_scr[d])

    def recur(d):
        for c0 in range(0, ntile, group):
            lre = slice(c0 * LANES, (c0 + group) * LANES)
            lim = slice(n + c0 * LANES, n + (c0 + group) * LANES)
            ar, ai = a_scr[d, 0, :, lre], a_scr[d, 1, :, lre]
            halves = [slice(0, 8), slice(8, 16)]
            hr = [h_scr[d, hs, lre] for hs in halves]
            hi = [h_scr[d, hs, lim] for hs in halves]
            steps = range(tt) if d == 0 else range(tt - 1, -1, -1)
            for s in steps:
                for k in range(2):
                    rows = slice(s * SCAN_ROWS + 8 * k, s * SCAN_ROWS + 8 * k + 8)
                    br, bi = bu_scr[d, rows, lre], bu_scr[d, rows, lim]
                    hr[k], hi[k] = ar * hr[k] - ai * hi[k] + br, ar * hi[k] + ai * hr[k] + bi
                    if emit_y:
                        bu_scr[d, rows, lre] = hr[k]
                        bu_scr[d, rows, lim] = hi[k]
            for k, hs in enumerate(halves):
                h_scr[d, hs, lre] = hr[k]
                h_scr[d, hs, lim] = hi[k]

    def project_out(d):
        for r0 in range(0, tt * SCAN_ROWS, SCAN_MM_ROWS):
            rows = slice(r0, r0 + SCAN_MM_ROWS)
            y = _dot_t(bu_scr[d, rows, :].astype(BF16), ct_scr[d])
            for sl in range(SSM_DIM // LANES):
                y_scr[d, sl, rows, :] = y[:, sl * LANES:(sl + 1) * LANES]
        for r in range(SCAN_ROWS):
            for sl in range(SSM_DIM // LANES):
                col = r * SSM_DIM + sl * LANES
                y_refs[d][:, col:col + LANES] = y_scr[d, sl, pl.ds(r, tt, stride=SCAN_ROWS), :].astype(BF16)

    project_in(0)
    project_in(1)
    recur(0)
    if emit_y:
        project_out(0)
    recur(1)
    if emit_y:
        project_out(1)

    @pl.when(i == pl.num_programs(0) - 1)
    def _final():
        fin_ref[...] = h_scr[...]


def _scan(prm, l, su_tm, emit_y, s0=None, ez=None):
    chain = s0 is not None
    nt = SCAN_LEN // SCAN_TT
    n = SSM_N
    full3 = lambda shape: pl.BlockSpec(shape, lambda i: (0, 0, 0))
    in_specs = [_layer_spec((2, 1, n), l)] * 3 + [_layer_spec((2, SSM_CH, n), l)] * 4
    args = [prm["lam_re"], prm["lam_im"], prm["log_dt"], prm["b_re"], prm["b_im"], prm["c_re"], prm["c_im"]]
    if chain:
        in_specs += [full3((2, 2, 2 * n)), full3((2, SCAN_ROWS, 2 * n))]
        args += [s0, ez]
    tblk = (SCAN_TT, SCAN_ROWS * SSM_DIM)
    fwd = lambda i: (i, 0)
    bwd = lambda i: (nt - 1 - i, 0)
    in_specs += [pl.BlockSpec(tblk, fwd), pl.BlockSpec(tblk, bwd)]
    args += [su_tm, su_tm]
    out_shape, out_specs = [], []
    scratch = [
        pltpu.VMEM((2, 2, 8, n), F32),
        pltpu.VMEM((2, SSM_DIM, 2 * n), BF16),
        pltpu.VMEM((2, SCAN_ROWS, 2 * n), F32),
        pltpu.VMEM((2, SCAN_ROWS * SCAN_TT, 2 * n), F32),
        pltpu.VMEM((2, SSM_DIM // LANES, SCAN_ROWS * SCAN_TT, LANES), F32),
    ]
    if emit_y:
        yshape = jax.ShapeDtypeStruct((SCAN_LEN, SCAN_ROWS * SSM_DIM), BF16)
        out_shape += [yshape, yshape]
        out_specs += [pl.BlockSpec(tblk, fwd), pl.BlockSpec(tblk, bwd)]
        scratch.append(pltpu.VMEM((2, SSM_DIM // LANES, SCAN_ROWS * SCAN_TT, LANES), F32))
        scratch.append(pltpu.VMEM((2, SSM_DIM, 2 * n), BF16))
    out_shape.append(jax.ShapeDtypeStruct((2, SCAN_ROWS, 2 * n), F32))
    out_specs.append(full3((2, SCAN_ROWS, 2 * n)))
    return pl.pallas_call(
        functools.partial(_scan_kernel, emit_y=emit_y, chain=chain),
        grid=(nt,),
        in_specs=in_specs,
        out_specs=out_specs,
        out_shape=out_shape,
        scratch_shapes=scratch,
        compiler_params=_cparams(("arbitrary",)),
        name="scan_chain" if chain else ("scan_y" if emit_y else "scan_state"),
    )(*args)


def _shifted(scr_ref, off, rows, pos, seq_len, w):
    mid = scr_ref[off:off + rows, :]
    up = jnp.where(pos != 0, scr_ref[off - 1:off - 1 + rows, :], 0.0)
    dn = jnp.where(pos != seq_len - 1, scr_ref[off + 1:off + 1 + rows, :], 0.0)
    return w[0:1, :] * up + w[1:2, :] * mid + w[2:3, :] * dn


def _mix_kernel(x_ref, mod_ref, attn_ref, g3_ref, *rest, halo, tm, seq_len):
    rest = list(rest)
    if halo:
        gp_ref, gn_ref = rest[:2]
        rest = rest[2:]
    (su_ref, yf_ref, yb_ref, wc_ref, dsk_ref, wglu_ref, wo_ref, gn2_ref,
     x1_ref, h2_ref, z_scr, wglu_scr, wo_scr) = rest
    i = pl.program_id(0)

    @pl.when(i == 0)
    def _cast_weights():
        wglu_scr[...] = wglu_ref[...].astype(BF16)
        wo_scr[...] = wo_ref[...].astype(BF16)

    def rows_of(ref):
        pieces = [ref[:, r * SSM_DIM:(r + 1) * SSM_DIM].astype(F32) for r in range(tm // SCAN_LEN)]
        return pieces[0] if len(pieces) == 1 else jnp.concatenate(pieces, axis=0)

    m = mod_ref[...]
    g1, sh2, sc2 = m[:, 2 * D:3 * D], m[:, 3 * D:4 * D], m[:, 4 * D:5 * D]
    pos = (i * tm + lax.broadcasted_iota(jnp.int32, (tm, 1), 0)) % seq_len

    def gate_prod(ref):
        return ref[:, SC_DIM:2 * SC_DIM].astype(F32) * ref[:, 2 * SC_DIM:3 * SC_DIM].astype(F32)

    gb = g3_ref[:, 0:SC_DIM].astype(F32)
    z_scr[16:16 + tm, :] = gate_prod(g3_ref)
    if halo:
        z_scr[0:16, :] = gate_prod(gp_ref)
        z_scr[16 + tm:32 + tm, :] = gate_prod(gn_ref)
    else:
        z_scr[0:16, :] = jnp.zeros((16, SC_DIM), F32)
        z_scr[16 + tm:32 + tm, :] = jnp.zeros((16, SC_DIM), F32)
    conv = gb * _shifted(z_scr, 16, tm, pos, seq_len, wc_ref[...])

    y = dsk_ref[...] * rows_of(su_ref) + rows_of(yf_ref) + rows_of(yb_ref)
    zz = 0.5 * y * (1.0 + jnp.tanh(math.sqrt(2.0 / math.pi) * (y + 0.044715 * (y * y * y))))
    ssm = zz * _sigmoid(_dot(zz.astype(BF16), wglu_scr[...]))

    mix = (_dot(attn_ref[...], wo_scr[0:ATTN_DIM, :])
           + _dot(conv.astype(BF16), wo_scr[ATTN_DIM:ATTN_DIM + SC_DIM, :])
           + _dot(ssm.astype(BF16), wo_scr[ATTN_DIM + SC_DIM:, :]))
    x1 = x_ref[...] + g1 * mix
    x1_ref[...] = x1
    h2_ref[...] = _rms_mod(x1, gn2_ref[...], sc2, sh2).astype(BF16)


def _mix(x, mods, attn, g3, su, yf, yb, prm, l, lat, tokens_per_mod, seq_len, tm=1024):
    t = x.shape[0]
    halo = seq_len > tm
    tiles_per_mod = tokens_per_mod // tm
    mod_map = (lambda i: (l, 1 + i // tiles_per_mod, 0, 0)) if lat else (lambda i: (l, 0, 0, 0))
    row = lambda i: (i, 0)
    in_specs = [pl.BlockSpec((tm, D), row), pl.BlockSpec((None, None, 1, 6 * D), mod_map),
                pl.BlockSpec((tm, 512), row), pl.BlockSpec((tm, 768), row)]
    args = [x, mods, attn, g3]
    if halo:
        r16 = tm // 16
        in_specs += [pl.BlockSpec((16, 768), lambda i: (jnp.maximum(i * r16 - 1, 0), 0)),
                     pl.BlockSpec((16, 768), lambda i: (jnp.minimum((i + 1) * r16, t // 16 - 1), 0))]
        args += [g3, g3]
    assert tm % SCAN_LEN == 0
    in_specs += [pl.BlockSpec((SCAN_LEN, tm // SCAN_LEN * SSM_DIM), lambda i: (0, i))] * 3
    args += [su, yf, yb]
    in_specs += [_layer_spec((3, SC_DIM), l), _layer_spec((1, SSM_DIM), l), _layer_spec((SSM_DIM, SSM_DIM), l),
                 _layer_spec((D, D), l), _layer_spec((1, D), l)]
    args += [prm["sc_conv"], prm["ssm_d"], prm["w_glu"], prm["w_out"], prm["norm_ffn"]]
    return pl.pallas_call(
        functools.partial(_mix_kernel, halo=halo, tm=tm, seq_len=seq_len),
        grid=(t // tm,),
        in_specs=in_specs,
        out_specs=[pl.BlockSpec((tm, D), row), pl.BlockSpec((tm, D), row)],
        out_shape=[jax.ShapeDtypeStruct((t, D), F32), jax.ShapeDtypeStruct((t, D), BF16)],
        scratch_shapes=[pltpu.VMEM((tm + 32, SC_DIM), F32), pltpu.VMEM((SSM_DIM, SSM_DIM), BF16),
                        pltpu.VMEM((D, D), BF16)],
        compiler_params=_cparams(("arbitrary",)),
        name="mix_lat" if lat else "mix_ctx",
    )(*args)


def _ffn_kernel(h_ref, *rest, halo, tm, seq_len, final):
    rest = list(rest)
    if halo:
        hp_ref, hn_ref = rest[:2]
        rest = rest[2:]
    x1_ref, mod_ref, wa_ref, wg_ref, ca_ref, cg_ref, wd_ref = rest[:7]
    rest = rest[7:]
    if final:
        gf_ref = rest[0]
        rest = rest[1:]
    o_ref, acc_scr = rest[:2]
    rest = rest[2:]
    nsub = len(FFN_SUB)
    ua_scrs, ug_scrs = rest[:nsub], rest[nsub:2 * nsub]
    i = pl.program_id(0)
    j = pl.program_id(1)

    if halo:
        hcat_scr = rest[2 * nsub]

        @pl.when(j == 0)
        def _stage():
            hcat_scr[0:16, :] = hp_ref[...]
            hcat_scr[16:16 + tm, :] = h_ref[...]
            hcat_scr[16 + tm:32 + tm, :] = hn_ref[...]

        hh = hcat_scr[...]
        pad, seg_len, nseg = 16, tm, 1
        tiles_per_seq = seq_len // tm
        keep_prev = (i % tiles_per_seq != 0).astype(F32)
        keep_next = (i % tiles_per_seq != tiles_per_seq - 1).astype(F32)
    else:
        hh = h_ref[...]
        pad, seg_len, nseg = 8, seq_len, tm // seq_len
    seg_rows = seg_len + 2 * pad
    cols = [sum(FFN_SUB[:c]) for c in range(nsub)]

    def up(c):
        width = FFN_SUB[c]
        for u_scr, w_ref in ((ua_scrs[c], wa_ref), (ug_scrs[c], wg_ref)):
            u = _dot(hh, w_ref[:, cols[c]:cols[c] + width])
            if halo:
                u_scr[...] = u
                u_scr[8:16, :] = u_scr[8:16, :] * keep_prev
                u_scr[16 + tm:24 + tm, :] = u_scr[16 + tm:24 + tm, :] * keep_next
            else:
                for sg in range(nseg):
                    r0 = sg * seg_rows
                    u_scr[r0:r0 + pad, :] = jnp.zeros((pad, width), F32)
                    u_scr[r0 + pad:r0 + pad + seg_len, :] = u[sg * seg_len:(sg + 1) * seg_len]
                    u_scr[r0 + pad + seg_len:r0 + seg_rows, :] = jnp.zeros((pad, width), F32)

    def conv(c, u_scr, cw_ref):
        w = cw_ref[:, cols[c]:cols[c] + FFN_SUB[c]]
        segs = []
        for sg in range(nseg):
            r0 = sg * seg_rows + pad
            segs.append(w[0:1, :] * u_scr[r0 - 1:r0 - 1 + seg_len, :]
                        + w[1:2, :] * u_scr[r0:r0 + seg_len, :]
                        + w[2:3, :] * u_scr[r0 + 1:r0 + 1 + seg_len, :])
        return segs[0] if nseg == 1 else jnp.concatenate(segs, axis=0)

    def down(c):
        a = conv(c, ua_scrs[c], ca_ref)
        g = conv(c, ug_scrs[c], cg_ref)
        act = (a * (g * _sigmoid(g))).astype(BF16)
        acc_scr[...] += _dot(act, wd_ref[cols[c]:cols[c] + FFN_SUB[c], :])

    @pl.when(j == 0)
    def _init():
        acc_scr[...] = jnp.zeros((tm, D), F32)

    up(0)
    for c in range(nsub):
        if c + 1 < nsub:
            up(c + 1)
        down(c)

    @pl.when(j == pl.num_programs(1) - 1)
    def _finish():
        g2 = mod_ref[...][:, 5 * D:6 * D]
        x2 = x1_ref[...] + g2 * acc_scr[...]
        if final:
            x2 = x2 * lax.rsqrt(jnp.mean(x2 * x2, axis=-1, keepdims=True) + RMS_EPS) * gf_ref[...]
        o_ref[...] = x2


def _ffn(h2, x1, mods, prm, l, lat, tokens_per_mod, seq_len, final_g, tm=512):
    t = h2.shape[0]
    tf = FFN_TF
    halo = seq_len > tm
    final = final_g is not None
    nj = D_FF // tf
    tiles_per_mod = tokens_per_mod // tm
    mod_map = (lambda i, j: (l, 1 + i // tiles_per_mod, 0, 0)) if lat else (lambda i, j: (l, 0, 0, 0))
    row = lambda i, j: (i, 0)
    in_specs = [pl.BlockSpec((tm, D), row)]
    args = [h2]
    if halo:
        r16 = tm // 16
        in_specs += [pl.BlockSpec((16, D), lambda i, j: (jnp.maximum(i * r16 - 1, 0), 0)),
                     pl.BlockSpec((16, D), lambda i, j: (jnp.minimum((i + 1) * r16, t // 16 - 1), 0))]
        args += [h2, h2]
    in_specs += [pl.BlockSpec((tm, D), row), pl.BlockSpec((None, None, 1, 6 * D), mod_map),
                 pl.BlockSpec((None, D, tf), lambda i, j: (l, 0, j)),
                 pl.BlockSpec((None, D, tf), lambda i, j: (l, 0, j + nj)),
                 pl.BlockSpec((None, 3, tf), lambda i, j: (l, 0, j)),
                 pl.BlockSpec((None, 3, tf), lambda i, j: (l, 0, j + nj)),
                 pl.BlockSpec((None, tf, D), lambda i, j: (l, j, 0))]
    args += [x1, mods, prm["w_up"], prm["w_up"], prm["ffn_conv"], prm["ffn_conv"], prm["w_down"]]
    if final:
        in_specs.append(pl.BlockSpec((1, D), lambda i, j: (0, 0)))
        args.append(final_g)
    urows = tm + 32 if halo else (tm // seq_len) * (seq_len + 16)
    scratch = [pltpu.VMEM((tm, D), F32)]
    scratch += [pltpu.VMEM((urows, w), F32) for w in FFN_SUB] * 2
    if halo:
        scratch.append(pltpu.VMEM((tm + 32, D), BF16))
    return pl.pallas_call(
        functools.partial(_ffn_kernel, halo=halo, tm=tm, seq_len=seq_len, final=final),
        grid=(t // tm, nj),
        in_specs=in_specs,
        out_specs=pl.BlockSpec((tm, D), row),
        out_shape=jax.ShapeDtypeStruct((t, D), F32),
        scratch_shapes=scratch,
        compiler_params=_cparams(("parallel", "arbitrary")),
        name="ffn_lat" if lat else "ffn_ctx",
    )(*args)


def _rope_tables(seq_len):
    rows = seq_len // GRID_W
    row = np.repeat(np.arange(rows, dtype=np.float32), GRID_W)
    col = np.tile(np.arange(GRID_W, dtype=np.float32), rows)
    freqs = (np.float32(ROPE_BASE) ** (-np.arange(ROPE_FREQS, dtype=np.float32) / np.float32(ROPE_FREQS)))
    freqs = freqs.astype(np.float32)
    ang = np.stack([row[:, None] * freqs, col[:, None] * freqs], axis=1).astype(np.float32)
    cos, sin = np.cos(ang).astype(np.float32), np.sin(ang).astype(np.float32)
    zero = np.zeros_like(sin)
    c64 = np.stack([cos, cos], axis=2).reshape(seq_len, HEAD_DIM)
    sneg64 = np.stack([-sin, zero], axis=2).reshape(seq_len, HEAD_DIM)
    spos64 = np.stack([zero, sin], axis=2).reshape(seq_len, HEAD_DIM)
    return tuple(jnp.asarray(np.tile(tb, (1, 2))) for tb in (c64, sneg64, spos64))


def _prep_params(w_in, w_out, norm_mix, norm_ffn, sc_conv, ssm_lam_re, ssm_lam_im, ssm_log_dt, ssm_b_re,
                 ssm_b_im, ssm_c_re, ssm_c_im, ssm_d, ssm_w_glu, ffn_w_up, ffn_conv, ffn_w_down):
    b_slab = lambda b: jnp.transpose(b, (0, 1, 4, 2, 3)).reshape(DEPTH, 2, SSM_CH, SSM_N)
    c_slab = lambda c: jnp.transpose(c, (0, 1, 3, 2, 4)).reshape(DEPTH, 2, SSM_CH, SSM_N)
    return {
        "w_in": w_in,
        "norm_mix": norm_mix.reshape(DEPTH, 1, D),
        "norm_ffn": norm_ffn.reshape(DEPTH, 1, D),
        "sc_conv": jnp.transpose(sc_conv, (0, 2, 1)),
        "ssm_d": ssm_d.reshape(DEPTH, 1, SSM_DIM),
        "w_glu": ssm_w_glu,
        "w_out": w_out,
        "w_up": ffn_w_up.astype(BF16),
        "ffn_conv": jnp.transpose(ffn_conv, (0, 2, 1)),
        "w_down": ffn_w_down.astype(BF16),
        "lam_re": ssm_lam_re.reshape(DEPTH, 2, 1, SSM_N),
        "lam_im": ssm_lam_im.reshape(DEPTH, 2, 1, SSM_N),
        "log_dt": jnp.repeat(ssm_log_dt, SSM_STATE, axis=-1).reshape(DEPTH, 2, 1, SSM_N),
        "b_re": b_slab(ssm_b_re),
        "b_im": b_slab(ssm_b_im),
        "c_re": c_slab(ssm_c_re),
        "c_im": c_slab(ssm_c_im),
    }


def kernel(x_prompt, x_sample, cache_k, cache_v, state_ssm_re, state_ssm_im, c, c_ctx, norm_mix, norm_ffn, norm_final, w_ada, b_ada, w_in, w_out, attn_sink, sc_conv, ssm_lam_re, ssm_lam_im, ssm_log_dt, ssm_b_re, ssm_b_im, ssm_c_re, ssm_c_im, ssm_d, ssm_w_glu, ffn_w_up, ffn_conv, ffn_w_down):
    batch, seq = x_prompt.shape[0], x_prompt.shape[1]
    dec_batch, dec_seq = x_sample.shape[0], x_sample.shape[1]
    assert batch == SCAN_ROWS and seq == SCAN_LEN
    assert dec_batch * (dec_seq // SCAN_LEN) == SCAN_ROWS and dec_batch == 2

    cs = jnp.concatenate([c_ctx[None, :], c, jnp.zeros((8 - 1 - dec_batch, D), F32)], axis=0)
    mods = _adaln(cs, w_ada, b_ada).reshape(DEPTH, 8, 1, 6 * D)
    rope_tabs = _rope_tables(dec_seq)
    gfin = norm_final.reshape(1, D)
    prm = _prep_params(w_in, w_out, norm_mix, norm_ffn, sc_conv, ssm_lam_re, ssm_lam_im, ssm_log_dt, ssm_b_re,
                       ssm_b_im, ssm_c_re, ssm_c_im, ssm_d, ssm_w_glu, ffn_w_up, ffn_conv, ffn_w_down)
    s0_all = jnp.transpose(jnp.concatenate([state_ssm_re.reshape(dec_batch, DEPTH, 2, SSM_N),
                                            state_ssm_im.reshape(dec_batch, DEPTH, 2, SSM_N)], axis=-1),
                           (1, 2, 0, 3))

    xp = x_prompt.reshape(batch * seq, D)
    xs = x_sample.reshape(dec_batch * dec_seq, D)
    kv_out, fin_out = [], []
    for l in range(DEPTH):
        last = gfin if l == DEPTH - 1 else None

        q, kd, vd, kv, g3, su = _inproj(xp, mods, prm, l, None, batch * seq, seq)
        attn = _attn_ctx(attn_sink, l, q, kd, vd, seq)
        yf, yb, fin = _scan(prm, l, su, True)
        x1, h2 = _mix(xp, mods, attn, g3, su, yf, yb, prm, l, False, batch * seq, seq)
        xp = _ffn(h2, x1, mods, prm, l, False, batch * seq, seq, last)
        kv_out.append(kv)
        fin_out.append(fin)

        q, kd, vd, g3, su = _inproj(xs, mods, prm, l, rope_tabs, dec_seq, dec_seq)
        attn = _attn_lat(attn_sink, l, q, kd, vd, cache_k, cache_v, dec_batch, dec_seq)
        (ez,) = _scan(prm, l, su, False)
        yf, yb, _ = _scan(prm, l, su, True, s0=s0_all[l], ez=ez)
        x1, h2 = _mix(xs, mods, attn, g3, su, yf, yb, prm, l, True, dec_seq, dec_seq)
        xs = _ffn(h2, x1, mods, prm, l, True, dec_seq, dec_seq, last)

    kv_all = jnp.stack(kv_out, axis=0).reshape(DEPTH, batch, seq, 2, KV_HEADS, HEAD_DIM)
    kv_all = jnp.transpose(kv_all, (3, 1, 0, 2, 4, 5))
    fin_all = jnp.stack(fin_out, axis=0).reshape(DEPTH, 2, batch, 2, SSM_GROUPS, SSM_STATE)
    fin_all = jnp.transpose(fin_all, (3, 2, 0, 1, 4, 5))
    return (xp.reshape(batch, seq, D), xs.reshape(dec_batch, dec_seq, D),
            kv_all[0], kv_all[1], fin_all[0], fin_all[1])
```

```python
import functools
import math

import jax
import jax.numpy as jnp
import numpy as np
from jax import lax
from jax.experimental import pallas as pl
from jax.experimental.pallas import tpu as pltpu

F32 = jnp.float32
BF16 = jnp.bfloat16

D = 1024
DEPTH = 2
GRID_W = 64
ATTN_DIM = 512
SC_DIM = 256
SSM_DIM = 256
HEAD_DIM = 64
N_HEADS = 8
KV_HEADS = 2
KV_DIM = 128
WINDOW = 128
Q_BLOCK = 128
ROPE_BASE = 10000.0
ROPE_FREQS = 16
SSM_CH = 16
SSM_GROUPS = 16
SSM_STATE = 64
SSM_N = SSM_GROUPS * SSM_STATE
IN_DIM = ATTN_DIM + 2 * KV_DIM + 3 * SC_DIM + SSM_DIM
D_FF = 2816
LOG2E = math.log2(math.e)
Q_SCALE = LOG2E / math.sqrt(HEAD_DIM)
RMS_EPS = 1e-6
NEG_BIG = -1e30

LANES = 128
SCAN_ROWS = 16
SCAN_LEN = 256
SCAN_TT = 64
SCAN_LW = 256
SCAN_MM_ROWS = 1024
FFN_TF = 1408
FFN_SUB = (768, 512, 128)
assert D_FF % FFN_TF == 0 and sum(FFN_SUB) == FFN_TF
VMEM_LIMIT = 56 * 1024 * 1024


def _cparams(sem):
    return pltpu.CompilerParams(dimension_semantics=sem, vmem_limit_bytes=VMEM_LIMIT)


def _sigmoid(x):
    return 1.0 / (1.0 + jnp.exp(-x))


def _rms_mod(x, g, scale, shift):
    y = x * lax.rsqrt(jnp.mean(x * x, axis=-1, keepdims=True) + RMS_EPS) * g
    return y * (1.0 + scale) + shift


def _split_bf16(v):
    hi = v.astype(BF16)
    lo = (v - hi.astype(F32)).astype(BF16)
    return hi, lo


def _dot(a, b):
    return jnp.dot(a, b, preferred_element_type=F32)


def _dot_t(a, b):
    return lax.dot_general(a, b, (((1,), (1,)), ((), ())), preferred_element_type=F32)


def _layer_spec(shape, l):
    zeros = (0,) * len(shape)
    return pl.BlockSpec((None,) + tuple(shape), lambda *_: (l,) + zeros)


def _half_tiles(t):
    lo = lax.broadcasted_iota(jnp.int32, t.shape, 1) < HEAD_DIM
    tr = pltpu.roll(t, HEAD_DIM, 1)
    return [jnp.where(lo, t, 0.0).astype(BF16), jnp.where(lo, 0.0, tr).astype(BF16),
            jnp.where(lo, tr, 0.0).astype(BF16), jnp.where(lo, 0.0, t).astype(BF16)]


def _adaln_kernel(c_ref, w_ref, b_ref, o_ref):
    c = c_ref[...]
    s = c * _sigmoid(c)
    s_hi, s_lo = _split_bf16(s)
    w_hi, w_lo = _split_bf16(w_ref[...])
    o_ref[...] = _dot(s_hi, w_hi) + _dot(s_lo, w_hi) + _dot(s_hi, w_lo) + b_ref[...]


def _adaln(cs, w_ada, b_ada):
    tn = 2048
    return pl.pallas_call(
        _adaln_kernel,
        grid=(DEPTH, 6 * D // tn),
        in_specs=[
            pl.BlockSpec((8, D), lambda l, j: (0, 0)),
            pl.BlockSpec((None, D, tn), lambda l, j: (l, 0, j)),
            pl.BlockSpec((None, 1, tn), lambda l, j: (l, 0, j)),
        ],
        out_specs=pl.BlockSpec((None, 8, tn), lambda l, j: (l, 0, j)),
        out_shape=jax.ShapeDtypeStruct((DEPTH, 8, 6 * D), F32),
        compiler_params=_cparams(("parallel", "parallel")),
        name="adaln",
    )(cs, w_ada, b_ada.reshape(DEPTH, 1, 6 * D))


def _inproj_kernel(x_ref, mod_ref, g_ref, w_ref, *rest, rope):
    if rope:
        cos_ref, sneg_ref, spos_ref, q_ref, kd_ref, vd_ref, g3_ref, su_ref, wbf_scr = rest
    else:
        q_ref, kd_ref, vd_ref, kv_ref, g3_ref, su_ref, wbf_scr = rest

    @pl.when(pl.program_id(0) == 0)
    def _cast_weights():
        wbf_scr[...] = w_ref[...].astype(BF16)

    m = mod_ref[...]
    h = _rms_mod(x_ref[...], g_ref[...], m[:, D:2 * D], m[:, 0:D]).astype(BF16)
    acc = _dot(h, wbf_scr[...])
    k = acc[:, 512:640]
    v = acc[:, 640:768]
    if rope:
        c, sn, sp = cos_ref[...], sneg_ref[...], spos_ref[...]

        def rot(t):
            return t * c + pltpu.roll(t, LANES - ROPE_FREQS, 1) * sn + pltpu.roll(t, ROPE_FREQS, 1) * sp

        for p in range(4):
            q_ref[:, LANES * p:LANES * (p + 1)] = (rot(acc[:, LANES * p:LANES * (p + 1)]) * Q_SCALE).astype(BF16)
        k = rot(k)
    else:
        q_ref[...] = (acc[:, 0:512] * Q_SCALE).astype(BF16)
        kv_ref[...] = acc[:, 512:768]
    for ref, t in ((kd_ref, k), (vd_ref, v)):
        for p, tile in enumerate(_half_tiles(t)):
            ref[:, LANES * p:LANES * (p + 1)] = tile
    g3_ref[...] = acc[:, 768:1536].astype(BF16)
    for r in range(acc.shape[0] // SCAN_LEN):
        su_ref[:, r * SSM_DIM:(r + 1) * SSM_DIM] = acc[r * SCAN_LEN:(r + 1) * SCAN_LEN, 1536:1792]


def _inproj(x, mods, prm, l, rope_tabs, tokens_per_mod, seq_len, tm=512):
    t = x.shape[0]
    rope = rope_tabs is not None
    tiles_per_mod = tokens_per_mod // tm
    tiles_per_seq = seq_len // tm
    if rope:
        mod_map = lambda i: (l, 1 + i // tiles_per_mod, 0, 0)
    else:
        mod_map = lambda i: (l, 0, 0, 0)
    row = lambda i: (i, 0)
    in_specs = [
        pl.BlockSpec((tm, D), row),
        pl.BlockSpec((None, None, 1, 6 * D), mod_map),
        _layer_spec((1, D), l),
        _layer_spec((D, IN_DIM), l),
    ]
    args = [x, mods, prm["norm_mix"], prm["w_in"]]
    if rope:
        in_specs += [pl.BlockSpec((tm, LANES), lambda i: (i % tiles_per_seq, 0))] * 3
        args += list(rope_tabs)
    out_shape = [jax.ShapeDtypeStruct((t, 512), BF16)] * 3
    out_specs = [pl.BlockSpec((tm, 512), row)] * 3
    if not rope:
        out_shape.append(jax.ShapeDtypeStruct((t, 256), F32))
        out_specs.append(pl.BlockSpec((tm, 256), row))
    out_shape += [jax.ShapeDtypeStruct((t, 768), BF16),
                  jax.ShapeDtypeStruct((SCAN_LEN, t // SCAN_LEN * SSM_DIM), F32)]
    out_specs += [pl.BlockSpec((tm, 768), row),
                  pl.BlockSpec((SCAN_LEN, tm // SCAN_LEN * SSM_DIM), lambda i: (0, i))]
    return pl.pallas_call(
        functools.partial(_inproj_kernel, rope=rope),
        grid=(t // tm,),
        in_specs=in_specs,
        out_specs=out_specs,
        out_shape=out_shape,
        scratch_shapes=[pltpu.VMEM((D, IN_DIM), BF16)],
        compiler_params=_cparams(("arbitrary",)),
        name="inproj_lat" if rope else "inproj_ctx",
    )(*args)


def _attention(sink_ref, l, q_ref, o_ref, keys, vals, bias, nq):
    top = lax.broadcasted_iota(jnp.int32, (2 * nq, 1), 0) < nq
    scores = []
    for kvh in range(KV_HEADS):
        q2 = jnp.concatenate([q_ref[:, LANES * (2 * kvh + pp):LANES * (2 * kvh + pp + 1)] for pp in range(2)],
                             axis=0)
        scores.append([_dot_t(q2, keys[kvh][half]) for half in range(2)])
    for kvh in range(KV_HEADS):
        acc = None
        for half in range(2):
            sk = jnp.where(top, sink_ref[l, 4 * kvh + half], sink_ref[l, 4 * kvh + 2 + half]) * LOG2E
            s = scores[kvh][half]
            if bias is not None:
                nb_ = bias.shape[1]
                s = jnp.concatenate([s[:, 0:nb_] + bias, s[:, nb_:]], axis=1)
            m = jnp.maximum(jnp.max(s, axis=-1, keepdims=True), sk)
            e = jnp.exp2(s - m)
            den = jnp.sum(e, axis=-1, keepdims=True) + jnp.exp2(sk - m)
            o = _dot(e.astype(BF16), vals[kvh][half]) / den
            acc = o if acc is None else acc + o
        o_ref[:, LANES * 2 * kvh:LANES * (2 * kvh + 1)] = acc[0:nq].astype(BF16)
        o_ref[:, LANES * (2 * kvh + 1):LANES * (2 * kvh + 2)] = acc[nq:2 * nq].astype(BF16)


def _kv_tiles(kvh):
    return [slice(LANES * (2 * kvh + h), LANES * (2 * kvh + h + 1)) for h in range(2)]


def _attn_ctx_kernel(sink_ref, q_ref, kd_ref, vd_ref, o_ref, *, l):
    keys = [[kd_ref[:, t] for t in _kv_tiles(kvh)] for kvh in range(KV_HEADS)]
    vals = [[vd_ref[:, t] for t in _kv_tiles(kvh)] for kvh in range(KV_HEADS)]
    _attention(sink_ref, l, q_ref, o_ref, keys, vals, None, q_ref.shape[0])


def _attn_ctx(sink, l, q, kd, vd, seq_len):
    t = q.shape[0]
    row = lambda b: (b, 0)
    return pl.pallas_call(
        functools.partial(_attn_ctx_kernel, l=l),
        grid=(t // seq_len,),
        in_specs=[
            pl.BlockSpec(memory_space=pltpu.SMEM),
            pl.BlockSpec((seq_len, 512), row),
            pl.BlockSpec((seq_len, 512), row),
            pl.BlockSpec((seq_len, 512), row),
        ],
        out_specs=pl.BlockSpec((seq_len, 512), row),
        out_shape=jax.ShapeDtypeStruct((t, 512), BF16),
        compiler_params=_cparams(("parallel",)),
        name="attn_ctx",
    )(sink, q, kd, vd)


def _attn_lat_kernel(sink_ref, q_ref, kp_ref, kc_ref, kn_ref, vp_ref, vc_ref, vn_ref, ck_ref, cv_ref, o_ref,
                     ckt_scr, cvt_scr, *, l):
    i = pl.program_id(1)
    nb = pl.num_programs(1)

    @pl.when(i == 0)
    def _context_tiles():
        for p, (kt, vt) in enumerate(zip(_half_tiles(ck_ref[...]), _half_tiles(cv_ref[...]))):
            ckt_scr[p] = kt
            cvt_scr[p] = vt

    r = lax.broadcasted_iota(jnp.int32, (Q_BLOCK, Q_BLOCK), 0)
    j = lax.broadcasted_iota(jnp.int32, (Q_BLOCK, Q_BLOCK), 1)
    bias = jnp.concatenate([
        jnp.where(jnp.logical_and(j >= r, i > 0), 0.0, NEG_BIG),
        jnp.where(jnp.logical_and(j <= r, i < nb - 1), 0.0, NEG_BIG)], axis=1)
    bias = jnp.concatenate([bias, bias], axis=0)
    def gather(refs, ctx_scr, kvh, h):
        t = _kv_tiles(kvh)[h]
        return jnp.concatenate([ref[:, t] for ref in refs] + [ctx_scr[2 * kvh + h]], axis=0)

    keys = [[gather((kp_ref, kn_ref, kc_ref), ckt_scr, kvh, h) for h in range(2)] for kvh in range(KV_HEADS)]
    vals = [[gather((vp_ref, vn_ref, vc_ref), cvt_scr, kvh, h) for h in range(2)] for kvh in range(KV_HEADS)]
    _attention(sink_ref, l, q_ref, o_ref, keys, vals, bias, Q_BLOCK)


def _attn_lat(sink, l, q, kd, vd, cache_k, cache_v, batch, seq_len):
    nb = seq_len // Q_BLOCK
    past = cache_k.shape[2]
    cur = lambda b, i: (b * nb + i, 0)
    prev = lambda b, i: (b * nb + jnp.maximum(i - 1, 0), 0)
    nxt = lambda b, i: (b * nb + jnp.minimum(i + 1, nb - 1), 0)
    kvspec = lambda f: pl.BlockSpec((Q_BLOCK, 512), f)
    cspec = pl.BlockSpec((None, None, past, KV_DIM), lambda b, i: (b, l, 0, 0))
    ck = cache_k.reshape(batch, DEPTH, past, KV_DIM)
    cv = cache_v.reshape(batch, DEPTH, past, KV_DIM)
    return pl.pallas_call(
        functools.partial(_attn_lat_kernel, l=l),
        grid=(batch, nb),
        in_specs=[
            pl.BlockSpec(memory_space=pltpu.SMEM),
            pl.BlockSpec((Q_BLOCK, 512), cur),
            kvspec(prev), kvspec(cur), kvspec(nxt),
            kvspec(prev), kvspec(cur), kvspec(nxt),
            cspec, cspec,
        ],
        out_specs=pl.BlockSpec((Q_BLOCK, 512), cur),
        out_shape=jax.ShapeDtypeStruct((batch * seq_len, 512), BF16),
        scratch_shapes=[pltpu.VMEM((2 * KV_HEADS, past, LANES), BF16)] * 2,
        compiler_params=_cparams(("parallel", "arbitrary")),
        name="attn_lat",
    )(sink, q, kd, kd, kd, vd, vd, vd, ck, cv)


def _scan_kernel(lre_ref, lim_ref, ldt_ref, bre_ref, bim_ref, cre_ref, cim_ref, *rest, emit_y, chain):
    rest = list(rest)
    if chain:
        s0_ref, ez_ref = rest[:2]
        rest = rest[2:]
    suf_ref, sub_ref = rest[:2]
    rest = rest[2:]
    if emit_y:
        y_refs = rest[:2]
        rest = rest[2:]
    fin_ref, a_scr, bb_scr, h_scr, bu_scr, u_scr = rest[:6]
    if emit_y:
        y_scr, ct_scr = rest[6:8]
    i = pl.program_id(0)
    tt = SCAN_TT
    n = SSM_N

    @pl.when(i == 0)
    def _prologue():
        row_g = lax.shift_right_logical(lax.broadcasted_iota(jnp.int32, (SSM_DIM, n), 0), 4)
        col_g = lax.shift_right_logical(lax.broadcasted_iota(jnp.int32, (SSM_DIM, n), 1), 6)
        own = row_g == col_g

        def blockdiag(ref, d):
            return jnp.where(own, jnp.concatenate([ref[d]] * SSM_GROUPS, axis=0), 0.0)

        for d in range(2):
            lr, li = lre_ref[d], lim_ref[d]
            dt = jnp.exp(ldt_ref[d])
            mag = jnp.exp(lr * dt)
            ar, ai = mag * jnp.cos(li * dt), mag * jnp.sin(li * dt)
            den = lr * lr + li * li
            fr = ((ar - 1.0) * lr + ai * li) / den
            fi = (ai * lr - (ar - 1.0) * li) / den
            bre, bim = blockdiag(bre_ref, d), blockdiag(bim_ref, d)
            bb_scr[d, :, 0:n] = (fr * bre - fi * bim).astype(BF16)
            bb_scr[d, :, n:2 * n] = (fr * bim + fi * bre).astype(BF16)
            if emit_y:
                ct_scr[d, :, 0:n] = blockdiag(cre_ref, d).astype(BF16)
                ct_scr[d, :, n:2 * n] = (-blockdiag(cim_ref, d)).astype(BF16)
            a_scr[d, 0] = jnp.broadcast_to(ar, (8, n))
            a_scr[d, 1] = jnp.broadcast_to(ai, (8, n))
            if chain:
                pr, pi_ = ar, ai
                for _ in range(8):
                    pr, pi_ = pr * pr - pi_ * pi_, 2.0 * pr * pi_
                chunks = SCAN_ROWS // 2
                for b in range(2):
                    hr = s0_ref[d, b:b + 1, 0:n]
                    hi = s0_ref[d, b:b + 1, n:2 * n]
                    order = range(chunks) if d == 0 else range(chunks - 1, -1, -1)
                    for c in order:
                        rw = b * chunks + c
                        h_scr[d, rw:rw + 1, 0:n] = hr
                        h_scr[d, rw:rw + 1, n:2 * n] = hi
                        er = ez_ref[d, rw:rw + 1, 0:n]
                        ei = ez_ref[d, rw:rw + 1, n:2 * n]
                        hr, hi = pr * hr - pi_ * hi + er, pr * hi + pi_ * hr + ei
            else:
                h_scr[d] = jnp.zeros((SCAN_ROWS, 2 * n), F32)

    ntile = n // LANES
    group = SCAN_LW // LANES

    def project_in(d):
        su_ref = suf_ref if d == 0 else sub_ref
        for r in range(SCAN_ROWS):
            for sl in range(SSM_DIM // LANES):
                col = r * SSM_DIM + sl * LANES
                u_scr[d, sl, pl.ds(r, tt, stride=SCAN_ROWS), :] = su_ref[:, col:col + LANES]
        for r0 in range(0, tt * SCAN_ROWS, SCAN_MM_ROWS):
            rows = slice(r0, r0 + SCAN_MM_ROWS)
            u = jnp.concatenate([u_scr[d, sl, rows, :] for sl in range(SSM_DIM // LANES)], axis=1).astype(BF16)
            bu_scr[d, rows, :] = _dot(u, bb_scr[d])

    def recur(d):
        for c0 in range(0, ntile, group):
            lre = slice(c0 * LANES, (c0 + group) * LANES)
            lim = slice(n + c0 * LANES, n + (c0 + group) * LANES)
            ar, ai = a_scr[d, 0, :, lre], a_scr[d, 1, :, lre]
            halves = [slice(0, 8), slice(8, 16)]
            hr = [h_scr[d, hs, lre] for hs in halves]
            hi = [h_scr[d, hs, lim] for hs in halves]
            steps = range(tt) if d == 0 else range(tt - 1, -1, -1)
            for s in steps:
                for k in range(2):
                    rows = slice(s * SCAN_ROWS + 8 * k, s * SCAN_ROWS + 8 * k + 8)
                    br, bi = bu_scr[d, rows, lre], bu_scr[d, rows, lim]
                    hr[k], hi[k] = ar * hr[k] - ai * hi[k] + br, ar * hi[k] + ai * hr[k] + bi
                    if emit_y:
                        bu_scr[d, rows, lre] = hr[k]
                        bu_scr[d, rows, lim] = hi[k]
            for k, hs in enumerate(halves):
                h_scr[d, hs, lre] = hr[k]
                h_scr[d, hs, lim] = hi[k]

    def project_out(d):
        for r0 in range(0, tt * SCAN_ROWS, SCAN_MM_ROWS):
            rows = slice(r0, r0 + SCAN_MM_ROWS)
            y = _dot_t(bu_scr[d, rows, :].astype(BF16), ct_scr[d])
            for sl in range(SSM_DIM // LANES):
                y_scr[d, sl, rows, :] = y[:, sl * LANES:(sl + 1) * LANES]
        for r in range(SCAN_ROWS):
            for sl in range(SSM_DIM // LANES):
                col = r * SSM_DIM + sl * LANES
                y_refs[d][:, col:col + LANES] = y_scr[d, sl, pl.ds(r, tt, stride=SCAN_ROWS), :].astype(BF16)

    project_in(0)
    project_in(1)
    recur(0)
    if emit_y:
        project_out(0)
    recur(1)
    if emit_y:
        project_out(1)

    @pl.when(i == pl.num_programs(0) - 1)
    def _final():
        fin_ref[...] = h_scr[...]


def _scan(prm, l, su_tm, emit_y, s0=None, ez=None):
    chain = s0 is not None
    nt = SCAN_LEN // SCAN_TT
    n = SSM_N
    full3 = lambda shape: pl.BlockSpec(shape, lambda i: (0, 0, 0))
    in_specs = [_layer_spec((2, 1, n), l)] * 3 + [_layer_spec((2, SSM_CH, n), l)] * 4
    args = [prm["lam_re"], prm["lam_im"], prm["log_dt"], prm["b_re"], prm["b_im"], prm["c_re"], prm["c_im"]]
    if chain:
        in_specs += [full3((2, 2, 2 * n)), full3((2, SCAN_ROWS, 2 * n))]
        args += [s0, ez]
    tblk = (SCAN_TT, SCAN_ROWS * SSM_DIM)
    fwd = lambda i: (i, 0)
    bwd = lambda i: (nt - 1 - i, 0)
    in_specs += [pl.BlockSpec(tblk, fwd), pl.BlockSpec(tblk, bwd)]
    args += [su_tm, su_tm]
    out_shape, out_specs = [], []
    scratch = [
        pltpu.VMEM((2, 2, 8, n), F32),
        pltpu.VMEM((2, SSM_DIM, 2 * n), BF16),
        pltpu.VMEM((2, SCAN_ROWS, 2 * n), F32),
        pltpu.VMEM((2, SCAN_ROWS * SCAN_TT, 2 * n), F32),
        pltpu.VMEM((2, SSM_DIM // LANES, SCAN_ROWS * SCAN_TT, LANES), F32),
    ]
    if emit_y:
        yshape = jax.ShapeDtypeStruct((SCAN_LEN, SCAN_ROWS * SSM_DIM), BF16)
        out_shape += [yshape, yshape]
        out_specs += [pl.BlockSpec(tblk, fwd), pl.BlockSpec(tblk, bwd)]
        scratch.append(pltpu.VMEM((2, SSM_DIM // LANES, SCAN_ROWS * SCAN_TT, LANES), F32))
        scratch.append(pltpu.VMEM((2, SSM_DIM, 2 * n), BF16))
    out_shape.append(jax.ShapeDtypeStruct((2, SCAN_ROWS, 2 * n), F32))
    out_specs.append(full3((2, SCAN_ROWS, 2 * n)))
    return pl.pallas_call(
        functools.partial(_scan_kernel, emit_y=emit_y, chain=chain),
        grid=(nt,),
        in_specs=in_specs,
        out_specs=out_specs,
        out_shape=out_shape,
        scratch_shapes=scratch,
        compiler_params=_cparams(("arbitrary",)),
        name="scan_chain" if chain else ("scan_y" if emit_y else "scan_state"),
    )(*args)


def _shifted(scr_ref, off, rows, pos, seq_len, w):
    mid = scr_ref[off:off + rows, :]
    up = jnp.where(pos != 0, scr_ref[off - 1:off - 1 + rows, :], 0.0)
    dn = jnp.where(pos != seq_len - 1, scr_ref[off + 1:off + 1 + rows, :], 0.0)
    return w[0:1, :] * up + w[1:2, :] * mid + w[2:3, :] * dn


def _mix_kernel(x_ref, mod_ref, attn_ref, g3_ref, *rest, halo, tm, seq_len):
    rest = list(rest)
    if halo:
        gp_ref, gn_ref = rest[:2]
        rest = rest[2:]
    (su_ref, yf_ref, yb_ref, wc_ref, dsk_ref, wglu_ref, wo_ref, gn2_ref,
     x1_ref, h2_ref, z_scr, wglu_scr, wo_scr) = rest
    i = pl.program_id(0)

    @pl.when(i == 0)
    def _cast_weights():
        wglu_scr[...] = wglu_ref[...].astype(BF16)
        wo_scr[...] = wo_ref[...].astype(BF16)

    def rows_of(ref):
        pieces = [ref[:, r * SSM_DIM:(r + 1) * SSM_DIM].astype(F32) for r in range(tm // SCAN_LEN)]
        return pieces[0] if len(pieces) == 1 else jnp.concatenate(pieces, axis=0)

    m = mod_ref[...]
    g1, sh2, sc2 = m[:, 2 * D:3 * D], m[:, 3 * D:4 * D], m[:, 4 * D:5 * D]
    pos = (i * tm + lax.broadcasted_iota(jnp.int32, (tm, 1), 0)) % seq_len

    def gate_prod(ref):
        return ref[:, SC_DIM:2 * SC_DIM].astype(F32) * ref[:, 2 * SC_DIM:3 * SC_DIM].astype(F32)

    gb = g3_ref[:, 0:SC_DIM].astype(F32)
    z_scr[16:16 + tm, :] = gate_prod(g3_ref)
    if halo:
        z_scr[0:16, :] = gate_prod(gp_ref)
        z_scr[16 + tm:32 + tm, :] = gate_prod(gn_ref)
    else:
        z_scr[0:16, :] = jnp.zeros((16, SC_DIM), F32)
        z_scr[16 + tm:32 + tm, :] = jnp.zeros((16, SC_DIM), F32)
    conv = gb * _shifted(z_scr, 16, tm, pos, seq_len, wc_ref[...])

    y = dsk_ref[...] * rows_of(su_ref) + rows_of(yf_ref) + rows_of(yb_ref)
    zz = 0.5 * y * (1.0 + jnp.tanh(math.sqrt(2.0 / math.pi) * (y + 0.044715 * (y * y * y))))
    ssm = zz * _sigmoid(_dot(zz.astype(BF16), wglu_scr[...]))

    mix = (_dot(attn_ref[...], wo_scr[0:ATTN_DIM, :])
           + _dot(conv.astype(BF16), wo_scr[ATTN_DIM:ATTN_DIM + SC_DIM, :])
           + _dot(ssm.astype(BF16), wo_scr[ATTN_DIM + SC_DIM:, :]))
    x1 = x_ref[...] + g1 * mix
    x1_ref[...] = x1
    h2_ref[...] = _rms_mod(x1, gn2_ref[...], sc2, sh2).astype(BF16)


def _mix(x, mods, attn, g3, su, yf, yb, prm, l, lat, tokens_per_mod, seq_len, tm=512):
    t = x.shape[0]
    halo = seq_len > tm
    tiles_per_mod = tokens_per_mod // tm
    mod_map = (lambda i: (l, 1 + i // tiles_per_mod, 0, 0)) if lat else (lambda i: (l, 0, 0, 0))
    row = lambda i: (i, 0)
    in_specs = [pl.BlockSpec((tm, D), row), pl.BlockSpec((None, None, 1, 6 * D), mod_map),
                pl.BlockSpec((tm, 512), row), pl.BlockSpec((tm, 768), row)]
    args = [x, mods, attn, g3]
    if halo:
        r16 = tm // 16
        in_specs += [pl.BlockSpec((16, 768), lambda i: (jnp.maximum(i * r16 - 1, 0), 0)),
                     pl.BlockSpec((16, 768), lambda i: (jnp.minimum((i + 1) * r16, t // 16 - 1), 0))]
        args += [g3, g3]
    assert tm % SCAN_LEN == 0
    in_specs += [pl.BlockSpec((SCAN_LEN, tm // SCAN_LEN * SSM_DIM), lambda i: (0, i))] * 3
    args += [su, yf, yb]
    in_specs += [_layer_spec((3, SC_DIM), l), _layer_spec((1, SSM_DIM), l), _layer_spec((SSM_DIM, SSM_DIM), l),
                 _layer_spec((D, D), l), _layer_spec((1, D), l)]
    args += [prm["sc_conv"], prm["ssm_d"], prm["w_glu"], prm["w_out"], prm["norm_ffn"]]
    return pl.pallas_call(
        functools.partial(_mix_kernel, halo=halo, tm=tm, seq_len=seq_len),
        grid=(t // tm,),
        in_specs=in_specs,
        out_specs=[pl.BlockSpec((tm, D), row), pl.BlockSpec((tm, D), row)],
        out_shape=[jax.ShapeDtypeStruct((t, D), F32), jax.ShapeDtypeStruct((t, D), BF16)],
        scratch_shapes=[pltpu.VMEM((tm + 32, SC_DIM), F32), pltpu.VMEM((SSM_DIM, SSM_DIM), BF16),
                        pltpu.VMEM((D, D), BF16)],
        compiler_params=_cparams(("arbitrary",)),
        name="mix_lat" if lat else "mix_ctx",
    )(*args)


def _ffn_kernel(h_ref, *rest, halo, tm, seq_len, final):
    rest = list(rest)
    if halo:
        hp_ref, hn_ref = rest[:2]
        rest = rest[2:]
    x1_ref, mod_ref, wa_ref, wg_ref, ca_ref, cg_ref, wd_ref = rest[:7]
    rest = rest[7:]
    if final:
        gf_ref = rest[0]
        rest = rest[1:]
    o_ref, acc_scr = rest[:2]
    rest = rest[2:]
    nsub = len(FFN_SUB)
    ua_scrs, ug_scrs = rest[:nsub], rest[nsub:2 * nsub]
    i = pl.program_id(0)
    j = pl.program_id(1)

    if halo:
        hcat_scr = rest[2 * nsub]

        @pl.when(j == 0)
        def _stage():
            hcat_scr[0:16, :] = hp_ref[...]
            hcat_scr[16:16 + tm, :] = h_ref[...]
            hcat_scr[16 + tm:32 + tm, :] = hn_ref[...]

        hh = hcat_scr[...]
        pad, seg_len, nseg = 16, tm, 1
        tiles_per_seq = seq_len // tm
        keep_prev = (i % tiles_per_seq != 0).astype(F32)
        keep_next = (i % tiles_per_seq != tiles_per_seq - 1).astype(F32)
    else:
        hh = h_ref[...]
        pad, seg_len, nseg = 8, seq_len, tm // seq_len
    seg_rows = seg_len + 2 * pad
    cols = [sum(FFN_SUB[:c]) for c in range(nsub)]

    def up(c):
        width = FFN_SUB[c]
        for u_scr, w_ref in ((ua_scrs[c], wa_ref), (ug_scrs[c], wg_ref)):
            u = _dot(hh, w_ref[:, cols[c]:cols[c] + width])
            if halo:
                u_scr[...] = u
                u_scr[8:16, :] = u_scr[8:16, :] * keep_prev
                u_scr[16 + tm:24 + tm, :] = u_scr[16 + tm:24 + tm, :] * keep_next
            else:
                for sg in range(nseg):
                    r0 = sg * seg_rows
                    u_scr[r0:r0 + pad, :] = jnp.zeros((pad, width), F32)
                    u_scr[r0 + pad:r0 + pad + seg_len, :] = u[sg * seg_len:(sg + 1) * seg_len]
                    u_scr[r0 + pad + seg_len:r0 + seg_rows, :] = jnp.zeros((pad, width), F32)

    def conv(c, u_scr, cw_ref):
        w = cw_ref[:, cols[c]:cols[c] + FFN_SUB[c]]
        segs = []
        for sg in range(nseg):
            r0 = sg * seg_rows + pad
            segs.append(w[0:1, :] * u_scr[r0 - 1:r0 - 1 + seg_len, :]
                        + w[1:2, :] * u_scr[r0:r0 + seg_len, :]
                        + w[2:3, :] * u_scr[r0 + 1:r0 + 1 + seg_len, :])
        return segs[0] if nseg == 1 else jnp.concatenate(segs, axis=0)

    def down(c):
        a = conv(c, ua_scrs[c], ca_ref)
        g = conv(c, ug_scrs[c], cg_ref)
        act = (a * (g * _sigmoid(g))).astype(BF16)
        acc_scr[...] += _dot(act, wd_ref[cols[c]:cols[c] + FFN_SUB[c], :])

    @pl.when(j == 0)
    def _init():
        acc_scr[...] = jnp.zeros((tm, D), F32)

    up(0)
    for c in range(nsub):
        if c + 1 < nsub:
            up(c + 1)
        down(c)

    @pl.when(j == pl.num_programs(1) - 1)
    def _finish():
        g2 = mod_ref[...][:, 5 * D:6 * D]
        x2 = x1_ref[...] + g2 * acc_scr[...]
        if final:
            x2 = x2 * lax.rsqrt(jnp.mean(x2 * x2, axis=-1, keepdims=True) + RMS_EPS) * gf_ref[...]
        o_ref[...] = x2


def _ffn(h2, x1, mods, prm, l, lat, tokens_per_mod, seq_len, final_g, tm=512):
    t = h2.shape[0]
    tf = FFN_TF
    halo = seq_len > tm
    final = final_g is not None
    nj = D_FF // tf
    tiles_per_mod = tokens_per_mod // tm
    mod_map = (lambda i, j: (l, 1 + i // tiles_per_mod, 0, 0)) if lat else (lambda i, j: (l, 0, 0, 0))
    row = lambda i, j: (i, 0)
    in_specs = [pl.BlockSpec((tm, D), row)]
    args = [h2]
    if halo:
        r16 = tm // 16
        in_specs += [pl.BlockSpec((16, D), lambda i, j: (jnp.maximum(i * r16 - 1, 0), 0)),
                     pl.BlockSpec((16, D), lambda i, j: (jnp.minimum((i + 1) * r16, t // 16 - 1), 0))]
        args += [h2, h2]
    in_specs += [pl.BlockSpec((tm, D), row), pl.BlockSpec((None, None, 1, 6 * D), mod_map),
                 pl.BlockSpec((None, D, tf), lambda i, j: (l, 0, j)),
                 pl.BlockSpec((None, D, tf), lambda i, j: (l, 0, j + nj)),
                 pl.BlockSpec((None, 3, tf), lambda i, j: (l, 0, j)),
                 pl.BlockSpec((None, 3, tf), lambda i, j: (l, 0, j + nj)),
                 pl.BlockSpec((None, tf, D), lambda i, j: (l, j, 0))]
    args += [x1, mods, prm["w_up"], prm["w_up"], prm["ffn_conv"], prm["ffn_conv"], prm["w_down"]]
    if final:
        in_specs.append(pl.BlockSpec((1, D), lambda i, j: (0, 0)))
        args.append(final_g)
    urows = tm + 32 if halo else (tm // seq_len) * (seq_len + 16)
    scratch = [pltpu.VMEM((tm, D), F32)]
    scratch += [pltpu.VMEM((urows, w), F32) for w in FFN_SUB] * 2
    if halo:
        scratch.append(pltpu.VMEM((tm + 32, D), BF16))
    return pl.pallas_call(
        functools.partial(_ffn_kernel, halo=halo, tm=tm, seq_len=seq_len, final=final),
        grid=(t // tm, nj),
        in_specs=in_specs,
        out_specs=pl.BlockSpec((tm, D), row),
        out_shape=jax.ShapeDtypeStruct((t, D), F32),
        scratch_shapes=scratch,
        compiler_params=_cparams(("parallel", "arbitrary")),
        name="ffn_lat" if lat else "ffn_ctx",
    )(*args)


def _rope_tables(seq_len):
    rows = seq_len // GRID_W
    row = np.repeat(np.arange(rows, dtype=np.float32), GRID_W)
    col = np.tile(np.arange(GRID_W, dtype=np.float32), rows)
    freqs = (np.float32(ROPE_BASE) ** (-np.arange(ROPE_FREQS, dtype=np.float32) / np.float32(ROPE_FREQS)))
    freqs = freqs.astype(np.float32)
    ang = np.stack([row[:, None] * freqs, col[:, None] * freqs], axis=1).astype(np.float32)
    cos, sin = np.cos(ang).astype(np.float32), np.sin(ang).astype(np.float32)
    zero = np.zeros_like(sin)
    c64 = np.stack([cos, cos], axis=2).reshape(seq_len, HEAD_DIM)
    sneg64 = np.stack([-sin, zero], axis=2).reshape(seq_len, HEAD_DIM)
    spos64 = np.stack([zero, sin], axis=2).reshape(seq_len, HEAD_DIM)
    return tuple(jnp.asarray(np.tile(tb, (1, 2))) for tb in (c64, sneg64, spos64))


def _prep_params(w_in, w_out, norm_mix, norm_ffn, sc_conv, ssm_lam_re, ssm_lam_im, ssm_log_dt, ssm_b_re,
                 ssm_b_im, ssm_c_re, ssm_c_im, ssm_d, ssm_w_glu, ffn_w_up, ffn_conv, ffn_w_down):
    b_slab = lambda b: jnp.transpose(b, (0, 1, 4, 2, 3)).reshape(DEPTH, 2, SSM_CH, SSM_N)
    c_slab = lambda c: jnp.transpose(c, (0, 1, 3, 2, 4)).reshape(DEPTH, 2, SSM_CH, SSM_N)
    return {
        "w_in": w_in,
        "norm_mix": norm_mix.reshape(DEPTH, 1, D),
        "norm_ffn": norm_ffn.reshape(DEPTH, 1, D),
        "sc_conv": jnp.transpose(sc_conv, (0, 2, 1)),
        "ssm_d": ssm_d.reshape(DEPTH, 1, SSM_DIM),
        "w_glu": ssm_w_glu,
        "w_out": w_out,
        "w_up": ffn_w_up.astype(BF16),
        "ffn_conv": jnp.transpose(ffn_conv, (0, 2, 1)),
        "w_down": ffn_w_down.astype(BF16),
        "lam_re": ssm_lam_re.reshape(DEPTH, 2, 1, SSM_N),
        "lam_im": ssm_lam_im.reshape(DEPTH, 2, 1, SSM_N),
        "log_dt": jnp.repeat(ssm_log_dt, SSM_STATE, axis=-1).reshape(DEPTH, 2, 1, SSM_N),
        "b_re": b_slab(ssm_b_re),
        "b_im": b_slab(ssm_b_im),
        "c_re": c_slab(ssm_c_re),
        "c_im": c_slab(ssm_c_im),
    }


def kernel(x_prompt, x_sample, cache_k, cache_v, state_ssm_re, state_ssm_im, c, c_ctx, norm_mix, norm_ffn, norm_final, w_ada, b_ada, w_in, w_out, attn_sink, sc_conv, ssm_lam_re, ssm_lam_im, ssm_log_dt, ssm_b_re, ssm_b_im, ssm_c_re, ssm_c_im, ssm_d, ssm_w_glu, ffn_w_up, ffn_conv, ffn_w_down):
    batch, seq = x_prompt.shape[0], x_prompt.shape[1]
    dec_batch, dec_seq = x_sample.shape[0], x_sample.shape[1]
    assert batch == SCAN_ROWS and seq == SCAN_LEN
    assert dec_batch * (dec_seq // SCAN_LEN) == SCAN_ROWS and dec_batch == 2

    cs = jnp.concatenate([c_ctx[None, :], c, jnp.zeros((8 - 1 - dec_batch, D), F32)], axis=0)
    mods = _adaln(cs, w_ada, b_ada).reshape(DEPTH, 8, 1, 6 * D)
    rope_tabs = _rope_tables(dec_seq)
    gfin = norm_final.reshape(1, D)
    prm = _prep_params(w_in, w_out, norm_mix, norm_ffn, sc_conv, ssm_lam_re, ssm_lam_im, ssm_log_dt, ssm_b_re,
                       ssm_b_im, ssm_c_re, ssm_c_im, ssm_d, ssm_w_glu, ffn_w_up, ffn_conv, ffn_w_down)
    s0_all = jnp.transpose(jnp.concatenate([state_ssm_re.reshape(dec_batch, DEPTH, 2, SSM_N),
                                            state_ssm_im.reshape(dec_batch, DEPTH, 2, SSM_N)], axis=-1),
                           (1, 2, 0, 3))

    xp = x_prompt.reshape(batch * seq, D)
    xs = x_sample.reshape(dec_batch * dec_seq, D)
    kv_out, fin_out = [], []
    for l in range(DEPTH):
        last = gfin if l == DEPTH - 1 else None

        q, kd, vd, kv, g3, su = _inproj(xp, mods, prm, l, None, batch * seq, seq)
        attn = _attn_ctx(attn_sink, l, q, kd, vd, seq)
        yf, yb, fin = _scan(prm, l, su, True)
        x1, h2 = _mix(xp, mods, attn, g3, su, yf, yb, prm, l, False, batch * seq, seq)
        xp = _ffn(h2, x1, mods, prm, l, False, batch * seq, seq, last)
        kv_out.append(kv)
        fin_out.append(fin)

        q, kd, vd, g3, su = _inproj(xs, mods, prm, l, rope_tabs, dec_seq, dec_seq)
        attn = _attn_lat(attn_sink, l, q, kd, vd, cache_k, cache_v, dec_batch, dec_seq)
        (ez,) = _scan(prm, l, su, False)
        yf, yb, _ = _scan(prm, l, su, True, s0=s0_all[l], ez=ez)
        x1, h2 = _mix(xs, mods, attn, g3, su, yf, yb, prm, l, True, dec_seq, dec_seq)
        xs = _ffn(h2, x1, mods, prm, l, True, dec_seq, dec_seq, last)

    kv_all = jnp.stack(kv_out, axis=0).reshape(DEPTH, batch, seq, 2, KV_HEADS, HEAD_DIM)
    kv_all = jnp.transpose(kv_all, (3, 1, 0, 2, 4, 5))
    fin_all = jnp.stack(fin_out, axis=0).reshape(DEPTH, 2, batch, 2, SSM_GROUPS, SSM_STATE)
    fin_all = jnp.transpose(fin_all, (3, 2, 0, 1, 4, 5))
    return (xp.reshape(batch, seq, D), xs.reshape(dec_batch, dec_seq, D),
            kv_all[0], kv_all[1], fin_all[0], fin_all[1])
```

```python
import functools
import math

import jax
import jax.numpy as jnp
import numpy as np
from jax import lax
from jax.experimental import pallas as pl
from jax.experimental.pallas import tpu as pltpu

F32 = jnp.float32
BF16 = jnp.bfloat16

D = 1024
DEPTH = 2
GRID_W = 64
ATTN_DIM = 512
SC_DIM = 256
SSM_DIM = 256
HEAD_DIM = 64
N_HEADS = 8
KV_HEADS = 2
KV_DIM = 128
WINDOW = 128
Q_BLOCK = 128
ROPE_BASE = 10000.0
ROPE_FREQS = 16
SSM_CH = 16
SSM_GROUPS = 16
SSM_STATE = 64
SSM_N = SSM_GROUPS * SSM_STATE
IN_DIM = ATTN_DIM + 2 * KV_DIM + 3 * SC_DIM + SSM_DIM
D_FF = 2816
LOG2E = math.log2(math.e)
Q_SCALE = LOG2E / math.sqrt(HEAD_DIM)
RMS_EPS = 1e-6
NEG_BIG = -1e30

LANES = 128
SCAN_ROWS = 16
SCAN_LEN = 256
SCAN_TT = 64
SCAN_LW = 256
SCAN_MM_ROWS = 1024
FFN_TF = 1408
FFN_SUB = (768, 512, 128)
assert D_FF % FFN_TF == 0 and sum(FFN_SUB) == FFN_TF
VMEM_LIMIT = 56 * 1024 * 1024


def _cparams(sem):
    return pltpu.CompilerParams(dimension_semantics=sem, vmem_limit_bytes=VMEM_LIMIT)


def _sigmoid(x):
    return 1.0 / (1.0 + jnp.exp(-x))


def _rms_mod(x, g, scale, shift):
    y = x * lax.rsqrt(jnp.mean(x * x, axis=-1, keepdims=True) + RMS_EPS) * g
    return y * (1.0 + scale) + shift


def _split_bf16(v):
    hi = v.astype(BF16)
    lo = (v - hi.astype(F32)).astype(BF16)
    return hi, lo


def _dot(a, b):
    return jnp.dot(a, b, preferred_element_type=F32)


def _dot_t(a, b):
    return lax.dot_general(a, b, (((1,), (1,)), ((), ())), preferred_element_type=F32)


def _layer_spec(shape, l):
    zeros = (0,) * len(shape)
    return pl.BlockSpec((None,) + tuple(shape), lambda *_: (l,) + zeros)


def _half_tiles(t):
    lo = lax.broadcasted_iota(jnp.int32, t.shape, 1) < HEAD_DIM
    tr = pltpu.roll(t, HEAD_DIM, 1)
    return [jnp.where(lo, t, 0.0).astype(BF16), jnp.where(lo, 0.0, tr).astype(BF16),
            jnp.where(lo, tr, 0.0).astype(BF16), jnp.where(lo, 0.0, t).astype(BF16)]


def _adaln_kernel(c_ref, w_ref, b_ref, o_ref):
    c = c_ref[...]
    s = c * _sigmoid(c)
    s_hi, s_lo = _split_bf16(s)
    w_hi, w_lo = _split_bf16(w_ref[...])
    o_ref[...] = _dot(s_hi, w_hi) + _dot(s_lo, w_hi) + _dot(s_hi, w_lo) + b_ref[...]


def _adaln(cs, w_ada, b_ada):
    tn = 3072
    return pl.pallas_call(
        _adaln_kernel,
        grid=(DEPTH, 6 * D // tn),
        in_specs=[
            pl.BlockSpec((8, D), lambda l, j: (0, 0)),
            pl.BlockSpec((None, D, tn), lambda l, j: (l, 0, j)),
            pl.BlockSpec((None, 1, tn), lambda l, j: (l, 0, j)),
        ],
        out_specs=pl.BlockSpec((None, 8, tn), lambda l, j: (l, 0, j)),
        out_shape=jax.ShapeDtypeStruct((DEPTH, 8, 6 * D), F32),
        compiler_params=_cparams(("parallel", "parallel")),
        name="adaln",
    )(cs, w_ada, b_ada.reshape(DEPTH, 1, 6 * D))


def _inproj_kernel(x_ref, mod_ref, g_ref, w_ref, *rest, rope):
    if rope:
        cos_ref, sneg_ref, spos_ref, q_ref, kd_ref, vd_ref, g3_ref, su_ref, wbf_scr = rest
    else:
        q_ref, kd_ref, vd_ref, kv_ref, g3_ref, su_ref, wbf_scr = rest

    @pl.when(pl.program_id(0) == 0)
    def _cast_weights():
        wbf_scr[...] = w_ref[...].astype(BF16)

    m = mod_ref[...]
    h = _rms_mod(x_ref[...], g_ref[...], m[:, D:2 * D], m[:, 0:D]).astype(BF16)
    acc = _dot(h, wbf_scr[...])
    k = acc[:, 512:640]
    v = acc[:, 640:768]
    if rope:
        c, sn, sp = cos_ref[...], sneg_ref[...], spos_ref[...]

        def rot(t):
            return t * c + pltpu.roll(t, LANES - ROPE_FREQS, 1) * sn + pltpu.roll(t, ROPE_FREQS, 1) * sp

        for p in range(4):
            q_ref[:, LANES * p:LANES * (p + 1)] = (rot(acc[:, LANES * p:LANES * (p + 1)]) * Q_SCALE).astype(BF16)
        k = rot(k)
    else:
        q_ref[...] = (acc[:, 0:512] * Q_SCALE).astype(BF16)
        kv_ref[...] = acc[:, 512:768]
    for ref, t in ((kd_ref, k), (vd_ref, v)):
        for p, tile in enumerate(_half_tiles(t)):
            ref[:, LANES * p:LANES * (p + 1)] = tile
    g3_ref[...] = acc[:, 768:1536].astype(BF16)
    for r in range(acc.shape[0] // SCAN_LEN):
        su_ref[:, r * SSM_DIM:(r + 1) * SSM_DIM] = acc[r * SCAN_LEN:(r + 1) * SCAN_LEN, 1536:1792]


def _inproj(x, mods, prm, l, rope_tabs, tokens_per_mod, seq_len, tm=512):
    t = x.shape[0]
    rope = rope_tabs is not None
    tiles_per_mod = tokens_per_mod // tm
    tiles_per_seq = seq_len // tm
    if rope:
        mod_map = lambda i: (l, 1 + i // tiles_per_mod, 0, 0)
    else:
        mod_map = lambda i: (l, 0, 0, 0)
    row = lambda i: (i, 0)
    in_specs = [
        pl.BlockSpec((tm, D), row),
        pl.BlockSpec((None, None, 1, 6 * D), mod_map),
        _layer_spec((1, D), l),
        _layer_spec((D, IN_DIM), l),
    ]
    args = [x, mods, prm["norm_mix"], prm["w_in"]]
    if rope:
        in_specs += [pl.BlockSpec((tm, LANES), lambda i: (i % tiles_per_seq, 0))] * 3
        args += list(rope_tabs)
    out_shape = [jax.ShapeDtypeStruct((t, 512), BF16)] * 3
    out_specs = [pl.BlockSpec((tm, 512), row)] * 3
    if not rope:
        out_shape.append(jax.ShapeDtypeStruct((t, 256), F32))
        out_specs.append(pl.BlockSpec((tm, 256), row))
    out_shape += [jax.ShapeDtypeStruct((t, 768), BF16),
                  jax.ShapeDtypeStruct((SCAN_LEN, t // SCAN_LEN * SSM_DIM), F32)]
    out_specs += [pl.BlockSpec((tm, 768), row),
                  pl.BlockSpec((SCAN_LEN, tm // SCAN_LEN * SSM_DIM), lambda i: (0, i))]
    return pl.pallas_call(
        functools.partial(_inproj_kernel, rope=rope),
        grid=(t // tm,),
        in_specs=in_specs,
        out_specs=out_specs,
        out_shape=out_shape,
        scratch_shapes=[pltpu.VMEM((D, IN_DIM), BF16)],
        compiler_params=_cparams(("arbitrary",)),
        name="inproj_lat" if rope else "inproj_ctx",
    )(*args)


def _attention(sink_ref, l, q_ref, o_ref, keys, vals, bias, nq):
    top = lax.broadcasted_iota(jnp.int32, (2 * nq, 1), 0) < nq
    scores = []
    for kvh in range(KV_HEADS):
        q2 = jnp.concatenate([q_ref[:, LANES * (2 * kvh + pp):LANES * (2 * kvh + pp + 1)] for pp in range(2)],
                             axis=0)
        scores.append([_dot_t(q2, keys[kvh][half]) for half in range(2)])
    for kvh in range(KV_HEADS):
        acc = None
        for half in range(2):
            sk = jnp.where(top, sink_ref[l, 4 * kvh + half], sink_ref[l, 4 * kvh + 2 + half]) * LOG2E
            s = scores[kvh][half]
            if bias is not None:
                nb_ = bias.shape[1]
                s = jnp.concatenate([s[:, 0:nb_] + bias, s[:, nb_:]], axis=1)
            m = jnp.maximum(jnp.max(s, axis=-1, keepdims=True), sk)
            e = jnp.exp2(s - m)
            den = jnp.sum(e, axis=-1, keepdims=True) + jnp.exp2(sk - m)
            o = _dot(e.astype(BF16), vals[kvh][half]) / den
            acc = o if acc is None else acc + o
        o_ref[:, LANES * 2 * kvh:LANES * (2 * kvh + 1)] = acc[0:nq].astype(BF16)
        o_ref[:, LANES * (2 * kvh + 1):LANES * (2 * kvh + 2)] = acc[nq:2 * nq].astype(BF16)


def _kv_tiles(kvh):
    return [slice(LANES * (2 * kvh + h), LANES * (2 * kvh + h + 1)) for h in range(2)]


def _attn_ctx_kernel(sink_ref, q_ref, kd_ref, vd_ref, o_ref, *, l):
    keys = [[kd_ref[:, t] for t in _kv_tiles(kvh)] for kvh in range(KV_HEADS)]
    vals = [[vd_ref[:, t] for t in _kv_tiles(kvh)] for kvh in range(KV_HEADS)]
    _attention(sink_ref, l, q_ref, o_ref, keys, vals, None, q_ref.shape[0])


def _attn_ctx(sink, l, q, kd, vd, seq_len):
    t = q.shape[0]
    row = lambda b: (b, 0)
    return pl.pallas_call(
        functools.partial(_attn_ctx_kernel, l=l),
        grid=(t // seq_len,),
        in_specs=[
            pl.BlockSpec(memory_space=pltpu.SMEM),
            pl.BlockSpec((seq_len, 512), row),
            pl.BlockSpec((seq_len, 512), row),
            pl.BlockSpec((seq_len, 512), row),
        ],
        out_specs=pl.BlockSpec((seq_len, 512), row),
        out_shape=jax.ShapeDtypeStruct((t, 512), BF16),
        compiler_params=_cparams(("parallel",)),
        name="attn_ctx",
    )(sink, q, kd, vd)


def _attn_lat_kernel(sink_ref, q_ref, kp_ref, kc_ref, kn_ref, vp_ref, vc_ref, vn_ref, ck_ref, cv_ref, o_ref,
                     ckt_scr, cvt_scr, *, l):
    i = pl.program_id(1)
    nb = pl.num_programs(1)

    @pl.when(i == 0)
    def _context_tiles():
        for p, (kt, vt) in enumerate(zip(_half_tiles(ck_ref[...]), _half_tiles(cv_ref[...]))):
            ckt_scr[p] = kt
            cvt_scr[p] = vt

    r = lax.broadcasted_iota(jnp.int32, (Q_BLOCK, Q_BLOCK), 0)
    j = lax.broadcasted_iota(jnp.int32, (Q_BLOCK, Q_BLOCK), 1)
    bias = jnp.concatenate([
        jnp.where(jnp.logical_and(j >= r, i > 0), 0.0, NEG_BIG),
        jnp.where(jnp.logical_and(j <= r, i < nb - 1), 0.0, NEG_BIG)], axis=1)
    bias = jnp.concatenate([bias, bias], axis=0)
    def gather(refs, ctx_scr, kvh, h):
        t = _kv_tiles(kvh)[h]
        return jnp.concatenate([ref[:, t] for ref in refs] + [ctx_scr[2 * kvh + h]], axis=0)

    keys = [[gather((kp_ref, kn_ref, kc_ref), ckt_scr, kvh, h) for h in range(2)] for kvh in range(KV_HEADS)]
    vals = [[gather((vp_ref, vn_ref, vc_ref), cvt_scr, kvh, h) for h in range(2)] for kvh in range(KV_HEADS)]
    _attention(sink_ref, l, q_ref, o_ref, keys, vals, bias, Q_BLOCK)


def _attn_lat(sink, l, q, kd, vd, cache_k, cache_v, batch, seq_len):
    nb = seq_len // Q_BLOCK
    past = cache_k.shape[2]
    cur = lambda b, i: (b * nb + i, 0)
    prev = lambda b, i: (b * nb + jnp.maximum(i - 1, 0), 0)
    nxt = lambda b, i: (b * nb + jnp.minimum(i + 1, nb - 1), 0)
    kvspec = lambda f: pl.BlockSpec((Q_BLOCK, 512), f)
    cspec = pl.BlockSpec((None, None, past, KV_DIM), lambda b, i: (b, l, 0, 0))
    ck = cache_k.reshape(batch, DEPTH, past, KV_DIM)
    cv = cache_v.reshape(batch, DEPTH, past, KV_DIM)
    return pl.pallas_call(
        functools.partial(_attn_lat_kernel, l=l),
        grid=(batch, nb),
        in_specs=[
            pl.BlockSpec(memory_space=pltpu.SMEM),
            pl.BlockSpec((Q_BLOCK, 512), cur),
            kvspec(prev), kvspec(cur), kvspec(nxt),
            kvspec(prev), kvspec(cur), kvspec(nxt),
            cspec, cspec,
        ],
        out_specs=pl.BlockSpec((Q_BLOCK, 512), cur),
        out_shape=jax.ShapeDtypeStruct((batch * seq_len, 512), BF16),
        scratch_shapes=[pltpu.VMEM((2 * KV_HEADS, past, LANES), BF16)] * 2,
        compiler_params=_cparams(("parallel", "arbitrary")),
        name="attn_lat",
    )(sink, q, kd, kd, kd, vd, vd, vd, ck, cv)


def _scan_kernel(lre_ref, lim_ref, ldt_ref, bre_ref, bim_ref, cre_ref, cim_ref, *rest, emit_y, chain):
    rest = list(rest)
    if chain:
        s0_ref, ez_ref = rest[:2]
        rest = rest[2:]
    suf_ref, sub_ref = rest[:2]
    rest = rest[2:]
    if emit_y:
        y_refs = rest[:2]
        rest = rest[2:]
    fin_ref, a_scr, bb_scr, h_scr, bu_scr, u_scr = rest[:6]
    if emit_y:
        y_scr, ct_scr = rest[6:8]
    i = pl.program_id(0)
    tt = SCAN_TT
    n = SSM_N

    @pl.when(i == 0)
    def _prologue():
        row_g = lax.shift_right_logical(lax.broadcasted_iota(jnp.int32, (SSM_DIM, n), 0), 4)
        col_g = lax.shift_right_logical(lax.broadcasted_iota(jnp.int32, (SSM_DIM, n), 1), 6)
        own = row_g == col_g

        def blockdiag(ref, d):
            return jnp.where(own, jnp.concatenate([ref[d]] * SSM_GROUPS, axis=0), 0.0)

        for d in range(2):
            lr, li = lre_ref[d], lim_ref[d]
            dt = jnp.exp(ldt_ref[d])
            mag = jnp.exp(lr * dt)
            ar, ai = mag * jnp.cos(li * dt), mag * jnp.sin(li * dt)
            den = lr * lr + li * li
            fr = ((ar - 1.0) * lr + ai * li) / den
            fi = (ai * lr - (ar - 1.0) * li) / den
            bre, bim = blockdiag(bre_ref, d), blockdiag(bim_ref, d)
            bb_scr[d, :, 0:n] = (fr * bre - fi * bim).astype(BF16)
            bb_scr[d, :, n:2 * n] = (fr * bim + fi * bre).astype(BF16)
            if emit_y:
                ct_scr[d, :, 0:n] = blockdiag(cre_ref, d).astype(BF16)
                ct_scr[d, :, n:2 * n] = (-blockdiag(cim_ref, d)).astype(BF16)
            a_scr[d, 0] = jnp.broadcast_to(ar, (8, n))
            a_scr[d, 1] = jnp.broadcast_to(ai, (8, n))
            if chain:
                pr, pi_ = ar, ai
                for _ in range(8):
                    pr, pi_ = pr * pr - pi_ * pi_, 2.0 * pr * pi_
                chunks = SCAN_ROWS // 2
                for b in range(2):
                    hr = s0_ref[d, b:b + 1, 0:n]
                    hi = s0_ref[d, b:b + 1, n:2 * n]
                    order = range(chunks) if d == 0 else range(chunks - 1, -1, -1)
                    for c in order:
                        rw = b * chunks + c
                        h_scr[d, rw:rw + 1, 0:n] = hr
                        h_scr[d, rw:rw + 1, n:2 * n] = hi
                        er = ez_ref[d, rw:rw + 1, 0:n]
                        ei = ez_ref[d, rw:rw + 1, n:2 * n]
                        hr, hi = pr * hr - pi_ * hi + er, pr * hi + pi_ * hr + ei
            else:
                h_scr[d] = jnp.zeros((SCAN_ROWS, 2 * n), F32)

    ntile = n // LANES
    group = SCAN_LW // LANES

    def project_in(d):
        su_ref = suf_ref if d == 0 else sub_ref
        for r in range(SCAN_ROWS):
            for sl in range(SSM_DIM // LANES):
                col = r * SSM_DIM + sl * LANES
                u_scr[d, sl, pl.ds(r, tt, stride=SCAN_ROWS), :] = su_ref[:, col:col + LANES]
        for r0 in range(0, tt * SCAN_ROWS, SCAN_MM_ROWS):
            rows = slice(r0, r0 + SCAN_MM_ROWS)
            u = jnp.concatenate([u_scr[d, sl, rows, :] for sl in range(SSM_DIM // LANES)], axis=1).astype(BF16)
            bu_scr[d, rows, :] = _dot(u, bb_scr[d])

    def recur(d):
        for c0 in range(0, ntile, group):
            lre = slice(c0 * LANES, (c0 + group) * LANES)
            lim = slice(n + c0 * LANES, n + (c0 + group) * LANES)
            ar, ai = a_scr[d, 0, :, lre], a_scr[d, 1, :, lre]
            halves = [slice(0, 8), slice(8, 16)]
            hr = [h_scr[d, hs, lre] for hs in halves]
            hi = [h_scr[d, hs, lim] for hs in halves]
            steps = range(tt) if d == 0 else range(tt - 1, -1, -1)
            for s in steps:
                for k in range(2):
                    rows = slice(s * SCAN_ROWS + 8 * k, s * SCAN_ROWS + 8 * k + 8)
                    br, bi = bu_scr[d, rows, lre], bu_scr[d, rows, lim]
                    hr[k], hi[k] = ar * hr[k] - ai * hi[k] + br, ar * hi[k] + ai * hr[k] + bi
                    if emit_y:
                        bu_scr[d, rows, lre] = hr[k]
                        bu_scr[d, rows, lim] = hi[k]
            for k, hs in enumerate(halves):
                h_scr[d, hs, lre] = hr[k]
                h_scr[d, hs, lim] = hi[k]

    def project_out(d):
        for r0 in range(0, tt * SCAN_ROWS, SCAN_MM_ROWS):
            rows = slice(r0, r0 + SCAN_MM_ROWS)
            y = _dot_t(bu_scr[d, rows, :].astype(BF16), ct_scr[d])
            for sl in range(SSM_DIM // LANES):
                y_scr[d, sl, rows, :] = y[:, sl * LANES:(sl + 1) * LANES]
        for r in range(SCAN_ROWS):
            for sl in range(SSM_DIM // LANES):
                col = r * SSM_DIM + sl * LANES
                y_refs[d][:, col:col + LANES] = y_scr[d, sl, pl.ds(r, tt, stride=SCAN_ROWS), :].astype(BF16)

    project_in(0)
    project_in(1)
    recur(0)
    if emit_y:
        project_out(0)
    recur(1)
    if emit_y:
        project_out(1)

    @pl.when(i == pl.num_programs(0) - 1)
    def _final():
        fin_ref[...] = h_scr[...]


def _scan(prm, l, su_tm, emit_y, s0=None, ez=None):
    chain = s0 is not None
    nt = SCAN_LEN // SCAN_TT
    n = SSM_N
    full3 = lambda shape: pl.BlockSpec(shape, lambda i: (0, 0, 0))
    in_specs = [_layer_spec((2, 1, n), l)] * 3 + [_layer_spec((2, SSM_CH, n), l)] * 4
    args = [prm["lam_re"], prm["lam_im"], prm["log_dt"], prm["b_re"], prm["b_im"], prm["c_re"], prm["c_im"]]
    if chain:
        in_specs += [full3((2, 2, 2 * n)), full3((2, SCAN_ROWS, 2 * n))]
        args += [s0, ez]
    tblk = (SCAN_TT, SCAN_ROWS * SSM_DIM)
    fwd = lambda i: (i, 0)
    bwd = lambda i: (nt - 1 - i, 0)
    in_specs += [pl.BlockSpec(tblk, fwd), pl.BlockSpec(tblk, bwd)]
    args += [su_tm, su_tm]
    out_shape, out_specs = [], []
    scratch = [
        pltpu.VMEM((2, 2, 8, n), F32),
        pltpu.VMEM((2, SSM_DIM, 2 * n), BF16),
        pltpu.VMEM((2, SCAN_ROWS, 2 * n), F32),
        pltpu.VMEM((2, SCAN_ROWS * SCAN_TT, 2 * n), F32),
        pltpu.VMEM((2, SSM_DIM // LANES, SCAN_ROWS * SCAN_TT, LANES), F32),
    ]
    if emit_y:
        yshape = jax.ShapeDtypeStruct((SCAN_LEN, SCAN_ROWS * SSM_DIM), BF16)
        out_shape += [yshape, yshape]
        out_specs += [pl.BlockSpec(tblk, fwd), pl.BlockSpec(tblk, bwd)]
        scratch.append(pltpu.VMEM((2, SSM_DIM // LANES, SCAN_ROWS * SCAN_TT, LANES), F32))
        scratch.append(pltpu.VMEM((2, SSM_DIM, 2 * n), BF16))
    out_shape.append(jax.ShapeDtypeStruct((2, SCAN_ROWS, 2 * n), F32))
    out_specs.append(full3((2, SCAN_ROWS, 2 * n)))
    return pl.pallas_call(
        functools.partial(_scan_kernel, emit_y=emit_y, chain=chain),
        grid=(nt,),
        in_specs=in_specs,
        out_specs=out_specs,
        out_shape=out_shape,
        scratch_shapes=scratch,
        compiler_params=_cparams(("arbitrary",)),
        name="scan_chain" if chain else ("scan_y" if emit_y else "scan_state"),
    )(*args)


def _shifted(scr_ref, off, rows, pos, seq_len, w):
    mid = scr_ref[off:off + rows, :]
    up = jnp.where(pos != 0, scr_ref[off - 1:off - 1 + rows, :], 0.0)
    dn = jnp.where(pos != seq_len - 1, scr_ref[off + 1:off + 1 + rows, :], 0.0)
    return w[0:1, :] * up + w[1:2, :] * mid + w[2:3, :] * dn


def _mix_kernel(x_ref, mod_ref, attn_ref, g3_ref, *rest, halo, tm, seq_len):
    rest = list(rest)
    if halo:
        gp_ref, gn_ref = rest[:2]
        rest = rest[2:]
    (su_ref, yf_ref, yb_ref, wc_ref, dsk_ref, wglu_ref, wo_ref, gn2_ref,
     x1_ref, h2_ref, z_scr, wglu_scr, wo_scr) = rest
    i = pl.program_id(0)

    @pl.when(i == 0)
    def _cast_weights():
        wglu_scr[...] = wglu_ref[...].astype(BF16)
        wo_scr[...] = wo_ref[...].astype(BF16)

    def rows_of(ref):
        pieces = [ref[:, r * SSM_DIM:(r + 1) * SSM_DIM].astype(F32) for r in range(tm // SCAN_LEN)]
        return pieces[0] if len(pieces) == 1 else jnp.concatenate(pieces, axis=0)

    m = mod_ref[...]
    g1, sh2, sc2 = m[:, 2 * D:3 * D], m[:, 3 * D:4 * D], m[:, 4 * D:5 * D]
    pos = (i * tm + lax.broadcasted_iota(jnp.int32, (tm, 1), 0)) % seq_len

    def gate_prod(ref):
        return ref[:, SC_DIM:2 * SC_DIM].astype(F32) * ref[:, 2 * SC_DIM:3 * SC_DIM].astype(F32)

    gb = g3_ref[:, 0:SC_DIM].astype(F32)
    z_scr[16:16 + tm, :] = gate_prod(g3_ref)
    if halo:
        z_scr[0:16, :] = gate_prod(gp_ref)
        z_scr[16 + tm:32 + tm, :] = gate_prod(gn_ref)
    else:
        z_scr[0:16, :] = jnp.zeros((16, SC_DIM), F32)
        z_scr[16 + tm:32 + tm, :] = jnp.zeros((16, SC_DIM), F32)
    conv = gb * _shifted(z_scr, 16, tm, pos, seq_len, wc_ref[...])

    y = dsk_ref[...] * rows_of(su_ref) + rows_of(yf_ref) + rows_of(yb_ref)
    zz = 0.5 * y * (1.0 + jnp.tanh(math.sqrt(2.0 / math.pi) * (y + 0.044715 * (y * y * y))))
    ssm = zz * _sigmoid(_dot(zz.astype(BF16), wglu_scr[...]))

    mix = (_dot(attn_ref[...], wo_scr[0:ATTN_DIM, :])
           + _dot(conv.astype(BF16), wo_scr[ATTN_DIM:ATTN_DIM + SC_DIM, :])
           + _dot(ssm.astype(BF16), wo_scr[ATTN_DIM + SC_DIM:, :]))
    x1 = x_ref[...] + g1 * mix
    x1_ref[...] = x1
    h2_ref[...] = _rms_mod(x1, gn2_ref[...], sc2, sh2).astype(BF16)


def _mix(x, mods, attn, g3, su, yf, yb, prm, l, lat, tokens_per_mod, seq_len, tm=512):
    t = x.shape[0]
    halo = seq_len > tm
    tiles_per_mod = tokens_per_mod // tm
    mod_map = (lambda i: (l, 1 + i // tiles_per_mod, 0, 0)) if lat else (lambda i: (l, 0, 0, 0))
    row = lambda i: (i, 0)
    in_specs = [pl.BlockSpec((tm, D), row), pl.BlockSpec((None, None, 1, 6 * D), mod_map),
                pl.BlockSpec((tm, 512), row), pl.BlockSpec((tm, 768), row)]
    args = [x, mods, attn, g3]
    if halo:
        r16 = tm // 16
        in_specs += [pl.BlockSpec((16, 768), lambda i: (jnp.maximum(i * r16 - 1, 0), 0)),
                     pl.BlockSpec((16, 768), lambda i: (jnp.minimum((i + 1) * r16, t // 16 - 1), 0))]
        args += [g3, g3]
    assert tm % SCAN_LEN == 0
    in_specs += [pl.BlockSpec((SCAN_LEN, tm // SCAN_LEN * SSM_DIM), lambda i: (0, i))] * 3
    args += [su, yf, yb]
    in_specs += [_layer_spec((3, SC_DIM), l), _layer_spec((1, SSM_DIM), l), _layer_spec((SSM_DIM, SSM_DIM), l),
                 _layer_spec((D, D), l), _layer_spec((1, D), l)]
    args += [prm["sc_conv"], prm["ssm_d"], prm["w_glu"], prm["w_out"], prm["norm_ffn"]]
    return pl.pallas_call(
        functools.partial(_mix_kernel, halo=halo, tm=tm, seq_len=seq_len),
        grid=(t // tm,),
        in_specs=in_specs,
        out_specs=[pl.BlockSpec((tm, D), row), pl.BlockSpec((tm, D), row)],
        out_shape=[jax.ShapeDtypeStruct((t, D), F32), jax.ShapeDtypeStruct((t, D), BF16)],
        scratch_shapes=[pltpu.VMEM((tm + 32, SC_DIM), F32), pltpu.VMEM((SSM_DIM, SSM_DIM), BF16),
                        pltpu.VMEM((D, D), BF16)],
        compiler_params=_cparams(("arbitrary",)),
        name="mix_lat" if lat else "mix_ctx",
    )(*args)


def _ffn_kernel(h_ref, *rest, halo, tm, seq_len, final):
    rest = list(rest)
    if halo:
        hp_ref, hn_ref = rest[:2]
        rest = rest[2:]
    x1_ref, mod_ref, wa_ref, wg_ref, ca_ref, cg_ref, wd_ref = rest[:7]
    rest = rest[7:]
    if final:
        gf_ref = rest[0]
        rest = rest[1:]
    o_ref, acc_scr = rest[:2]
    rest = rest[2:]
    nsub = len(FFN_SUB)
    ua_scrs, ug_scrs = rest[:nsub], rest[nsub:2 * nsub]
    i = pl.program_id(0)
    j = pl.program_id(1)

    if halo:
        hcat_scr = rest[2 * nsub]

        @pl.when(j == 0)
        def _stage():
            hcat_scr[0:16, :] = hp_ref[...]
            hcat_scr[16:16 + tm, :] = h_ref[...]
            hcat_scr[16 + tm:32 + tm, :] = hn_ref[...]

        hh = hcat_scr[...]
        pad, seg_len, nseg = 16, tm, 1
        tiles_per_seq = seq_len // tm
        keep_prev = (i % tiles_per_seq != 0).astype(F32)
        keep_next = (i % tiles_per_seq != tiles_per_seq - 1).astype(F32)
    else:
        hh = h_ref[...]
        pad, seg_len, nseg = 8, seq_len, tm // seq_len
    seg_rows = seg_len + 2 * pad
    cols = [sum(FFN_SUB[:c]) for c in range(nsub)]

    def up(c):
        width = FFN_SUB[c]
        for u_scr, w_ref in ((ua_scrs[c], wa_ref), (ug_scrs[c], wg_ref)):
            u = _dot(hh, w_ref[:, cols[c]:cols[c] + width])
            if halo:
                u_scr[...] = u
                u_scr[8:16, :] = u_scr[8:16, :] * keep_prev
                u_scr[16 + tm:24 + tm, :] = u_scr[16 + tm:24 + tm, :] * keep_next
            else:
                for sg in range(nseg):
                    r0 = sg * seg_rows
                    u_scr[r0:r0 + pad, :] = jnp.zeros((pad, width), F32)
                    u_scr[r0 + pad:r0 + pad + seg_len, :] = u[sg * seg_len:(sg + 1) * seg_len]
                    u_scr[r0 + pad + seg_len:r0 + seg_rows, :] = jnp.zeros((pad, width), F32)

    def conv(c, u_scr, cw_ref):
        w = cw_ref[:, cols[c]:cols[c] + FFN_SUB[c]]
        segs = []
        for sg in range(nseg):
            r0 = sg * seg_rows + pad
            segs.append(w[0:1, :] * u_scr[r0 - 1:r0 - 1 + seg_len, :]
                        + w[1:2, :] * u_scr[r0:r0 + seg_len, :]
                        + w[2:3, :] * u_scr[r0 + 1:r0 + 1 + seg_len, :])
        return segs[0] if nseg == 1 else jnp.concatenate(segs, axis=0)

    def down(c):
        a = conv(c, ua_scrs[c], ca_ref)
        g = conv(c, ug_scrs[c], cg_ref)
        act = (a * (g * _sigmoid(g))).astype(BF16)
        acc_scr[...] += _dot(act, wd_ref[cols[c]:cols[c] + FFN_SUB[c], :])

    @pl.when(j == 0)
    def _init():
        acc_scr[...] = jnp.zeros((tm, D), F32)

    up(0)
    for c in range(nsub):
        if c + 1 < nsub:
            up(c + 1)
        down(c)

    @pl.when(j == pl.num_programs(1) - 1)
    def _finish():
        g2 = mod_ref[...][:, 5 * D:6 * D]
        x2 = x1_ref[...] + g2 * acc_scr[...]
        if final:
            x2 = x2 * lax.rsqrt(jnp.mean(x2 * x2, axis=-1, keepdims=True) + RMS_EPS) * gf_ref[...]
        o_ref[...] = x2


def _ffn(h2, x1, mods, prm, l, lat, tokens_per_mod, seq_len, final_g, tm=512):
    t = h2.shape[0]
    tf = FFN_TF
    halo = seq_len > tm
    final = final_g is not None
    nj = D_FF // tf
    tiles_per_mod = tokens_per_mod // tm
    mod_map = (lambda i, j: (l, 1 + i // tiles_per_mod, 0, 0)) if lat else (lambda i, j: (l, 0, 0, 0))
    row = lambda i, j: (i, 0)
    in_specs = [pl.BlockSpec((tm, D), row)]
    args = [h2]
    if halo:
        r16 = tm // 16
        in_specs += [pl.BlockSpec((16, D), lambda i, j: (jnp.maximum(i * r16 - 1, 0), 0)),
                     pl.BlockSpec((16, D), lambda i, j: (jnp.minimum((i + 1) * r16, t // 16 - 1), 0))]
        args += [h2, h2]
    in_specs += [pl.BlockSpec((tm, D), row), pl.BlockSpec((None, None, 1, 6 * D), mod_map),
                 pl.BlockSpec((None, D, tf), lambda i, j: (l, 0, j)),
                 pl.BlockSpec((None, D, tf), lambda i, j: (l, 0, j + nj)),
                 pl.BlockSpec((None, 3, tf), lambda i, j: (l, 0, j)),
                 pl.BlockSpec((None, 3, tf), lambda i, j: (l, 0, j + nj)),
                 pl.BlockSpec((None, tf, D), lambda i, j: (l, j, 0))]
    args += [x1, mods, prm["w_up"], prm["w_up"], prm["ffn_conv"], prm["ffn_conv"], prm["w_down"]]
    if final:
        in_specs.append(pl.BlockSpec((1, D), lambda i, j: (0, 0)))
        args.append(final_g)
    urows = tm + 32 if halo else (tm // seq_len) * (seq_len + 16)
    scratch = [pltpu.VMEM((tm, D), F32)]
    scratch += [pltpu.VMEM((urows, w), F32) for w in FFN_SUB] * 2
    if halo:
        scratch.append(pltpu.VMEM((tm + 32, D), BF16))
    return pl.pallas_call(
        functools.partial(_ffn_kernel, halo=halo, tm=tm, seq_len=seq_len, final=final),
        grid=(t // tm, nj),
        in_specs=in_specs,
        out_specs=pl.BlockSpec((tm, D), row),
        out_shape=jax.ShapeDtypeStruct((t, D), F32),
        scratch_shapes=scratch,
        compiler_params=_cparams(("parallel", "arbitrary")),
        name="ffn_lat" if lat else "ffn_ctx",
    )(*args)


def _rope_tables(seq_len):
    rows = seq_len // GRID_W
    row = np.repeat(np.arange(rows, dtype=np.float32), GRID_W)
    col = np.tile(np.arange(GRID_W, dtype=np.float32), rows)
    freqs = (np.float32(ROPE_BASE) ** (-np.arange(ROPE_FREQS, dtype=np.float32) / np.float32(ROPE_FREQS)))
    freqs = freqs.astype(np.float32)
    ang = np.stack([row[:, None] * freqs, col[:, None] * freqs], axis=1).astype(np.float32)
    cos, sin = np.cos(ang).astype(np.float32), np.sin(ang).astype(np.float32)
    zero = np.zeros_like(sin)
    c64 = np.stack([cos, cos], axis=2).reshape(seq_len, HEAD_DIM)
    sneg64 = np.stack([-sin, zero], axis=2).reshape(seq_len, HEAD_DIM)
    spos64 = np.stack([zero, sin], axis=2).reshape(seq_len, HEAD_DIM)
    return tuple(jnp.asarray(np.tile(tb, (1, 2))) for tb in (c64, sneg64, spos64))


def _prep_params(w_in, w_out, norm_mix, norm_ffn, sc_conv, ssm_lam_re, ssm_lam_im, ssm_log_dt, ssm_b_re,
                 ssm_b_im, ssm_c_re, ssm_c_im, ssm_d, ssm_w_glu, ffn_w_up, ffn_conv, ffn_w_down):
    b_slab = lambda b: jnp.transpose(b, (0, 1, 4, 2, 3)).reshape(DEPTH, 2, SSM_CH, SSM_N)
    c_slab = lambda c: jnp.transpose(c, (0, 1, 3, 2, 4)).reshape(DEPTH, 2, SSM_CH, SSM_N)
    return {
        "w_in": w_in,
        "norm_mix": norm_mix.reshape(DEPTH, 1, D),
        "norm_ffn": norm_ffn.reshape(DEPTH, 1, D),
        "sc_conv": jnp.transpose(sc_conv, (0, 2, 1)),
        "ssm_d": ssm_d.reshape(DEPTH, 1, SSM_DIM),
        "w_glu": ssm_w_glu,
        "w_out": w_out,
        "w_up": ffn_w_up.astype(BF16),
        "ffn_conv": jnp.transpose(ffn_conv, (0, 2, 1)),
        "w_down": ffn_w_down.astype(BF16),
        "lam_re": ssm_lam_re.reshape(DEPTH, 2, 1, SSM_N),
        "lam_im": ssm_lam_im.reshape(DEPTH, 2, 1, SSM_N),
        "log_dt": jnp.repeat(ssm_log_dt, SSM_STATE, axis=-1).reshape(DEPTH, 2, 1, SSM_N),
        "b_re": b_slab(ssm_b_re),
        "b_im": b_slab(ssm_b_im),
        "c_re": c_slab(ssm_c_re),
        "c_im": c_slab(ssm_c_im),
    }


def kernel(x_prompt, x_sample, cache_k, cache_v, state_ssm_re, state_ssm_im, c, c_ctx, norm_mix, norm_ffn, norm_final, w_ada, b_ada, w_in, w_out, attn_sink, sc_conv, ssm_lam_re, ssm_lam_im, ssm_log_dt, ssm_b_re, ssm_b_im, ssm_c_re, ssm_c_im, ssm_d, ssm_w_glu, ffn_w_up, ffn_conv, ffn_w_down):
    batch, seq = x_prompt.shape[0], x_prompt.shape[1]
    dec_batch, dec_seq = x_sample.shape[0], x_sample.shape[1]
    assert batch == SCAN_ROWS and seq == SCAN_LEN
    assert dec_batch * (dec_seq // SCAN_LEN) == SCAN_ROWS and dec_batch == 2

    cs = jnp.concatenate([c_ctx[None, :], c, jnp.zeros((8 - 1 - dec_batch, D), F32)], axis=0)
    mods = _adaln(cs, w_ada, b_ada).reshape(DEPTH, 8, 1, 6 * D)
    rope_tabs = _rope_tables(dec_seq)
    gfin = norm_final.reshape(1, D)
    prm = _prep_params(w_in, w_out, norm_mix, norm_ffn, sc_conv, ssm_lam_re, ssm_lam_im, ssm_log_dt, ssm_b_re,
                       ssm_b_im, ssm_c_re, ssm_c_im, ssm_d, ssm_w_glu, ffn_w_up, ffn_conv, ffn_w_down)
    s0_all = jnp.transpose(jnp.concatenate([state_ssm_re.reshape(dec_batch, DEPTH, 2, SSM_N),
                                            state_ssm_im.reshape(dec_batch, DEPTH, 2, SSM_N)], axis=-1),
                           (1, 2, 0, 3))

    xp = x_prompt.reshape(batch * seq, D)
    xs = x_sample.reshape(dec_batch * dec_seq, D)
    kv_out, fin_out = [], []
    for l in range(DEPTH):
        last = gfin if l == DEPTH - 1 else None

        q, kd, vd, kv, g3, su = _inproj(xp, mods, prm, l, None, batch * seq, seq)
        attn = _attn_ctx(attn_sink, l, q, kd, vd, seq)
        yf, yb, fin = _scan(prm, l, su, True)
        x1, h2 = _mix(xp, mods, attn, g3, su, yf, yb, prm, l, False, batch * seq, seq)
        xp = _ffn(h2, x1, mods, prm, l, False, batch * seq, seq, last)
        kv_out.append(kv)
        fin_out.append(fin)

        q, kd, vd, g3, su = _inproj(xs, mods, prm, l, rope_tabs, dec_seq, dec_seq)
        attn = _attn_lat(attn_sink, l, q, kd, vd, cache_k, cache_v, dec_batch, dec_seq)
        (ez,) = _scan(prm, l, su, False)
        yf, yb, _ = _scan(prm, l, su, True, s0=s0_all[l], ez=ez)
        x1, h2 = _mix(xs, mods, attn, g3, su, yf, yb, prm, l, True, dec_seq, dec_seq)
        xs = _ffn(h2, x1, mods, prm, l, True, dec_seq, dec_seq, last)

    kv_all = jnp.stack(kv_out, axis=0).reshape(DEPTH, batch, seq, 2, KV_HEADS, HEAD_DIM)
    kv_all = jnp.transpose(kv_all, (3, 1, 0, 2, 4, 5))
    fin_all = jnp.stack(fin_out, axis=0).reshape(DEPTH, 2, batch, 2, SSM_GROUPS, SSM_STATE)
    fin_all = jnp.transpose(fin_all, (3, 2, 0, 1, 4, 5))
    return (xp.reshape(batch, seq, D), xs.reshape(dec_batch, dec_seq, D),
            kv_all[0], kv_all[1], fin_all[0], fin_all[1])
```

```python
import functools
import math

import jax
import jax.numpy as jnp
import numpy as np
from jax import lax
from jax.experimental import pallas as pl
from jax.experimental.pallas import tpu as pltpu

F32 = jnp.float32
BF16 = jnp.bfloat16

D = 1024
DEPTH = 2
GRID_W = 64
ATTN_DIM = 512
SC_DIM = 256
SSM_DIM = 256
HEAD_DIM = 64
N_HEADS = 8
KV_HEADS = 2
KV_DIM = 128
WINDOW = 128
Q_BLOCK = 128
ROPE_BASE = 10000.0
ROPE_FREQS = 16
SSM_CH = 16
SSM_GROUPS = 16
SSM_STATE = 64
SSM_N = SSM_GROUPS * SSM_STATE
IN_DIM = ATTN_DIM + 2 * KV_DIM + 3 * SC_DIM + SSM_DIM
D_FF = 2816
LOG2E = math.log2(math.e)
Q_SCALE = LOG2E / math.sqrt(HEAD_DIM)
RMS_EPS = 1e-6
NEG_BIG = -1e30

LANES = 128
SCAN_ROWS = 16
SCAN_LEN = 256
SCAN_TT = 64
SCAN_LW = 256
SCAN_MM_ROWS = 1024
FFN_TF = 1408
FFN_SUB = (768, 512, 128)
assert D_FF % FFN_TF == 0 and sum(FFN_SUB) == FFN_TF
VMEM_LIMIT = 56 * 1024 * 1024


def _cparams(sem):
    return pltpu.CompilerParams(dimension_semantics=sem, vmem_limit_bytes=VMEM_LIMIT)


def _sigmoid(x):
    return 1.0 / (1.0 + jnp.exp(-x))


def _rms_mod(x, g, scale, shift):
    y = x * lax.rsqrt(jnp.mean(x * x, axis=-1, keepdims=True) + RMS_EPS) * g
    return y * (1.0 + scale) + shift


def _split_bf16(v):
    hi = v.astype(BF16)
    lo = (v - hi.astype(F32)).astype(BF16)
    return hi, lo


def _dot(a, b):
    return jnp.dot(a, b, preferred_element_type=F32)


def _dot_t(a, b):
    return lax.dot_general(a, b, (((1,), (1,)), ((), ())), preferred_element_type=F32)


def _layer_spec(shape, l):
    zeros = (0,) * len(shape)
    return pl.BlockSpec((None,) + tuple(shape), lambda *_: (l,) + zeros)


def _half_tiles(t):
    lo = lax.broadcasted_iota(jnp.int32, t.shape, 1) < HEAD_DIM
    tr = pltpu.roll(t, HEAD_DIM, 1)
    return [jnp.where(lo, t, 0.0).astype(BF16), jnp.where(lo, 0.0, tr).astype(BF16),
            jnp.where(lo, tr, 0.0).astype(BF16), jnp.where(lo, 0.0, t).astype(BF16)]


def _adaln_kernel(c_ref, w_ref, b_ref, o_ref):
    c = c_ref[...]
    s = c * _sigmoid(c)
    s_hi, s_lo = _split_bf16(s)
    w_hi, w_lo = _split_bf16(w_ref[...])
    o_ref[...] = _dot(s_hi, w_hi) + _dot(s_lo, w_hi) + _dot(s_hi, w_lo) + b_ref[...]


def _adaln(cs, w_ada, b_ada):
    tn = 2048
    return pl.pallas_call(
        _adaln_kernel,
        grid=(DEPTH, 6 * D // tn),
        in_specs=[
            pl.BlockSpec((8, D), lambda l, j: (0, 0)),
            pl.BlockSpec((None, D, tn), lambda l, j: (l, 0, j)),
            pl.BlockSpec((None, 1, tn), lambda l, j: (l, 0, j)),
        ],
        out_specs=pl.BlockSpec((None, 8, tn), lambda l, j: (l, 0, j)),
        out_shape=jax.ShapeDtypeStruct((DEPTH, 8, 6 * D), F32),
        compiler_params=_cparams(("parallel", "parallel")),
        name="adaln",
    )(cs, w_ada, b_ada.reshape(DEPTH, 1, 6 * D))


def _inproj_kernel(x_ref, mod_ref, g_ref, w_ref, *rest, rope):
    if rope:
        cos_ref, sneg_ref, spos_ref, q_ref, kd_ref, vd_ref, g3_ref, su_ref, wbf_scr = rest
    else:
        q_ref, kd_ref, vd_ref, kv_ref, g3_ref, su_ref, wbf_scr = rest

    @pl.when(pl.program_id(0) == 0)
    def _cast_weights():
        wbf_scr[...] = w_ref[...].astype(BF16)

    m = mod_ref[...]
    h = _rms_mod(x_ref[...], g_ref[...], m[:, D:2 * D], m[:, 0:D]).astype(BF16)
    acc = _dot(h, wbf_scr[...])
    k = acc[:, 512:640]
    v = acc[:, 640:768]
    if rope:
        c, sn, sp = cos_ref[...], sneg_ref[...], spos_ref[...]

        def rot(t):
            return t * c + pltpu.roll(t, LANES - ROPE_FREQS, 1) * sn + pltpu.roll(t, ROPE_FREQS, 1) * sp

        for p in range(4):
            q_ref[:, LANES * p:LANES * (p + 1)] = (rot(acc[:, LANES * p:LANES * (p + 1)]) * Q_SCALE).astype(BF16)
        k = rot(k)
    else:
        q_ref[...] = (acc[:, 0:512] * Q_SCALE).astype(BF16)
        kv_ref[...] = acc[:, 512:768]
    for ref, t in ((kd_ref, k), (vd_ref, v)):
        for p, tile in enumerate(_half_tiles(t)):
            ref[:, LANES * p:LANES * (p + 1)] = tile
    g3_ref[...] = acc[:, 768:1536].astype(BF16)
    for r in range(acc.shape[0] // SCAN_LEN):
        su_ref[:, r * SSM_DIM:(r + 1) * SSM_DIM] = acc[r * SCAN_LEN:(r + 1) * SCAN_LEN, 1536:1792]


def _inproj(x, mods, prm, l, rope_tabs, tokens_per_mod, seq_len, tm=512):
    t = x.shape[0]
    rope = rope_tabs is not None
    tiles_per_mod = tokens_per_mod // tm
    tiles_per_seq = seq_len // tm
    if rope:
        mod_map = lambda i: (l, 1 + i // tiles_per_mod, 0, 0)
    else:
        mod_map = lambda i: (l, 0, 0, 0)
    row = lambda i: (i, 0)
    in_specs = [
        pl.BlockSpec((tm, D), row),
        pl.BlockSpec((None, None, 1, 6 * D), mod_map),
        _layer_spec((1, D), l),
        _layer_spec((D, IN_DIM), l),
    ]
    args = [x, mods, prm["norm_mix"], prm["w_in"]]
    if rope:
        in_specs += [pl.BlockSpec((tm, LANES), lambda i: (i % tiles_per_seq, 0))] * 3
        args += list(rope_tabs)
    out_shape = [jax.ShapeDtypeStruct((t, 512), BF16)] * 3
    out_specs = [pl.BlockSpec((tm, 512), row)] * 3
    if not rope:
        out_shape.append(jax.ShapeDtypeStruct((t, 256), F32))
        out_specs.append(pl.BlockSpec((tm, 256), row))
    out_shape += [jax.ShapeDtypeStruct((t, 768), BF16),
                  jax.ShapeDtypeStruct((SCAN_LEN, t // SCAN_LEN * SSM_DIM), F32)]
    out_specs += [pl.BlockSpec((tm, 768), row),
                  pl.BlockSpec((SCAN_LEN, tm // SCAN_LEN * SSM_DIM), lambda i: (0, i))]
    return pl.pallas_call(
        functools.partial(_inproj_kernel, rope=rope),
        grid=(t // tm,),
        in_specs=in_specs,
        out_specs=out_specs,
        out_shape=out_shape,
        scratch_shapes=[pltpu.VMEM((D, IN_DIM), BF16)],
        compiler_params=_cparams(("arbitrary",)),
        name="inproj_lat" if rope else "inproj_ctx",
    )(*args)


def _attention(sink_ref, l, q_ref, o_ref, keys, vals, bias, nq):
    top = lax.broadcasted_iota(jnp.int32, (2 * nq, 1), 0) < nq
    scores = []
    for kvh in range(KV_HEADS):
        q2 = jnp.concatenate([q_ref[:, LANES * (2 * kvh + pp):LANES * (2 * kvh + pp + 1)] for pp in range(2)],
                             axis=0)
        scores.append([_dot_t(q2, keys[kvh][half]) for half in range(2)])
    for kvh in range(KV_HEADS):
        acc = None
        for half in range(2):
            sk = jnp.where(top, sink_ref[l, 4 * kvh + half], sink_ref[l, 4 * kvh + 2 + half]) * LOG2E
            s = scores[kvh][half]
            if bias is not None:
                nb_ = bias.shape[1]
                s = jnp.concatenate([s[:, 0:nb_] + bias, s[:, nb_:]], axis=1)
            m = jnp.maximum(jnp.max(s, axis=-1, keepdims=True), sk)
            e = jnp.exp2(s - m)
            den = jnp.sum(e, axis=-1, keepdims=True) + jnp.exp2(sk - m)
            o = _dot(e.astype(BF16), vals[kvh][half]) / den
            acc = o if acc is None else acc + o
        o_ref[:, LANES * 2 * kvh:LANES * (2 * kvh + 1)] = acc[0:nq].astype(BF16)
        o_ref[:, LANES * (2 * kvh + 1):LANES * (2 * kvh + 2)] = acc[nq:2 * nq].astype(BF16)


def _kv_tiles(kvh):
    return [slice(LANES * (2 * kvh + h), LANES * (2 * kvh + h + 1)) for h in range(2)]


def _attn_ctx_kernel(sink_ref, q_ref, kd_ref, vd_ref, o_ref, *, l):
    keys = [[kd_ref[:, t] for t in _kv_tiles(kvh)] for kvh in range(KV_HEADS)]
    vals = [[vd_ref[:, t] for t in _kv_tiles(kvh)] for kvh in range(KV_HEADS)]
    _attention(sink_ref, l, q_ref, o_ref, keys, vals, None, q_ref.shape[0])


def _attn_ctx(sink, l, q, kd, vd, seq_len):
    t = q.shape[0]
    row = lambda b: (b, 0)
    return pl.pallas_call(
        functools.partial(_attn_ctx_kernel, l=l),
        grid=(t // seq_len,),
        in_specs=[
            pl.BlockSpec(memory_space=pltpu.SMEM),
            pl.BlockSpec((seq_len, 512), row),
            pl.BlockSpec((seq_len, 512), row),
            pl.BlockSpec((seq_len, 512), row),
        ],
        out_specs=pl.BlockSpec((seq_len, 512), row),
        out_shape=jax.ShapeDtypeStruct((t, 512), BF16),
        compiler_params=_cparams(("parallel",)),
        name="attn_ctx",
    )(sink, q, kd, vd)


def _attn_lat_kernel(sink_ref, q_ref, kp_ref, kc_ref, kn_ref, vp_ref, vc_ref, vn_ref, ck_ref, cv_ref, o_ref,
                     ckt_scr, cvt_scr, *, l):
    i = pl.program_id(1)
    nb = pl.num_programs(1)

    @pl.when(i == 0)
    def _context_tiles():
        for p, (kt, vt) in enumerate(zip(_half_tiles(ck_ref[...]), _half_tiles(cv_ref[...]))):
            ckt_scr[p] = kt
            cvt_scr[p] = vt

    r = lax.broadcasted_iota(jnp.int32, (Q_BLOCK, Q_BLOCK), 0)
    j = lax.broadcasted_iota(jnp.int32, (Q_BLOCK, Q_BLOCK), 1)
    bias = jnp.concatenate([
        jnp.where(jnp.logical_and(j >= r, i > 0), 0.0, NEG_BIG),
        jnp.where(jnp.logical_and(j <= r, i < nb - 1), 0.0, NEG_BIG)], axis=1)
    bias = jnp.concatenate([bias, bias], axis=0)
    def gather(refs, ctx_scr, kvh, h):
        t = _kv_tiles(kvh)[h]
        return jnp.concatenate([ref[:, t] for ref in refs] + [ctx_scr[2 * kvh + h]], axis=0)

    keys = [[gather((kp_ref, kn_ref, kc_ref), ckt_scr, kvh, h) for h in range(2)] for kvh in range(KV_HEADS)]
    vals = [[gather((vp_ref, vn_ref, vc_ref), cvt_scr, kvh, h) for h in range(2)] for kvh in range(KV_HEADS)]
    _attention(sink_ref, l, q_ref, o_ref, keys, vals, bias, Q_BLOCK)


def _attn_lat(sink, l, q, kd, vd, cache_k, cache_v, batch, seq_len):
    nb = seq_len // Q_BLOCK
    past = cache_k.shape[2]
    cur = lambda b, i: (b * nb + i, 0)
    prev = lambda b, i: (b * nb + jnp.maximum(i - 1, 0), 0)
    nxt = lambda b, i: (b * nb + jnp.minimum(i + 1, nb - 1), 0)
    kvspec = lambda f: pl.BlockSpec((Q_BLOCK, 512), f)
    cspec = pl.BlockSpec((None, None, past, KV_DIM), lambda b, i: (b, l, 0, 0))
    ck = cache_k.reshape(batch, DEPTH, past, KV_DIM)
    cv = cache_v.reshape(batch, DEPTH, past, KV_DIM)
    return pl.pallas_call(
        functools.partial(_attn_lat_kernel, l=l),
        grid=(batch, nb),
        in_specs=[
            pl.BlockSpec(memory_space=pltpu.SMEM),
            pl.BlockSpec((Q_BLOCK, 512), cur),
            kvspec(prev), kvspec(cur), kvspec(nxt),
            kvspec(prev), kvspec(cur), kvspec(nxt),
            cspec, cspec,
        ],
        out_specs=pl.BlockSpec((Q_BLOCK, 512), cur),
        out_shape=jax.ShapeDtypeStruct((batch * seq_len, 512), BF16),
        scratch_shapes=[pltpu.VMEM((2 * KV_HEADS, past, LANES), BF16)] * 2,
        compiler_params=_cparams(("parallel", "arbitrary")),
        name="attn_lat",
    )(sink, q, kd, kd, kd, vd, vd, vd, ck, cv)


def _scan_kernel(lre_ref, lim_ref, ldt_ref, bre_ref, bim_ref, cre_ref, cim_ref, *rest, emit_y, chain):
    rest = list(rest)
    if chain:
        s0_ref, ez_ref = rest[:2]
        rest = rest[2:]
    suf_ref, sub_ref = rest[:2]
    rest = rest[2:]
    if emit_y:
        y_refs = rest[:2]
        rest = rest[2:]
    fin_ref, a_scr, bb_scr, h_scr, bu_scr, u_scr = rest[:6]
    if emit_y:
        y_scr, ct_scr = rest[6:8]
    i = pl.program_id(0)
    tt = SCAN_TT
    n = SSM_N

    @pl.when(i == 0)
    def _prologue():
        row_g = lax.shift_right_logical(lax.broadcasted_iota(jnp.int32, (SSM_DIM, n), 0), 4)
        col_g = lax.shift_right_logical(lax.broadcasted_iota(jnp.int32, (SSM_DIM, n), 1), 6)
        own = row_g == col_g

        def blockdiag(ref, d):
            return jnp.where(own, jnp.concatenate([ref[d]] * SSM_GROUPS, axis=0), 0.0)

        for d in range(2):
            lr, li = lre_ref[d], lim_ref[d]
            dt = jnp.exp(ldt_ref[d])
            mag = jnp.exp(lr * dt)
            ar, ai = mag * jnp.cos(li * dt), mag * jnp.sin(li * dt)
            den = lr * lr + li * li
            fr = ((ar - 1.0) * lr + ai * li) / den
            fi = (ai * lr - (ar - 1.0) * li) / den
            bre, bim = blockdiag(bre_ref, d), blockdiag(bim_ref, d)
            bb_scr[d, :, 0:n] = (fr * bre - fi * bim).astype(BF16)
            bb_scr[d, :, n:2 * n] = (fr * bim + fi * bre).astype(BF16)
            if emit_y:
                ct_scr[d, :, 0:n] = blockdiag(cre_ref, d).astype(BF16)
                ct_scr[d, :, n:2 * n] = (-blockdiag(cim_ref, d)).astype(BF16)
            a_scr[d, 0] = jnp.broadcast_to(ar, (8, n))
            a_scr[d, 1] = jnp.broadcast_to(ai, (8, n))
            if chain:
                pr, pi_ = ar, ai
                for _ in range(8):
                    pr, pi_ = pr * pr - pi_ * pi_, 2.0 * pr * pi_
                chunks = SCAN_ROWS // 2
                for b in range(2):
                    hr = s0_ref[d, b:b + 1, 0:n]
                    hi = s0_ref[d, b:b + 1, n:2 * n]
                    order = range(chunks) if d == 0 else range(chunks - 1, -1, -1)
                    for c in order:
                        rw = b * chunks + c
                        h_scr[d, rw:rw + 1, 0:n] = hr
                        h_scr[d, rw:rw + 1, n:2 * n] = hi
                        er = ez_ref[d, rw:rw + 1, 0:n]
                        ei = ez_ref[d, rw:rw + 1, n:2 * n]
                        hr, hi = pr * hr - pi_ * hi + er, pr * hi + pi_ * hr + ei
            else:
                h_scr[d] = jnp.zeros((SCAN_ROWS, 2 * n), F32)

    ntile = n // LANES
    group = SCAN_LW // LANES

    def project_in(d):
        su_ref = suf_ref if d == 0 else sub_ref
        for r in range(SCAN_ROWS):
            for sl in range(SSM_DIM // LANES):
                col = r * SSM_DIM + sl * LANES
                u_scr[d, sl, pl.ds(r, tt, stride=SCAN_ROWS), :] = su_ref[:, col:col + LANES]
        for r0 in range(0, tt * SCAN_ROWS, SCAN_MM_ROWS):
            rows = slice(r0, r0 + SCAN_MM_ROWS)
            u = jnp.concatenate([u_scr[d, sl, rows, :] for sl in range(SSM_DIM // LANES)], axis=1).astype(BF16)
            bu_scr[d, rows, :] = _dot(u, bb_scr[d])

    def recur(d):
        for c0 in range(0, ntile, group):
            lre = slice(c0 * LANES, (c0 + group) * LANES)
            lim = slice(n + c0 * LANES, n + (c0 + group) * LANES)
            ar, ai = a_scr[d, 0, :, lre], a_scr[d, 1, :, lre]
            halves = [slice(0, 8), slice(8, 16)]
            hr = [h_scr[d, hs, lre] for hs in halves]
            hi = [h_scr[d, hs, lim] for hs in halves]
            steps = range(tt) if d == 0 else range(tt - 1, -1, -1)
            for s in steps:
                for k in range(2):
                    rows = slice(s * SCAN_ROWS + 8 * k, s * SCAN_ROWS + 8 * k + 8)
                    br, bi = bu_scr[d, rows, lre], bu_scr[d, rows, lim]
                    hr[k], hi[k] = ar * hr[k] - ai * hi[k] + br, ar * hi[k] + ai * hr[k] + bi
                    if emit_y:
                        bu_scr[d, rows, lre] = hr[k]
                        bu_scr[d, rows, lim] = hi[k]
            for k, hs in enumerate(halves):
                h_scr[d, hs, lre] = hr[k]
                h_scr[d, hs, lim] = hi[k]

    def project_out(d):
        for r0 in range(0, tt * SCAN_ROWS, SCAN_MM_ROWS):
            rows = slice(r0, r0 + SCAN_MM_ROWS)
            y = _dot_t(bu_scr[d, rows, :].astype(BF16), ct_scr[d])
            for sl in range(SSM_DIM // LANES):
                y_scr[d, sl, rows, :] = y[:, sl * LANES:(sl + 1) * LANES]
        for r in range(SCAN_ROWS):
            for sl in range(SSM_DIM // LANES):
                col = r * SSM_DIM + sl * LANES
                y_refs[d][:, col:col + LANES] = y_scr[d, sl, pl.ds(r, tt, stride=SCAN_ROWS), :].astype(BF16)

    project_in(0)
    project_in(1)
    recur(0)
    if emit_y:
        project_out(0)
    recur(1)
    if emit_y:
        project_out(1)

    @pl.when(i == pl.num_programs(0) - 1)
    def _final():
        fin_ref[...] = h_scr[...]


def _scan(prm, l, su_tm, emit_y, s0=None, ez=None):
    chain = s0 is not None
    nt = SCAN_LEN // SCAN_TT
    n = SSM_N
    full3 = lambda shape: pl.BlockSpec(shape, lambda i: (0, 0, 0))
    in_specs = [_layer_spec((2, 1, n), l)] * 3 + [_layer_spec((2, SSM_CH, n), l)] * 4
    args = [prm["lam_re"], prm["lam_im"], prm["log_dt"], prm["b_re"], prm["b_im"], prm["c_re"], prm["c_im"]]
    if chain:
        in_specs += [full3((2, 2, 2 * n)), full3((2, SCAN_ROWS, 2 * n))]
        args += [s0, ez]
    tblk = (SCAN_TT, SCAN_ROWS * SSM_DIM)
    fwd = lambda i: (i, 0)
    bwd = lambda i: (nt - 1 - i, 0)
    in_specs += [pl.BlockSpec(tblk, fwd), pl.BlockSpec(tblk, bwd)]
    args += [su_tm, su_tm]
    out_shape, out_specs = [], []
    scratch = [
        pltpu.VMEM((2, 2, 8, n), F32),
        pltpu.VMEM((2, SSM_DIM, 2 * n), BF16),
        pltpu.VMEM((2, SCAN_ROWS, 2 * n), F32),
        pltpu.VMEM((2, SCAN_ROWS * SCAN_TT, 2 * n), F32),
        pltpu.VMEM((2, SSM_DIM // LANES, SCAN_ROWS * SCAN_TT, LANES), F32),
    ]
    if emit_y:
        yshape = jax.ShapeDtypeStruct((SCAN_LEN, SCAN_ROWS * SSM_DIM), BF16)
        out_shape += [yshape, yshape]
        out_specs += [pl.BlockSpec(tblk, fwd), pl.BlockSpec(tblk, bwd)]
        scratch.append(pltpu.VMEM((2, SSM_DIM // LANES, SCAN_ROWS * SCAN_TT, LANES), F32))
        scratch.append(pltpu.VMEM((2, SSM_DIM, 2 * n), BF16))
    out_shape.append(jax.ShapeDtypeStruct((2, SCAN_ROWS, 2 * n), F32))
    out_specs.append(full3((2, SCAN_ROWS, 2 * n)))
    return pl.pallas_call(
        functools.partial(_scan_kernel, emit_y=emit_y, chain=chain),
        grid=(nt,),
        in_specs=in_specs,
        out_specs=out_specs,
        out_shape=out_shape,
        scratch_shapes=scratch,
        compiler_params=_cparams(("arbitrary",)),
        name="scan_chain" if chain else ("scan_y" if emit_y else "scan_state"),
    )(*args)


def _shifted(scr_ref, off, rows, pos, seq_len, w):
    mid = scr_ref[off:off + rows, :]
    up = jnp.where(pos != 0, scr_ref[off - 1:off - 1 + rows, :], 0.0)
    dn = jnp.where(pos != seq_len - 1, scr_ref[off + 1:off + 1 + rows, :], 0.0)
    return w[0:1, :] * up + w[1:2, :] * mid + w[2:3, :] * dn


def _mix_kernel(x_ref, mod_ref, attn_ref, g3_ref, *rest, halo, tm, seq_len):
    rest = list(rest)
    if halo:
        gp_ref, gn_ref = rest[:2]
        rest = rest[2:]
    (su_ref, yf_ref, yb_ref, wc_ref, dsk_ref, wglu_ref, wo_ref, gn2_ref,
     x1_ref, h2_ref, z_scr, wglu_scr, wo_scr) = rest
    i = pl.program_id(0)

    @pl.when(i == 0)
    def _cast_weights():
        wglu_scr[...] = wglu_ref[...].astype(BF16)
        wo_scr[...] = wo_ref[...].astype(BF16)

    def rows_of(ref):
        pieces = [ref[:, r * SSM_DIM:(r + 1) * SSM_DIM].astype(F32) for r in range(tm // SCAN_LEN)]
        return pieces[0] if len(pieces) == 1 else jnp.concatenate(pieces, axis=0)

    m = mod_ref[...]
    g1, sh2, sc2 = m[:, 2 * D:3 * D], m[:, 3 * D:4 * D], m[:, 4 * D:5 * D]
    pos = (i * tm + lax.broadcasted_iota(jnp.int32, (tm, 1), 0)) % seq_len

    def gate_prod(ref):
        return ref[:, SC_DIM:2 * SC_DIM].astype(F32) * ref[:, 2 * SC_DIM:3 * SC_DIM].astype(F32)

    gb = g3_ref[:, 0:SC_DIM].astype(F32)
    z_scr[16:16 + tm, :] = gate_prod(g3_ref)
    if halo:
        z_scr[0:16, :] = gate_prod(gp_ref)
        z_scr[16 + tm:32 + tm, :] = gate_prod(gn_ref)
    else:
        z_scr[0:16, :] = jnp.zeros((16, SC_DIM), F32)
        z_scr[16 + tm:32 + tm, :] = jnp.zeros((16, SC_DIM), F32)
    conv = gb * _shifted(z_scr, 16, tm, pos, seq_len, wc_ref[...])

    y = dsk_ref[...] * rows_of(su_ref) + rows_of(yf_ref) + rows_of(yb_ref)
    zz = 0.5 * y * (1.0 + jnp.tanh(math.sqrt(2.0 / math.pi) * (y + 0.044715 * (y * y * y))))
    ssm = zz * _sigmoid(_dot(zz.astype(BF16), wglu_scr[...]))

    mix = (_dot(attn_ref[...], wo_scr[0:ATTN_DIM, :])
           + _dot(conv.astype(BF16), wo_scr[ATTN_DIM:ATTN_DIM + SC_DIM, :])
           + _dot(ssm.astype(BF16), wo_scr[ATTN_DIM + SC_DIM:, :]))
    x1 = x_ref[...] + g1 * mix
    x1_ref[...] = x1
    h2_ref[...] = _rms_mod(x1, gn2_ref[...], sc2, sh2).astype(BF16)


def _mix(x, mods, attn, g3, su, yf, yb, prm, l, lat, tokens_per_mod, seq_len, tm=512):
    t = x.shape[0]
    halo = seq_len > tm
    tiles_per_mod = tokens_per_mod // tm
    mod_map = (lambda i: (l, 1 + i // tiles_per_mod, 0, 0)) if lat else (lambda i: (l, 0, 0, 0))
    row = lambda i: (i, 0)
    in_specs = [pl.BlockSpec((tm, D), row), pl.BlockSpec((None, None, 1, 6 * D), mod_map),
                pl.BlockSpec((tm, 512), row), pl.BlockSpec((tm, 768), row)]
    args = [x, mods, attn, g3]
    if halo:
        r16 = tm // 16
        in_specs += [pl.BlockSpec((16, 768), lambda i: (jnp.maximum(i * r16 - 1, 0), 0)),
                     pl.BlockSpec((16, 768), lambda i: (jnp.minimum((i + 1) * r16, t // 16 - 1), 0))]
        args += [g3, g3]
    assert tm % SCAN_LEN == 0
    in_specs += [pl.BlockSpec((SCAN_LEN, tm // SCAN_LEN * SSM_DIM), lambda i: (0, i))] * 3
    args += [su, yf, yb]
    in_specs += [_layer_spec((3, SC_DIM), l), _layer_spec((1, SSM_DIM), l), _layer_spec((SSM_DIM, SSM_DIM), l),
                 _layer_spec((D, D), l), _layer_spec((1, D), l)]
    args += [prm["sc_conv"], prm["ssm_d"], prm["w_glu"], prm["w_out"], prm["norm_ffn"]]
    return pl.pallas_call(
        functools.partial(_mix_kernel, halo=halo, tm=tm, seq_len=seq_len),
        grid=(t // tm,),
        in_specs=in_specs,
        out_specs=[pl.BlockSpec((tm, D), row), pl.BlockSpec((tm, D), row)],
        out_shape=[jax.ShapeDtypeStruct((t, D), F32), jax.ShapeDtypeStruct((t, D), BF16)],
        scratch_shapes=[pltpu.VMEM((tm + 32, SC_DIM), F32), pltpu.VMEM((SSM_DIM, SSM_DIM), BF16),
                        pltpu.VMEM((D, D), BF16)],
        compiler_params=_cparams(("arbitrary",)),
        name="mix_lat" if lat else "mix_ctx",
    )(*args)


def _ffn_kernel(h_ref, *rest, halo, tm, seq_len, final):
    rest = list(rest)
    if halo:
        hp_ref, hn_ref = rest[:2]
        rest = rest[2:]
    x1_ref, mod_ref, wa_ref, wg_ref, ca_ref, cg_ref, wd_ref = rest[:7]
    rest = rest[7:]
    if final:
        gf_ref = rest[0]
        rest = rest[1:]
    o_ref, acc_scr = rest[:2]
    rest = rest[2:]
    nsub = len(FFN_SUB)
    ua_scrs, ug_scrs = rest[:nsub], rest[nsub:2 * nsub]
    i = pl.program_id(0)
    j = pl.program_id(1)

    if halo:
        hcat_scr = rest[2 * nsub]

        @pl.when(j == 0)
        def _stage():
            hcat_scr[0:16, :] = hp_ref[...]
            hcat_scr[16:16 + tm, :] = h_ref[...]
            hcat_scr[16 + tm:32 + tm, :] = hn_ref[...]

        hh = hcat_scr[...]
        pad, seg_len, nseg = 16, tm, 1
        tiles_per_seq = seq_len // tm
        keep_prev = (i % tiles_per_seq != 0).astype(F32)
        keep_next = (i % tiles_per_seq != tiles_per_seq - 1).astype(F32)
    else:
        hh = h_ref[...]
        pad, seg_len, nseg = 8, seq_len, tm // seq_len
    seg_rows = seg_len + 2 * pad
    cols = [sum(FFN_SUB[:c]) for c in range(nsub)]

    def up(c):
        width = FFN_SUB[c]
        for u_scr, w_ref in ((ua_scrs[c], wa_ref), (ug_scrs[c], wg_ref)):
            u = _dot(hh, w_ref[:, cols[c]:cols[c] + width])
            if halo:
                u_scr[...] = u
                u_scr[8:16, :] = u_scr[8:16, :] * keep_prev
                u_scr[16 + tm:24 + tm, :] = u_scr[16 + tm:24 + tm, :] * keep_next
            else:
                for sg in range(nseg):
                    r0 = sg * seg_rows
                    u_scr[r0:r0 + pad, :] = jnp.zeros((pad, width), F32)
                    u_scr[r0 + pad:r0 + pad + seg_len, :] = u[sg * seg_len:(sg + 1) * seg_len]
                    u_scr[r0 + pad + seg_len:r0 + seg_rows, :] = jnp.zeros((pad, width), F32)

    def conv(c, u_scr, cw_ref):
        w = cw_ref[:, cols[c]:cols[c] + FFN_SUB[c]]
        segs = []
        for sg in range(nseg):
            r0 = sg * seg_rows + pad
            segs.append(w[0:1, :] * u_scr[r0 - 1:r0 - 1 + seg_len, :]
                        + w[1:2, :] * u_scr[r0:r0 + seg_len, :]
                        + w[2:3, :] * u_scr[r0 + 1:r0 + 1 + seg_len, :])
        return segs[0] if nseg == 1 else jnp.concatenate(segs, axis=0)

    def down(c):
        a = conv(c, ua_scrs[c], ca_ref)
        g = conv(c, ug_scrs[c], cg_ref)
        act = (a * (g * _sigmoid(g))).astype(BF16)
        acc_scr[...] += _dot(act, wd_ref[cols[c]:cols[c] + FFN_SUB[c], :])

    @pl.when(j == 0)
    def _init():
        acc_scr[...] = jnp.zeros((tm, D), F32)

    up(0)
    for c in range(nsub):
        if c + 1 < nsub:
            up(c + 1)
        down(c)

    @pl.when(j == pl.num_programs(1) - 1)
    def _finish():
        g2 = mod_ref[...][:, 5 * D:6 * D]
        x2 = x1_ref[...] + g2 * acc_scr[...]
        if final:
            x2 = x2 * lax.rsqrt(jnp.mean(x2 * x2, axis=-1, keepdims=True) + RMS_EPS) * gf_ref[...]
        o_ref[...] = x2


def _ffn(h2, x1, mods, prm, l, lat, tokens_per_mod, seq_len, final_g, tm=512):
    t = h2.shape[0]
    tf = FFN_TF
    halo = seq_len > tm
    final = final_g is not None
    nj = D_FF // tf
    tiles_per_mod = tokens_per_mod // tm
    mod_map = (lambda i, j: (l, 1 + i // tiles_per_mod, 0, 0)) if lat else (lambda i, j: (l, 0, 0, 0))
    row = lambda i, j: (i, 0)
    in_specs = [pl.BlockSpec((tm, D), row)]
    args = [h2]
    if halo:
        r16 = tm // 16
        in_specs += [pl.BlockSpec((16, D), lambda i, j: (jnp.maximum(i * r16 - 1, 0), 0)),
                     pl.BlockSpec((16, D), lambda i, j: (jnp.minimum((i + 1) * r16, t // 16 - 1), 0))]
        args += [h2, h2]
    in_specs += [pl.BlockSpec((tm, D), row), pl.BlockSpec((None, None, 1, 6 * D), mod_map),
                 pl.BlockSpec((None, D, tf), lambda i, j: (l, 0, j)),
                 pl.BlockSpec((None, D, tf), lambda i, j: (l, 0, j + nj)),
                 pl.BlockSpec((None, 3, tf), lambda i, j: (l, 0, j)),
                 pl.BlockSpec((None, 3, tf), lambda i, j: (l, 0, j + nj)),
                 pl.BlockSpec((None, tf, D), lambda i, j: (l, j, 0))]
    args += [x1, mods, prm["w_up"], prm["w_up"], prm["ffn_conv"], prm["ffn_conv"], prm["w_down"]]
    if final:
        in_specs.append(pl.BlockSpec((1, D), lambda i, j: (0, 0)))
        args.append(final_g)
    urows = tm + 32 if halo else (tm // seq_len) * (seq_len + 16)
    scratch = [pltpu.VMEM((tm, D), F32)]
    scratch += [pltpu.VMEM((urows, w), F32) for w in FFN_SUB] * 2
    if halo:
        scratch.append(pltpu.VMEM((tm + 32, D), BF16))
    return pl.pallas_call(
        functools.partial(_ffn_kernel, halo=halo, tm=tm, seq_len=seq_len, final=final),
        grid=(t // tm, nj),
        in_specs=in_specs,
        out_specs=pl.BlockSpec((tm, D), row),
        out_shape=jax.ShapeDtypeStruct((t, D), F32),
        scratch_shapes=scratch,
        compiler_params=_cparams(("parallel", "arbitrary")),
        name="ffn_lat" if lat else "ffn_ctx",
    )(*args)


def _rope_tables(seq_len):
    rows = seq_len // GRID_W
    row = np.repeat(np.arange(rows, dtype=np.float32), GRID_W)
    col = np.tile(np.arange(GRID_W, dtype=np.float32), rows)
    freqs = (np.float32(ROPE_BASE) ** (-np.arange(ROPE_FREQS, dtype=np.float32) / np.float32(ROPE_FREQS)))
    freqs = freqs.astype(np.float32)
    ang = np.stack([row[:, None] * freqs, col[:, None] * freqs], axis=1).astype(np.float32)
    cos, sin = np.cos(ang).astype(np.float32), np.sin(ang).astype(np.float32)
    zero = np.zeros_like(sin)
    c64 = np.stack([cos, cos], axis=2).reshape(seq_len, HEAD_DIM)
    sneg64 = np.stack([-sin, zero], axis=2).reshape(seq_len, HEAD_DIM)
    spos64 = np.stack([zero, sin], axis=2).reshape(seq_len, HEAD_DIM)
    return tuple(jnp.asarray(np.tile(tb, (1, 2))) for tb in (c64, sneg64, spos64))


def _prep_params(w_in, w_out, norm_mix, norm_ffn, sc_conv, ssm_lam_re, ssm_lam_im, ssm_log_dt, ssm_b_re,
                 ssm_b_im, ssm_c_re, ssm_c_im, ssm_d, ssm_w_glu, ffn_w_up, ffn_conv, ffn_w_down):
    b_slab = lambda b: jnp.transpose(b, (0, 1, 4, 2, 3)).reshape(DEPTH, 2, SSM_CH, SSM_N)
    c_slab = lambda c: jnp.transpose(c, (0, 1, 3, 2, 4)).reshape(DEPTH, 2, SSM_CH, SSM_N)
    return {
        "w_in": w_in,
        "norm_mix": norm_mix.reshape(DEPTH, 1, D),
        "norm_ffn": norm_ffn.reshape(DEPTH, 1, D),
        "sc_conv": jnp.transpose(sc_conv, (0, 2, 1)),
        "ssm_d": ssm_d.reshape(DEPTH, 1, SSM_DIM),
        "w_glu": ssm_w_glu,
        "w_out": w_out,
        "w_up": ffn_w_up.astype(BF16),
        "ffn_conv": jnp.transpose(ffn_conv, (0, 2, 1)),
        "w_down": ffn_w_down.astype(BF16),
        "lam_re": ssm_lam_re.reshape(DEPTH, 2, 1, SSM_N),
        "lam_im": ssm_lam_im.reshape(DEPTH, 2, 1, SSM_N),
        "log_dt": jnp.repeat(ssm_log_dt, SSM_STATE, axis=-1).reshape(DEPTH, 2, 1, SSM_N),
        "b_re": b_slab(ssm_b_re),
        "b_im": b_slab(ssm_b_im),
        "c_re": c_slab(ssm_c_re),
        "c_im": c_slab(ssm_c_im),
    }


def kernel(x_prompt, x_sample, cache_k, cache_v, state_ssm_re, state_ssm_im, c, c_ctx, norm_mix, norm_ffn, norm_final, w_ada, b_ada, w_in, w_out, attn_sink, sc_conv, ssm_lam_re, ssm_lam_im, ssm_log_dt, ssm_b_re, ssm_b_im, ssm_c_re, ssm_c_im, ssm_d, ssm_w_glu, ffn_w_up, ffn_conv, ffn_w_down):
    batch, seq = x_prompt.shape[0], x_prompt.shape[1]
    dec_batch, dec_seq = x_sample.shape[0], x_sample.shape[1]
    assert batch == SCAN_ROWS and seq == SCAN_LEN
    assert dec_batch * (dec_seq // SCAN_LEN) == SCAN_ROWS and dec_batch == 2

    cs = jnp.concatenate([c_ctx[None, :], c, jnp.zeros((8 - 1 - dec_batch, D), F32)], axis=0)
    mods = _adaln(cs, w_ada, b_ada).reshape(DEPTH, 8, 1, 6 * D)
    rope_tabs = _rope_tables(dec_seq)
    gfin = norm_final.reshape(1, D)
    prm = _prep_params(w_in, w_out, norm_mix, norm_ffn, sc_conv, ssm_lam_re, ssm_lam_im, ssm_log_dt, ssm_b_re,
                       ssm_b_im, ssm_c_re, ssm_c_im, ssm_d, ssm_w_glu, ffn_w_up, ffn_conv, ffn_w_down)
    s0_all = jnp.transpose(jnp.concatenate([state_ssm_re.reshape(dec_batch, DEPTH, 2, SSM_N),
                                            state_ssm_im.reshape(dec_batch, DEPTH, 2, SSM_N)], axis=-1),
                           (1, 2, 0, 3))

    xp = x_prompt.reshape(batch * seq, D)
    xs = x_sample.reshape(dec_batch * dec_seq, D)
    kv_out, fin_out = [], []
    for l in range(DEPTH):
        last = gfin if l == DEPTH - 1 else None

        q, kd, vd, kv, g3, su = _inproj(xp, mods, prm, l, None, batch * seq, seq)
        attn = _attn_ctx(attn_sink, l, q, kd, vd, seq)
        yf, yb, fin = _scan(prm, l, su, True)
        x1, h2 = _mix(xp, mods, attn, g3, su, yf, yb, prm, l, False, batch * seq, seq)
        xp = _ffn(h2, x1, mods, prm, l, False, batch * seq, seq, last)
        kv_out.append(kv)
        fin_out.append(fin)

        q, kd, vd, g3, su = _inproj(xs, mods, prm, l, rope_tabs, dec_seq, dec_seq)
        attn = _attn_lat(attn_sink, l, q, kd, vd, cache_k, cache_v, dec_batch, dec_seq)
        (ez,) = _scan(prm, l, su, False)
        yf, yb, _ = _scan(prm, l, su, True, s0=s0_all[l], ez=ez)
        x1, h2 = _mix(xs, mods, attn, g3, su, yf, yb, prm, l, True, dec_seq, dec_seq)
        xs = _ffn(h2, x1, mods, prm, l, True, dec_seq, dec_seq, last)

    kv_all = jnp.stack(kv_out, axis=0).reshape(DEPTH, batch, seq, 2, KV_HEADS, HEAD_DIM)
    kv_all = jnp.transpose(kv_all, (3, 1, 0, 2, 4, 5))
    fin_all = jnp.stack(fin_out, axis=0).reshape(DEPTH, 2, batch, 2, SSM_GROUPS, SSM_STATE)
    fin_all = jnp.transpose(fin_all, (3, 2, 0, 1, 4, 5))
    return (xp.reshape(batch, seq, D), xs.reshape(dec_batch, dec_seq, D),
            kv_all[0], kv_all[1], fin_all[0], fin_all[1])
```

```python
import functools
import math

import jax
import jax.numpy as jnp
import numpy as np
from jax import lax
from jax.experimental import pallas as pl
from jax.experimental.pallas import tpu as pltpu

F32 = jnp.float32
BF16 = jnp.bfloat16

D = 1024
DEPTH = 2
GRID_W = 64
ATTN_DIM = 512
SC_DIM = 256
SSM_DIM = 256
HEAD_DIM = 64
N_HEADS = 8
KV_HEADS = 2
KV_DIM = 128
WINDOW = 128
Q_BLOCK = 128
ROPE_BASE = 10000.0
ROPE_FREQS = 16
SSM_CH = 16
SSM_GROUPS = 16
SSM_STATE = 64
SSM_N = SSM_GROUPS * SSM_STATE
IN_DIM = ATTN_DIM + 2 * KV_DIM + 3 * SC_DIM + SSM_DIM
D_FF = 2816
LOG2E = math.log2(math.e)
Q_SCALE = LOG2E / math.sqrt(HEAD_DIM)
RMS_EPS = 1e-6
NEG_BIG = -1e30

LANES = 128
SCAN_ROWS = 16
SCAN_LEN = 256
SCAN_TT = 64
SCAN_LW = 256
SCAN_MM_ROWS = 1024
FFN_TF = 1408
FFN_SUB = (768, 512, 128)
assert D_FF % FFN_TF == 0 and sum(FFN_SUB) == FFN_TF
VMEM_LIMIT = 56 * 1024 * 1024


def _cparams(sem):
    return pltpu.CompilerParams(dimension_semantics=sem, vmem_limit_bytes=VMEM_LIMIT)


def _sigmoid(x):
    return 1.0 / (1.0 + jnp.exp(-x))


def _rms_mod(x, g, scale, shift):
    y = x * lax.rsqrt(jnp.mean(x * x, axis=-1, keepdims=True) + RMS_EPS) * g
    return y * (1.0 + scale) + shift


def _split_bf16(v):
    hi = v.astype(BF16)
    lo = (v - hi.astype(F32)).astype(BF16)
    return hi, lo


def _dot(a, b):
    return jnp.dot(a, b, preferred_element_type=F32)


def _dot_t(a, b):
    return lax.dot_general(a, b, (((1,), (1,)), ((), ())), preferred_element_type=F32)


def _layer_spec(shape, l):
    zeros = (0,) * len(shape)
    return pl.BlockSpec((None,) + tuple(shape), lambda *_: (l,) + zeros)


def _half_tiles(t):
    lo = lax.broadcasted_iota(jnp.int32, t.shape, 1) < HEAD_DIM
    tr = pltpu.roll(t, HEAD_DIM, 1)
    return [jnp.where(lo, t, 0.0).astype(BF16), jnp.where(lo, 0.0, tr).astype(BF16),
            jnp.where(lo, tr, 0.0).astype(BF16), jnp.where(lo, 0.0, t).astype(BF16)]


def _adaln_kernel(c_ref, w_ref, b_ref, o_ref):
    c = c_ref[...]
    s = c * _sigmoid(c)
    s_hi, s_lo = _split_bf16(s)
    w_hi, w_lo = _split_bf16(w_ref[...])
    o_ref[...] = _dot(s_hi, w_hi) + _dot(s_lo, w_hi) + _dot(s_hi, w_lo) + b_ref[...]


def _adaln(cs, w_ada, b_ada):
    tn = 2048
    return pl.pallas_call(
        _adaln_kernel,
        grid=(DEPTH, 6 * D // tn),
        in_specs=[
            pl.BlockSpec((8, D), lambda l, j: (0, 0)),
            pl.BlockSpec((None, D, tn), lambda l, j: (l, 0, j)),
            pl.BlockSpec((None, 1, tn), lambda l, j: (l, 0, j)),
        ],
        out_specs=pl.BlockSpec((None, 8, tn), lambda l, j: (l, 0, j)),
        out_shape=jax.ShapeDtypeStruct((DEPTH, 8, 6 * D), F32),
        compiler_params=_cparams(("parallel", "parallel")),
        name="adaln",
    )(cs, w_ada, b_ada.reshape(DEPTH, 1, 6 * D))


def _inproj_kernel(x_ref, mod_ref, g_ref, w_ref, *rest, rope):
    if rope:
        cos_ref, sneg_ref, spos_ref, q_ref, kd_ref, vd_ref, g3_ref, su_ref, wbf_scr = rest
    else:
        q_ref, kd_ref, vd_ref, kv_ref, g3_ref, su_ref, wbf_scr = rest

    @pl.when(pl.program_id(0) == 0)
    def _cast_weights():
        wbf_scr[...] = w_ref[...].astype(BF16)

    m = mod_ref[...]
    h = _rms_mod(x_ref[...], g_ref[...], m[:, D:2 * D], m[:, 0:D]).astype(BF16)
    acc = _dot(h, wbf_scr[...])
    k = acc[:, 512:640]
    v = acc[:, 640:768]
    if rope:
        c, sn, sp = cos_ref[...], sneg_ref[...], spos_ref[...]

        def rot(t):
            return t * c + pltpu.roll(t, LANES - ROPE_FREQS, 1) * sn + pltpu.roll(t, ROPE_FREQS, 1) * sp

        for p in range(4):
            q_ref[:, LANES * p:LANES * (p + 1)] = (rot(acc[:, LANES * p:LANES * (p + 1)]) * Q_SCALE).astype(BF16)
        k = rot(k)
    else:
        q_ref[...] = (acc[:, 0:512] * Q_SCALE).astype(BF16)
        kv_ref[...] = acc[:, 512:768]
    for ref, t in ((kd_ref, k), (vd_ref, v)):
        for p, tile in enumerate(_half_tiles(t)):
            ref[:, LANES * p:LANES * (p + 1)] = tile
    g3_ref[...] = acc[:, 768:1536].astype(BF16)
    for r in range(acc.shape[0] // SCAN_LEN):
        su_ref[:, r * SSM_DIM:(r + 1) * SSM_DIM] = acc[r * SCAN_LEN:(r + 1) * SCAN_LEN, 1536:1792]


def _inproj(x, mods, prm, l, rope_tabs, tokens_per_mod, seq_len, tm=512):
    t = x.shape[0]
    rope = rope_tabs is not None
    tiles_per_mod = tokens_per_mod // tm
    tiles_per_seq = seq_len // tm
    if rope:
        mod_map = lambda i: (l, 1 + i // tiles_per_mod, 0, 0)
    else:
        mod_map = lambda i: (l, 0, 0, 0)
    row = lambda i: (i, 0)
    in_specs = [
        pl.BlockSpec((tm, D), row),
        pl.BlockSpec((None, None, 1, 6 * D), mod_map),
        _layer_spec((1, D), l),
        _layer_spec((D, IN_DIM), l),
    ]
    args = [x, mods, prm["norm_mix"], prm["w_in"]]
    if rope:
        in_specs += [pl.BlockSpec((tm, LANES), lambda i: (i % tiles_per_seq, 0))] * 3
        args += list(rope_tabs)
    out_shape = [jax.ShapeDtypeStruct((t, 512), BF16)] * 3
    out_specs = [pl.BlockSpec((tm, 512), row)] * 3
    if not rope:
        out_shape.append(jax.ShapeDtypeStruct((t, 256), F32))
        out_specs.append(pl.BlockSpec((tm, 256), row))
    out_shape += [jax.ShapeDtypeStruct((t, 768), BF16),
                  jax.ShapeDtypeStruct((SCAN_LEN, t // SCAN_LEN * SSM_DIM), F32)]
    out_specs += [pl.BlockSpec((tm, 768), row),
                  pl.BlockSpec((SCAN_LEN, tm // SCAN_LEN * SSM_DIM), lambda i: (0, i))]
    return pl.pallas_call(
        functools.partial(_inproj_kernel, rope=rope),
        grid=(t // tm,),
        in_specs=in_specs,
        out_specs=out_specs,
        out_shape=out_shape,
        scratch_shapes=[pltpu.VMEM((D, IN_DIM), BF16)],
        compiler_params=_cparams(("arbitrary",)),
        name="inproj_lat" if rope else "inproj_ctx",
    )(*args)


def _attention(sink_ref, l, q_ref, o_ref, keys, vals, bias, nq):
    top = lax.broadcasted_iota(jnp.int32, (2 * nq, 1), 0) < nq
    scores = []
    for kvh in range(KV_HEADS):
        q2 = jnp.concatenate([q_ref[:, LANES * (2 * kvh + pp):LANES * (2 * kvh + pp + 1)] for pp in range(2)],
                             axis=0)
        scores.append([_dot_t(q2, keys[kvh][half]) for half in range(2)])
    for kvh in range(KV_HEADS):
        acc = None
        for half in range(2):
            sk = jnp.where(top, sink_ref[l, 4 * kvh + half], sink_ref[l, 4 * kvh + 2 + half]) * LOG2E
            s = scores[kvh][half]
            if bias is not None:
                nb_ = bias.shape[1]
                s = jnp.concatenate([s[:, 0:nb_] + bias, s[:, nb_:]], axis=1)
            m = jnp.maximum(jnp.max(s, axis=-1, keepdims=True), sk)
            e = jnp.exp2(s - m)
            den = jnp.sum(e, axis=-1, keepdims=True) + jnp.exp2(sk - m)
            o = _dot(e.astype(BF16), vals[kvh][half]) / den
            acc = o if acc is None else acc + o
        o_ref[:, LANES * 2 * kvh:LANES * (2 * kvh + 1)] = acc[0:nq].astype(BF16)
        o_ref[:, LANES * (2 * kvh + 1):LANES * (2 * kvh + 2)] = acc[nq:2 * nq].astype(BF16)


def _kv_tiles(kvh):
    return [slice(LANES * (2 * kvh + h), LANES * (2 * kvh + h + 1)) for h in range(2)]


def _attn_ctx_kernel(sink_ref, q_ref, kd_ref, vd_ref, o_ref, *, l, seq_len):
    for s in range(q_ref.shape[0] // seq_len):
        rows = pl.ds(s * seq_len, seq_len)
        keys = [[kd_ref[rows, t] for t in _kv_tiles(kvh)] for kvh in range(KV_HEADS)]
        vals = [[vd_ref[rows, t] for t in _kv_tiles(kvh)] for kvh in range(KV_HEADS)]
        _attention(sink_ref, l, q_ref.at[rows, :], o_ref.at[rows, :], keys, vals, None, seq_len)


def _attn_ctx(sink, l, q, kd, vd, seq_len, seqs_per_step=2):
    t = q.shape[0]
    row = lambda b: (b, 0)
    rows = seqs_per_step * seq_len
    return pl.pallas_call(
        functools.partial(_attn_ctx_kernel, l=l, seq_len=seq_len),
        grid=(t // rows,),
        in_specs=[
            pl.BlockSpec(memory_space=pltpu.SMEM),
            pl.BlockSpec((rows, 512), row),
            pl.BlockSpec((rows, 512), row),
            pl.BlockSpec((rows, 512), row),
        ],
        out_specs=pl.BlockSpec((rows, 512), row),
        out_shape=jax.ShapeDtypeStruct((t, 512), BF16),
        compiler_params=_cparams(("parallel",)),
        name="attn_ctx",
    )(sink, q, kd, vd)


def _attn_lat_kernel(sink_ref, q_ref, kp_ref, kc_ref, kn_ref, vp_ref, vc_ref, vn_ref, ck_ref, cv_ref, o_ref,
                     ckt_scr, cvt_scr, *, l):
    i = pl.program_id(1)
    nb = pl.num_programs(1)

    @pl.when(i == 0)
    def _context_tiles():
        for p, (kt, vt) in enumerate(zip(_half_tiles(ck_ref[...]), _half_tiles(cv_ref[...]))):
            ckt_scr[p] = kt
            cvt_scr[p] = vt

    r = lax.broadcasted_iota(jnp.int32, (Q_BLOCK, Q_BLOCK), 0)
    j = lax.broadcasted_iota(jnp.int32, (Q_BLOCK, Q_BLOCK), 1)
    bias = jnp.concatenate([
        jnp.where(jnp.logical_and(j >= r, i > 0), 0.0, NEG_BIG),
        jnp.where(jnp.logical_and(j <= r, i < nb - 1), 0.0, NEG_BIG)], axis=1)
    bias = jnp.concatenate([bias, bias], axis=0)
    def gather(refs, ctx_scr, kvh, h):
        t = _kv_tiles(kvh)[h]
        return jnp.concatenate([ref[:, t] for ref in refs] + [ctx_scr[2 * kvh + h]], axis=0)

    keys = [[gather((kp_ref, kn_ref, kc_ref), ckt_scr, kvh, h) for h in range(2)] for kvh in range(KV_HEADS)]
    vals = [[gather((vp_ref, vn_ref, vc_ref), cvt_scr, kvh, h) for h in range(2)] for kvh in range(KV_HEADS)]
    _attention(sink_ref, l, q_ref, o_ref, keys, vals, bias, Q_BLOCK)


def _attn_lat(sink, l, q, kd, vd, cache_k, cache_v, batch, seq_len):
    nb = seq_len // Q_BLOCK
    past = cache_k.shape[2]
    cur = lambda b, i: (b * nb + i, 0)
    prev = lambda b, i: (b * nb + jnp.maximum(i - 1, 0), 0)
    nxt = lambda b, i: (b * nb + jnp.minimum(i + 1, nb - 1), 0)
    kvspec = lambda f: pl.BlockSpec((Q_BLOCK, 512), f)
    cspec = pl.BlockSpec((None, None, past, KV_DIM), lambda b, i: (b, l, 0, 0))
    ck = cache_k.reshape(batch, DEPTH, past, KV_DIM)
    cv = cache_v.reshape(batch, DEPTH, past, KV_DIM)
    return pl.pallas_call(
        functools.partial(_attn_lat_kernel, l=l),
        grid=(batch, nb),
        in_specs=[
            pl.BlockSpec(memory_space=pltpu.SMEM),
            pl.BlockSpec((Q_BLOCK, 512), cur),
            kvspec(prev), kvspec(cur), kvspec(nxt),
            kvspec(prev), kvspec(cur), kvspec(nxt),
            cspec, cspec,
        ],
        out_specs=pl.BlockSpec((Q_BLOCK, 512), cur),
        out_shape=jax.ShapeDtypeStruct((batch * seq_len, 512), BF16),
        scratch_shapes=[pltpu.VMEM((2 * KV_HEADS, past, LANES), BF16)] * 2,
        compiler_params=_cparams(("parallel", "arbitrary")),
        name="attn_lat",
    )(sink, q, kd, kd, kd, vd, vd, vd, ck, cv)


def _scan_kernel(lre_ref, lim_ref, ldt_ref, bre_ref, bim_ref, cre_ref, cim_ref, *rest, emit_y, chain):
    rest = list(rest)
    if chain:
        s0_ref, ez_ref = rest[:2]
        rest = rest[2:]
    suf_ref, sub_ref = rest[:2]
    rest = rest[2:]
    if emit_y:
        y_refs = rest[:2]
        rest = rest[2:]
    fin_ref, a_scr, bb_scr, h_scr, bu_scr, u_scr = rest[:6]
    if emit_y:
        y_scr, ct_scr = rest[6:8]
    i = pl.program_id(0)
    tt = SCAN_TT
    n = SSM_N

    @pl.when(i == 0)
    def _prologue():
        row_g = lax.shift_right_logical(lax.broadcasted_iota(jnp.int32, (SSM_DIM, n), 0), 4)
        col_g = lax.shift_right_logical(lax.broadcasted_iota(jnp.int32, (SSM_DIM, n), 1), 6)
        own = row_g == col_g

        def blockdiag(ref, d):
            return jnp.where(own, jnp.concatenate([ref[d]] * SSM_GROUPS, axis=0), 0.0)

        for d in range(2):
            lr, li = lre_ref[d], lim_ref[d]
            dt = jnp.exp(ldt_ref[d])
            mag = jnp.exp(lr * dt)
            ar, ai = mag * jnp.cos(li * dt), mag * jnp.sin(li * dt)
            den = lr * lr + li * li
            fr = ((ar - 1.0) * lr + ai * li) / den
            fi = (ai * lr - (ar - 1.0) * li) / den
            bre, bim = blockdiag(bre_ref, d), blockdiag(bim_ref, d)
            bb_scr[d, :, 0:n] = (fr * bre - fi * bim).astype(BF16)
            bb_scr[d, :, n:2 * n] = (fr * bim + fi * bre).astype(BF16)
            if emit_y:
                ct_scr[d, :, 0:n] = blockdiag(cre_ref, d).astype(BF16)
                ct_scr[d, :, n:2 * n] = (-blockdiag(cim_ref, d)).astype(BF16)
            a_scr[d, 0] = jnp.broadcast_to(ar, (8, n))
            a_scr[d, 1] = jnp.broadcast_to(ai, (8, n))
            if chain:
                pr, pi_ = ar, ai
                for _ in range(8):
                    pr, pi_ = pr * pr - pi_ * pi_, 2.0 * pr * pi_
                chunks = SCAN_ROWS // 2
                for b in range(2):
                    hr = s0_ref[d, b:b + 1, 0:n]
                    hi = s0_ref[d, b:b + 1, n:2 * n]
                    order = range(chunks) if d == 0 else range(chunks - 1, -1, -1)
                    for c in order:
                        rw = b * chunks + c
                        h_scr[d, rw:rw + 1, 0:n] = hr
                        h_scr[d, rw:rw + 1, n:2 * n] = hi
                        er = ez_ref[d, rw:rw + 1, 0:n]
                        ei = ez_ref[d, rw:rw + 1, n:2 * n]
                        hr, hi = pr * hr - pi_ * hi + er, pr * hi + pi_ * hr + ei
            else:
                h_scr[d] = jnp.zeros((SCAN_ROWS, 2 * n), F32)

    ntile = n // LANES
    group = SCAN_LW // LANES

    def project_in(d):
        su_ref = suf_ref if d == 0 else sub_ref
        for r in range(SCAN_ROWS):
            for sl in range(SSM_DIM // LANES):
                col = r * SSM_DIM + sl * LANES
                u_scr[d, sl, pl.ds(r, tt, stride=SCAN_ROWS), :] = su_ref[:, col:col + LANES]
        for r0 in range(0, tt * SCAN_ROWS, SCAN_MM_ROWS):
            rows = slice(r0, r0 + SCAN_MM_ROWS)
            u = jnp.concatenate([u_scr[d, sl, rows, :] for sl in range(SSM_DIM // LANES)], axis=1).astype(BF16)
            bu_scr[d, rows, :] = _dot(u, bb_scr[d])

    def recur(d):
        for c0 in range(0, ntile, group):
            lre = slice(c0 * LANES, (c0 + group) * LANES)
            lim = slice(n + c0 * LANES, n + (c0 + group) * LANES)
            ar, ai = a_scr[d, 0, :, lre], a_scr[d, 1, :, lre]
            halves = [slice(0, 8), slice(8, 16)]
            hr = [h_scr[d, hs, lre] for hs in halves]
            hi = [h_scr[d, hs, lim] for hs in halves]
            steps = range(tt) if d == 0 else range(tt - 1, -1, -1)
            for s in steps:
                for k in range(2):
                    rows = slice(s * SCAN_ROWS + 8 * k, s * SCAN_ROWS + 8 * k + 8)
                    br, bi = bu_scr[d, rows, lre], bu_scr[d, rows, lim]
                    hr[k], hi[k] = ar * hr[k] - ai * hi[k] + br, ar * hi[k] + ai * hr[k] + bi
                    if emit_y:
                        bu_scr[d, rows, lre] = hr[k]
                        bu_scr[d, rows, lim] = hi[k]
            for k, hs in enumerate(halves):
                h_scr[d, hs, lre] = hr[k]
                h_scr[d, hs, lim] = hi[k]

    def project_out(d):
        for r0 in range(0, tt * SCAN_ROWS, SCAN_MM_ROWS):
            rows = slice(r0, r0 + SCAN_MM_ROWS)
            y = _dot_t(bu_scr[d, rows, :].astype(BF16), ct_scr[d])
            for sl in range(SSM_DIM // LANES):
                y_scr[d, sl, rows, :] = y[:, sl * LANES:(sl + 1) * LANES]
        for r in range(SCAN_ROWS):
            for sl in range(SSM_DIM // LANES):
                col = r * SSM_DIM + sl * LANES
                y_refs[d][:, col:col + LANES] = y_scr[d, sl, pl.ds(r, tt, stride=SCAN_ROWS), :].astype(BF16)

    project_in(0)
    project_in(1)
    recur(0)
    if emit_y:
        project_out(0)
    recur(1)
    if emit_y:
        project_out(1)

    @pl.when(i == pl.num_programs(0) - 1)
    def _final():
        fin_ref[...] = h_scr[...]


def _scan(prm, l, su_tm, emit_y, s0=None, ez=None):
    chain = s0 is not None
    nt = SCAN_LEN // SCAN_TT
    n = SSM_N
    full3 = lambda shape: pl.BlockSpec(shape, lambda i: (0, 0, 0))
    in_specs = [_layer_spec((2, 1, n), l)] * 3 + [_layer_spec((2, SSM_CH, n), l)] * 4
    args = [prm["lam_re"], prm["lam_im"], prm["log_dt"], prm["b_re"], prm["b_im"], prm["c_re"], prm["c_im"]]
    if chain:
        in_specs += [full3((2, 2, 2 * n)), full3((2, SCAN_ROWS, 2 * n))]
        args += [s0, ez]
    tblk = (SCAN_TT, SCAN_ROWS * SSM_DIM)
    fwd = lambda i: (i, 0)
    bwd = lambda i: (nt - 1 - i, 0)
    in_specs += [pl.BlockSpec(tblk, fwd), pl.BlockSpec(tblk, bwd)]
    args += [su_tm, su_tm]
    out_shape, out_specs = [], []
    scratch = [
        pltpu.VMEM((2, 2, 8, n), F32),
        pltpu.VMEM((2, SSM_DIM, 2 * n), BF16),
        pltpu.VMEM((2, SCAN_ROWS, 2 * n), F32),
        pltpu.VMEM((2, SCAN_ROWS * SCAN_TT, 2 * n), F32),
        pltpu.VMEM((2, SSM_DIM // LANES, SCAN_ROWS * SCAN_TT, LANES), F32),
    ]
    if emit_y:
        yshape = jax.ShapeDtypeStruct((SCAN_LEN, SCAN_ROWS * SSM_DIM), BF16)
        out_shape += [yshape, yshape]
        out_specs += [pl.BlockSpec(tblk, fwd), pl.BlockSpec(tblk, bwd)]
        scratch.append(pltpu.VMEM((2, SSM_DIM // LANES, SCAN_ROWS * SCAN_TT, LANES), F32))
        scratch.append(pltpu.VMEM((2, SSM_DIM, 2 * n), BF16))
    out_shape.append(jax.ShapeDtypeStruct((2, SCAN_ROWS, 2 * n), F32))
    out_specs.append(full3((2, SCAN_ROWS, 2 * n)))
    return pl.pallas_call(
        functools.partial(_scan_kernel, emit_y=emit_y, chain=chain),
        grid=(nt,),
        in_specs=in_specs,
        out_specs=out_specs,
        out_shape=out_shape,
        scratch_shapes=scratch,
        compiler_params=_cparams(("arbitrary",)),
        name="scan_chain" if chain else ("scan_y" if emit_y else "scan_state"),
    )(*args)


def _shifted(scr_ref, off, rows, pos, seq_len, w):
    mid = scr_ref[off:off + rows, :]
    up = jnp.where(pos != 0, scr_ref[off - 1:off - 1 + rows, :], 0.0)
    dn = jnp.where(pos != seq_len - 1, scr_ref[off + 1:off + 1 + rows, :], 0.0)
    return w[0:1, :] * up + w[1:2, :] * mid + w[2:3, :] * dn


def _mix_kernel(x_ref, mod_ref, attn_ref, g3_ref, *rest, halo, tm, seq_len):
    rest = list(rest)
    if halo:
        gp_ref, gn_ref = rest[:2]
        rest = rest[2:]
    (su_ref, yf_ref, yb_ref, wc_ref, dsk_ref, wglu_ref, wo_ref, gn2_ref,
     x1_ref, h2_ref, z_scr, wglu_scr, wo_scr) = rest
    i = pl.program_id(0)

    @pl.when(i == 0)
    def _cast_weights():
        wglu_scr[...] = wglu_ref[...].astype(BF16)
        wo_scr[...] = wo_ref[...].astype(BF16)

    def rows_of(ref):
        pieces = [ref[:, r * SSM_DIM:(r + 1) * SSM_DIM].astype(F32) for r in range(tm // SCAN_LEN)]
        return pieces[0] if len(pieces) == 1 else jnp.concatenate(pieces, axis=0)

    m = mod_ref[...]
    g1, sh2, sc2 = m[:, 2 * D:3 * D], m[:, 3 * D:4 * D], m[:, 4 * D:5 * D]
    pos = (i * tm + lax.broadcasted_iota(jnp.int32, (tm, 1), 0)) % seq_len

    def gate_prod(ref):
        return ref[:, SC_DIM:2 * SC_DIM].astype(F32) * ref[:, 2 * SC_DIM:3 * SC_DIM].astype(F32)

    gb = g3_ref[:, 0:SC_DIM].astype(F32)
    z_scr[16:16 + tm, :] = gate_prod(g3_ref)
    if halo:
        z_scr[0:16, :] = gate_prod(gp_ref)
        z_scr[16 + tm:32 + tm, :] = gate_prod(gn_ref)
    else:
        z_scr[0:16, :] = jnp.zeros((16, SC_DIM), F32)
        z_scr[16 + tm:32 + tm, :] = jnp.zeros((16, SC_DIM), F32)
    conv = gb * _shifted(z_scr, 16, tm, pos, seq_len, wc_ref[...])

    y = dsk_ref[...] * rows_of(su_ref) + rows_of(yf_ref) + rows_of(yb_ref)
    zz = 0.5 * y * (1.0 + jnp.tanh(math.sqrt(2.0 / math.pi) * (y + 0.044715 * (y * y * y))))
    ssm = zz * _sigmoid(_dot(zz.astype(BF16), wglu_scr[...]))

    mix = (_dot(attn_ref[...], wo_scr[0:ATTN_DIM, :])
           + _dot(conv.astype(BF16), wo_scr[ATTN_DIM:ATTN_DIM + SC_DIM, :])
           + _dot(ssm.astype(BF16), wo_scr[ATTN_DIM + SC_DIM:, :]))
    x1 = x_ref[...] + g1 * mix
    x1_ref[...] = x1
    h2_ref[...] = _rms_mod(x1, gn2_ref[...], sc2, sh2).astype(BF16)


def _mix(x, mods, attn, g3, su, yf, yb, prm, l, lat, tokens_per_mod, seq_len, tm=512):
    t = x.shape[0]
    halo = seq_len > tm
    tiles_per_mod = tokens_per_mod // tm
    mod_map = (lambda i: (l, 1 + i // tiles_per_mod, 0, 0)) if lat else (lambda i: (l, 0, 0, 0))
    row = lambda i: (i, 0)
    in_specs = [pl.BlockSpec((tm, D), row), pl.BlockSpec((None, None, 1, 6 * D), mod_map),
                pl.BlockSpec((tm, 512), row), pl.BlockSpec((tm, 768), row)]
    args = [x, mods, attn, g3]
    if halo:
        r16 = tm // 16
        in_specs += [pl.BlockSpec((16, 768), lambda i: (jnp.maximum(i * r16 - 1, 0), 0)),
                     pl.BlockSpec((16, 768), lambda i: (jnp.minimum((i + 1) * r16, t // 16 - 1), 0))]
        args += [g3, g3]
    assert tm % SCAN_LEN == 0
    in_specs += [pl.BlockSpec((SCAN_LEN, tm // SCAN_LEN * SSM_DIM), lambda i: (0, i))] * 3
    args += [su, yf, yb]
    in_specs += [_layer_spec((3, SC_DIM), l), _layer_spec((1, SSM_DIM), l), _layer_spec((SSM_DIM, SSM_DIM), l),
                 _layer_spec((D, D), l), _layer_spec((1, D), l)]
    args += [prm["sc_conv"], prm["ssm_d"], prm["w_glu"], prm["w_out"], prm["norm_ffn"]]
    return pl.pallas_call(
        functools.partial(_mix_kernel, halo=halo, tm=tm, seq_len=seq_len),
        grid=(t // tm,),
        in_specs=in_specs,
        out_specs=[pl.BlockSpec((tm, D), row), pl.BlockSpec((tm, D), row)],
        out_shape=[jax.ShapeDtypeStruct((t, D), F32), jax.ShapeDtypeStruct((t, D), BF16)],
        scratch_shapes=[pltpu.VMEM((tm + 32, SC_DIM), F32), pltpu.VMEM((SSM_DIM, SSM_DIM), BF16),
                        pltpu.VMEM((D, D), BF16)],
        compiler_params=_cparams(("arbitrary",)),
        name="mix_lat" if lat else "mix_ctx",
    )(*args)


def _ffn_kernel(h_ref, *rest, halo, tm, seq_len, final):
    rest = list(rest)
    if halo:
        hp_ref, hn_ref = rest[:2]
        rest = rest[2:]
    x1_ref, mod_ref, wa_ref, wg_ref, ca_ref, cg_ref, wd_ref = rest[:7]
    rest = rest[7:]
    if final:
        gf_ref = rest[0]
        rest = rest[1:]
    o_ref, acc_scr = rest[:2]
    rest = rest[2:]
    nsub = len(FFN_SUB)
    ua_scrs, ug_scrs = rest[:nsub], rest[nsub:2 * nsub]
    i = pl.program_id(0)
    j = pl.program_id(1)

    if halo:
        hcat_scr = rest[2 * nsub]

        @pl.when(j == 0)
        def _stage():
            hcat_scr[0:16, :] = hp_ref[...]
            hcat_scr[16:16 + tm, :] = h_ref[...]
            hcat_scr[16 + tm:32 + tm, :] = hn_ref[...]

        hh = hcat_scr[...]
        pad, seg_len, nseg = 16, tm, 1
        tiles_per_seq = seq_len // tm
        keep_prev = (i % tiles_per_seq != 0).astype(F32)
        keep_next = (i % tiles_per_seq != tiles_per_seq - 1).astype(F32)
    else:
        hh = h_ref[...]
        pad, seg_len, nseg = 8, seq_len, tm // seq_len
    seg_rows = seg_len + 2 * pad
    cols = [sum(FFN_SUB[:c]) for c in range(nsub)]

    def up(c):
        width = FFN_SUB[c]
        for u_scr, w_ref in ((ua_scrs[c], wa_ref), (ug_scrs[c], wg_ref)):
            u = _dot(hh, w_ref[:, cols[c]:cols[c] + width])
            if halo:
                u_scr[...] = u
                u_scr[8:16, :] = u_scr[8:16, :] * keep_prev
                u_scr[16 + tm:24 + tm, :] = u_scr[16 + tm:24 + tm, :] * keep_next
            else:
                for sg in range(nseg):
                    r0 = sg * seg_rows
                    u_scr[r0:r0 + pad, :] = jnp.zeros((pad, width), F32)
                    u_scr[r0 + pad:r0 + pad + seg_len, :] = u[sg * seg_len:(sg + 1) * seg_len]
                    u_scr[r0 + pad + seg_len:r0 + seg_rows, :] = jnp.zeros((pad, width), F32)

    def conv(c, u_scr, cw_ref):
        w = cw_ref[:, cols[c]:cols[c] + FFN_SUB[c]]
        segs = []
        for sg in range(nseg):
            r0 = sg * seg_rows + pad
            segs.append(w[0:1, :] * u_scr[r0 - 1:r0 - 1 + seg_len, :]
                        + w[1:2, :] * u_scr[r0:r0 + seg_len, :]
                        + w[2:3, :] * u_scr[r0 + 1:r0 + 1 + seg_len, :])
        return segs[0] if nseg == 1 else jnp.concatenate(segs, axis=0)

    def down(c):
        a = conv(c, ua_scrs[c], ca_ref)
        g = conv(c, ug_scrs[c], cg_ref)
        act = (a * (g * _sigmoid(g))).astype(BF16)
        acc_scr[...] += _dot(act, wd_ref[cols[c]:cols[c] + FFN_SUB[c], :])

    @pl.when(j == 0)
    def _init():
        acc_scr[...] = jnp.zeros((tm, D), F32)

    up(0)
    for c in range(nsub):
        if c + 1 < nsub:
            up(c + 1)
        down(c)

    @pl.when(j == pl.num_programs(1) - 1)
    def _finish():
        g2 = mod_ref[...][:, 5 * D:6 * D]
        x2 = x1_ref[...] + g2 * acc_scr[...]
        if final:
            x2 = x2 * lax.rsqrt(jnp.mean(x2 * x2, axis=-1, keepdims=True) + RMS_EPS) * gf_ref[...]
        o_ref[...] = x2


def _ffn(h2, x1, mods, prm, l, lat, tokens_per_mod, seq_len, final_g, tm=512):
    t = h2.shape[0]
    tf = FFN_TF
    halo = seq_len > tm
    final = final_g is not None
    nj = D_FF // tf
    tiles_per_mod = tokens_per_mod // tm
    mod_map = (lambda i, j: (l, 1 + i // tiles_per_mod, 0, 0)) if lat else (lambda i, j: (l, 0, 0, 0))
    row = lambda i, j: (i, 0)
    in_specs = [pl.BlockSpec((tm, D), row)]
    args = [h2]
    if halo:
        r16 = tm // 16
        in_specs += [pl.BlockSpec((16, D), lambda i, j: (jnp.maximum(i * r16 - 1, 0), 0)),
                     pl.BlockSpec((16, D), lambda i, j: (jnp.minimum((i + 1) * r16, t // 16 - 1), 0))]
        args += [h2, h2]
    in_specs += [pl.BlockSpec((tm, D), row), pl.BlockSpec((None, None, 1, 6 * D), mod_map),
                 pl.BlockSpec((None, D, tf), lambda i, j: (l, 0, j)),
                 pl.BlockSpec((None, D, tf), lambda i, j: (l, 0, j + nj)),
                 pl.BlockSpec((None, 3, tf), lambda i, j: (l, 0, j)),
                 pl.BlockSpec((None, 3, tf), lambda i, j: (l, 0, j + nj)),
                 pl.BlockSpec((None, tf, D), lambda i, j: (l, j, 0))]
    args += [x1, mods, prm["w_up"], prm["w_up"], prm["ffn_conv"], prm["ffn_conv"], prm["w_down"]]
    if final:
        in_specs.append(pl.BlockSpec((1, D), lambda i, j: (0, 0)))
        args.append(final_g)
    urows = tm + 32 if halo else (tm // seq_len) * (seq_len + 16)
    scratch = [pltpu.VMEM((tm, D), F32)]
    scratch += [pltpu.VMEM((urows, w), F32) for w in FFN_SUB] * 2
    if halo:
        scratch.append(pltpu.VMEM((tm + 32, D), BF16))
    return pl.pallas_call(
        functools.partial(_ffn_kernel, halo=halo, tm=tm, seq_len=seq_len, final=final),
        grid=(t // tm, nj),
        in_specs=in_specs,
        out_specs=pl.BlockSpec((tm, D), row),
        out_shape=jax.ShapeDtypeStruct((t, D), F32),
        scratch_shapes=scratch,
        compiler_params=_cparams(("parallel", "arbitrary")),
        name="ffn_lat" if lat else "ffn_ctx",
    )(*args)


def _rope_tables(seq_len):
    rows = seq_len // GRID_W
    row = np.repeat(np.arange(rows, dtype=np.float32), GRID_W)
    col = np.tile(np.arange(GRID_W, dtype=np.float32), rows)
    freqs = (np.float32(ROPE_BASE) ** (-np.arange(ROPE_FREQS, dtype=np.float32) / np.float32(ROPE_FREQS)))
    freqs = freqs.astype(np.float32)
    ang = np.stack([row[:, None] * freqs, col[:, None] * freqs], axis=1).astype(np.float32)
    cos, sin = np.cos(ang).astype(np.float32), np.sin(ang).astype(np.float32)
    zero = np.zeros_like(sin)
    c64 = np.stack([cos, cos], axis=2).reshape(seq_len, HEAD_DIM)
    sneg64 = np.stack([-sin, zero], axis=2).reshape(seq_len, HEAD_DIM)
    spos64 = np.stack([zero, sin], axis=2).reshape(seq_len, HEAD_DIM)
    return tuple(jnp.asarray(np.tile(tb, (1, 2))) for tb in (c64, sneg64, spos64))


def _prep_params(w_in, w_out, norm_mix, norm_ffn, sc_conv, ssm_lam_re, ssm_lam_im, ssm_log_dt, ssm_b_re,
                 ssm_b_im, ssm_c_re, ssm_c_im, ssm_d, ssm_w_glu, ffn_w_up, ffn_conv, ffn_w_down):
    b_slab = lambda b: jnp.transpose(b, (0, 1, 4, 2, 3)).reshape(DEPTH, 2, SSM_CH, SSM_N)
    c_slab = lambda c: jnp.transpose(c, (0, 1, 3, 2, 4)).reshape(DEPTH, 2, SSM_CH, SSM_N)
    return {
        "w_in": w_in,
        "norm_mix": norm_mix.reshape(DEPTH, 1, D),
        "norm_ffn": norm_ffn.reshape(DEPTH, 1, D),
        "sc_conv": jnp.transpose(sc_conv, (0, 2, 1)),
        "ssm_d": ssm_d.reshape(DEPTH, 1, SSM_DIM),
        "w_glu": ssm_w_glu,
        "w_out": w_out,
        "w_up": ffn_w_up.astype(BF16),
        "ffn_conv": jnp.transpose(ffn_conv, (0, 2, 1)),
        "w_down": ffn_w_down.astype(BF16),
        "lam_re": ssm_lam_re.reshape(DEPTH, 2, 1, SSM_N),
        "lam_im": ssm_lam_im.reshape(DEPTH, 2, 1, SSM_N),
        "log_dt": jnp.repeat(ssm_log_dt, SSM_STATE, axis=-1).reshape(DEPTH, 2, 1, SSM_N),
        "b_re": b_slab(ssm_b_re),
        "b_im": b_slab(ssm_b_im),
        "c_re": c_slab(ssm_c_re),
        "c_im": c_slab(ssm_c_im),
    }


def kernel(x_prompt, x_sample, cache_k, cache_v, state_ssm_re, state_ssm_im, c, c_ctx, norm_mix, norm_ffn, norm_final, w_ada, b_ada, w_in, w_out, attn_sink, sc_conv, ssm_lam_re, ssm_lam_im, ssm_log_dt, ssm_b_re, ssm_b_im, ssm_c_re, ssm_c_im, ssm_d, ssm_w_glu, ffn_w_up, ffn_conv, ffn_w_down):
    batch, seq = x_prompt.shape[0], x_prompt.shape[1]
    dec_batch, dec_seq = x_sample.shape[0], x_sample.shape[1]
    assert batch == SCAN_ROWS and seq == SCAN_LEN
    assert dec_batch * (dec_seq // SCAN_LEN) == SCAN_ROWS and dec_batch == 2

    cs = jnp.concatenate([c_ctx[None, :], c, jnp.zeros((8 - 1 - dec_batch, D), F32)], axis=0)
    mods = _adaln(cs, w_ada, b_ada).reshape(DEPTH, 8, 1, 6 * D)
    rope_tabs = _rope_tables(dec_seq)
    gfin = norm_final.reshape(1, D)
    prm = _prep_params(w_in, w_out, norm_mix, norm_ffn, sc_conv, ssm_lam_re, ssm_lam_im, ssm_log_dt, ssm_b_re,
                       ssm_b_im, ssm_c_re, ssm_c_im, ssm_d, ssm_w_glu, ffn_w_up, ffn_conv, ffn_w_down)
    s0_all = jnp.transpose(jnp.concatenate([state_ssm_re.reshape(dec_batch, DEPTH, 2, SSM_N),
                                            state_ssm_im.reshape(dec_batch, DEPTH, 2, SSM_N)], axis=-1),
                           (1, 2, 0, 3))

    xp = x_prompt.reshape(batch * seq, D)
    xs = x_sample.reshape(dec_batch * dec_seq, D)
    kv_out, fin_out = [], []
    for l in range(DEPTH):
        last = gfin if l == DEPTH - 1 else None

        q, kd, vd, kv, g3, su = _inproj(xp, mods, prm, l, None, batch * seq, seq)
        attn = _attn_ctx(attn_sink, l, q, kd, vd, seq)
        yf, yb, fin = _scan(prm, l, su, True)
        x1, h2 = _mix(xp, mods, attn, g3, su, yf, yb, prm, l, False, batch * seq, seq)
        xp = _ffn(h2, x1, mods, prm, l, False, batch * seq, seq, last)
        kv_out.append(kv)
        fin_out.append(fin)

        q, kd, vd, g3, su = _inproj(xs, mods, prm, l, rope_tabs, dec_seq, dec_seq)
        attn = _attn_lat(attn_sink, l, q, kd, vd, cache_k, cache_v, dec_batch, dec_seq)
        (ez,) = _scan(prm, l, su, False)
        yf, yb, _ = _scan(prm, l, su, True, s0=s0_all[l], ez=ez)
        x1, h2 = _mix(xs, mods, attn, g3, su, yf, yb, prm, l, True, dec_seq, dec_seq)
        xs = _ffn(h2, x1, mods, prm, l, True, dec_seq, dec_seq, last)

    kv_all = jnp.stack(kv_out, axis=0).reshape(DEPTH, batch, seq, 2, KV_HEADS, HEAD_DIM)
    kv_all = jnp.transpose(kv_all, (3, 1, 0, 2, 4, 5))
    fin_all = jnp.stack(fin_out, axis=0).reshape(DEPTH, 2, batch, 2, SSM_GROUPS, SSM_STATE)
    fin_all = jnp.transpose(fin_all, (3, 2, 0, 1, 4, 5))
    return (xp.reshape(batch, seq, D), xs.reshape(dec_batch, dec_seq, D),
            kv_all[0], kv_all[1], fin_all[0], fin_all[1])
```
